```python
import math
import jax
import jax.numpy as jnp
from jax import lax
import numpy as np

D_MODEL = 2048
BATCH = 2
SEQ = 16384
DEPTH = 4

GRID_W = 64
CTX_LEN = 256
N_MIXERS = 3
N_LAYERS_A = (DEPTH + 2) // N_MIXERS
N_LAYERS_B = (DEPTH + 1) // N_MIXERS
N_LAYERS_C = DEPTH // N_MIXERS
NORM_EPS = 1e-6
ROPE_THETA = 10000.0

SSM_D_INNER = 2 * D_MODEL
SSM_HEAD_DIM = 64
SSM_HEADS = SSM_D_INNER // SSM_HEAD_DIM
SSM_STATE = 128
SSM_GROUPS = 8
SSM_CONV = 3
SSM_CHUNK = 128
SSM_CONV_DIM = SSM_D_INNER + 2 * SSM_GROUPS * SSM_STATE
SSM_IN_DIM = SSM_D_INNER + SSM_CONV_DIM + 2 * SSM_HEADS

HEAD_DIM = 128
WIN_HEADS = D_MODEL // HEAD_DIM
WIN_KV_HEADS = 4
WINDOW = 128
BLOCK = 128
DIFF_HEADS = D_MODEL // (2 * HEAD_DIM)

MOE_GROUPS = 4
MOE_EXPERTS_PER_GROUP = 8
MOE_EXPERTS = MOE_GROUPS * MOE_EXPERTS_PER_GROUP
MOE_TOP_K = 2
MOE_D_FF = D_MODEL // 4
MOE_BLOCK = 128

F32 = jnp.float32

kernel_name = "hybrid_ssd_swa_diffattn_hmoe_dit"


def rms_norm(x, gain):
    xf = x.astype(F32)
    y = xf * lax.rsqrt(jnp.mean(xf * xf, axis=-1, keepdims=True) + NORM_EPS)
    return (y * gain.astype(F32)).astype(x.dtype)


def modulate(h, shift, scale):
    return h * (1.0 + scale) + shift


def modulation(cvec, w, b):
    return jnp.split(jax.nn.silu(cvec) @ w + b, 6, axis=-1)


def axial_rope(n_tokens, head_dim):
    rows = n_tokens // GRID_W
    pos = jnp.arange(rows * GRID_W)
    row = (pos // GRID_W).astype(F32)
    col = (pos % GRID_W).astype(F32)
    quarter = head_dim // 4
    inv_freq = ROPE_THETA ** (-jnp.arange(quarter, dtype=F32) / quarter)
    ang = jnp.stack([row[:, None] * inv_freq, col[:, None] * inv_freq], axis=1)
    return jnp.cos(ang), jnp.sin(ang)


def apply_rope(x, cos, sin):
    b, s, h, d = x.shape
    xr = x.astype(F32).reshape(b, s, h, 2, 2, d // 4)
    x1, x2 = xr[..., 0, :], xr[..., 1, :]
    cs, sn = cos[None, :, None], sin[None, :, None]
    out = jnp.stack([x1 * cs - x2 * sn, x2 * cs + x1 * sn], axis=-2)
    return out.reshape(b, s, h, d).astype(x.dtype)


def centred_dwconv(u, w, bias):
    out = lax.conv_general_dilated(
        u, w[:, None, :].astype(u.dtype), window_strides=(1,),
        padding=[(SSM_CONV // 2, SSM_CONV // 2)],
        dimension_numbers=("NWC", "WIO", "NWC"), feature_group_count=u.shape[-1])
    return out + bias


def ssd_scan(xs, dt, a, bm, cm, h0):
    b, l, h, p = xs.shape
    nc = l // SSM_CHUNK
    r = h // SSM_GROUPS

    def chunks(t, *tail):
        return jnp.moveaxis(t.astype(F32).reshape(b, nc, SSM_CHUNK, *tail), 1, 0)

    xq = chunks(xs, SSM_GROUPS, r, p)
    laq = chunks(dt * a, SSM_GROUPS, r)
    dtq = chunks(dt, SSM_GROUPS, r)
    bq = chunks(bm, SSM_GROUPS, SSM_STATE)
    cq = chunks(cm, SSM_GROUPS, SSM_STATE)
    tri = jnp.tril(jnp.ones((SSM_CHUNK, SSM_CHUNK), dtype=bool))

    def step(state, inp):
        x_, la, dt_, b_, c_ = inp
        cum = jnp.moveaxis(jnp.cumsum(la, axis=1), 1, -1)
        seg = cum[..., :, None] - cum[..., None, :]
        decay = jnp.where(tri, jnp.exp(jnp.where(tri, seg, 0.0)), 0.0)
        dt_t = jnp.moveaxis(dt_, 1, -1)
        cb = jnp.einsum("bign,bjgn->bgij", c_, b_)
        w = cb[:, :, None] * decay * dt_t[..., None, :]
        y = jnp.einsum("bgrij,bjgrp->bigrp", w, x_)
        y = y + jnp.einsum("bign,bgrpn->bigrp", c_, state) * jnp.exp(jnp.moveaxis(cum, -1, 1))[..., None]
        last = cum[..., -1:]
        w_state = jnp.exp(last - cum) * dt_t
        new_state = jnp.exp(last)[..., None] * state + jnp.einsum("bjgn,bgrj,bjgrp->bgrpn", b_, w_state, x_)
        return new_state, y

    state, ys = lax.scan(step, h0, (xq, laq, dtq, bq, cq))
    return jnp.moveaxis(ys, 0, 1).reshape(b, l, h, p), state


def ssd_mixer(hx, hc, w_in, conv_w, conv_b, a_log, dt_bias, d_skip, norm_g, w_out, need_ctx):
    def project(h):
        b, l, _ = h.shape
        z, xbc, dt = jnp.split(h @ w_in, [SSM_D_INNER, SSM_D_INNER + SSM_CONV_DIM], axis=-1)
        xbc = jax.nn.silu(centred_dwconv(xbc, conv_w, conv_b))
        xs, bm, cm = jnp.split(xbc, [SSM_D_INNER, SSM_D_INNER + SSM_GROUPS * SSM_STATE], axis=-1)
        dt = jax.nn.softplus(dt.astype(F32).reshape(b, l, 2, SSM_HEADS) + dt_bias.astype(F32))
        return (z, xs.reshape(b, l, SSM_HEADS, SSM_HEAD_DIM),
                bm.reshape(b, l, SSM_GROUPS, SSM_STATE), cm.reshape(b, l, SSM_GROUPS, SSM_STATE), dt)

    zx, xx, bx, cx, dtx = project(hx)
    zc, xc, bc, cc, dtc = project(hc)
    a = -jnp.exp(a_log.astype(F32))
    dsk = d_skip.astype(F32)[:, None]
    y_lat = dsk * xx.astype(F32)
    y_ctx = dsk * xc.astype(F32)
    bsz = hx.shape[0]
    for direction in range(2):
        rev = direction == 1
        f = (lambda t: jnp.flip(t, axis=1)) if rev else (lambda t: t)
        h0 = jnp.zeros((bsz, SSM_GROUPS, SSM_HEADS // SSM_GROUPS, SSM_HEAD_DIM, SSM_STATE), F32)
        yc_d, hc_final = ssd_scan(f(xc), f(dtc[:, :, direction]), a[direction], f(bc), f(cc), h0)
        yx_d, _ = ssd_scan(f(xx), f(dtx[:, :, direction]), a[direction], f(bx), f(cx), hc_final)
        y_lat = y_lat + f(yx_d)
        y_ctx = y_ctx + f(yc_d)

    def finish(y, z):
        b, l = y.shape[:2]
        y = y.reshape(b, l, SSM_D_INNER).astype(z.dtype)
        return rms_norm(y * jax.nn.silu(z), norm_g) @ w_out

    return finish(y_lat, zx), (finish(y_ctx, zc) if need_ctx else None)


def window_gqa_mixer(hx, hc, w_qkv, q_g, k_g, sinks, w_out, cos, sin, need_ctx):
    grp = WIN_HEADS // WIN_KV_HEADS

    def qkv(h):
        b, l, _ = h.shape
        q, k, v = jnp.split(h @ w_qkv, [WIN_HEADS * HEAD_DIM, (WIN_HEADS + WIN_KV_HEADS) * HEAD_DIM], axis=-1)
        q = rms_norm(q.reshape(b, l, WIN_HEADS, HEAD_DIM), q_g)
        k = rms_norm(k.reshape(b, l, WIN_KV_HEADS, HEAD_DIM), k_g)
        return q, k, v.reshape(b, l, WIN_KV_HEADS, HEAD_DIM)

    qx, kx, vx = qkv(hx)
    qc, kc, vc = qkv(hc)
    qx = apply_rope(qx, cos, sin)
    kx = apply_rope(kx, cos, sin)
    b, s = hx.shape[:2]
    n_ctx = hc.shape[1]
    nb = s // BLOCK
    scale = HEAD_DIM ** -0.5
    sink = sinks.astype(F32).reshape(WIN_KV_HEADS, grp)[None, :, :, None, None]

    kpad = jnp.pad(kx, ((0, 0), (BLOCK, BLOCK), (0, 0), (0, 0)))
    vpad = jnp.pad(vx, ((0, 0), (BLOCK, BLOCK), (0, 0), (0, 0)))
    qblocks = jnp.moveaxis(qx.reshape(b, nb, BLOCK, WIN_KV_HEADS, grp, HEAD_DIM), 1, 0)
    rel = (jnp.arange(3 * BLOCK) - BLOCK)[None, :] - jnp.arange(BLOCK)[:, None]
    in_band = jnp.abs(rel) <= WINDOW

    def block_fn(args):
        q, n = args
        start = n * BLOCK
        k = lax.dynamic_slice_in_dim(kpad, start, 3 * BLOCK, axis=1)
        v = lax.dynamic_slice_in_dim(vpad, start, 3 * BLOCK, axis=1)
        kpos = start - BLOCK + jnp.arange(3 * BLOCK)
        mask = in_band & ((kpos >= 0) & (kpos < s))[None, :]
        s_lat = jnp.einsum("bqkgd,bjkd->bkgqj", q, k).astype(F32) * scale
        s_lat = jnp.where(mask, s_lat, -jnp.inf)
        s_ctx = jnp.einsum("bqkgd,bjkd->bkgqj", q, kc).astype(F32) * scale
        sink_b = jnp.broadcast_to(sink, (b, WIN_KV_HEADS, grp, BLOCK, 1))
        p = jax.nn.softmax(jnp.concatenate([s_lat, s_ctx, sink_b], axis=-1), axis=-1).astype(v.dtype)
        o = jnp.einsum("bkgqj,bjkd->bqkgd", p[..., :3 * BLOCK], v)
        return o + jnp.einsum("bkgqj,bjkd->bqkgd", p[..., 3 * BLOCK:3 * BLOCK + n_ctx], vc)

    o = lax.map(block_fn, (qblocks, jnp.arange(nb)))
    yx = jnp.moveaxis(o, 0, 1).reshape(b, s, WIN_HEADS * HEAD_DIM) @ w_out
    yc = None
    if need_ctx:
        qcb = qc.reshape(b, n_ctx, WIN_KV_HEADS, grp, HEAD_DIM)
        s_cc = jnp.einsum("bqkgd,bjkd->bkgqj", qcb, kc).astype(F32) * scale
        sink_c = jnp.broadcast_to(sink, (b, WIN_KV_HEADS, grp, n_ctx, 1))
        p = jax.nn.softmax(jnp.concatenate([s_cc, sink_c], axis=-1), axis=-1).astype(vc.dtype)
        oc = jnp.einsum("bkgqj,bjkd->bqkgd", p[..., :n_ctx], vc)
        yc = oc.reshape(b, n_ctx, WIN_HEADS * HEAD_DIM) @ w_out
    return yx, yc


def diff_attn_mixer(hx, hc, w_qkv, q_g, k_g, lam, subln_g, w_out, lam_init, cos, sin, need_ctx):
    def qkv(h):
        b, l, _ = h.shape
        q, k, v = jnp.split(h @ w_qkv, [2 * DIFF_HEADS * HEAD_DIM, 4 * DIFF_HEADS * HEAD_DIM], axis=-1)
        q = rms_norm(q.reshape(b, l, 2 * DIFF_HEADS, HEAD_DIM), q_g)
        k = rms_norm(k.reshape(b, l, 2 * DIFF_HEADS, HEAD_DIM), k_g)
        return q, k, v.reshape(b, l, DIFF_HEADS, 2 * HEAD_DIM)

    qx, kx, vx = qkv(hx)
    qc, kc, vc = qkv(hc)
    qx = apply_rope(qx, cos, sin)
    kx = apply_rope(kx, cos, sin)
    lf = lam.astype(F32)
    lam_full = jnp.exp(jnp.sum(lf[0] * lf[1])) - jnp.exp(jnp.sum(lf[2] * lf[3])) + lam_init
    scale = HEAD_DIM ** -0.5

    def attend(q, k, v):
        bq, nq = q.shape[:2]
        logits = jnp.einsum("bqhd,bjhd->bhqj", q, k).astype(F32) * scale
        p = jax.nn.softmax(logits, axis=-1).reshape(bq, DIFF_HEADS, 2, nq, k.shape[1])
        w = (p[:, :, 0] - lam_full * p[:, :, 1]).astype(v.dtype)
        o = jnp.einsum("bhqj,bjhe->bqhe", w, v)
        o = rms_norm(o, subln_g) * (1.0 - lam_init)
        return o.reshape(bq, nq, DIFF_HEADS * 2 * HEAD_DIM)

    b, s = hx.shape[:2]
    nb = s // BLOCK
    k_all = jnp.concatenate([kx, kc], axis=1)
    v_all = jnp.concatenate([vx, vc], axis=1)
    qblocks = jnp.moveaxis(qx.reshape(b, nb, BLOCK, 2 * DIFF_HEADS, HEAD_DIM), 1, 0)
    o = lax.map(lambda q: attend(q, k_all, v_all), qblocks)
    yx = jnp.moveaxis(o, 0, 1).reshape(b, s, D_MODEL) @ w_out
    yc = attend(qc, kc, vc) @ w_out if need_ctx else None
    return yx, yc


def hier_moe(t, w_group, b_group, w_expert, b_expert, w_gate, w_up, w_down):
    n, d = t.shape
    g_prob = jax.nn.softmax((t @ w_group).astype(F32) + b_group.astype(F32), axis=-1)
    g_p, g_sel = lax.top_k(g_prob, 1)
    e_logits = ((t @ w_expert).astype(F32) + b_expert.astype(F32)).reshape(n, MOE_GROUPS, MOE_EXPERTS_PER_GROUP)
    sel_idx = jnp.broadcast_to(g_sel[:, :, None], (n, 1, MOE_EXPERTS_PER_GROUP))
    e_logits = jnp.take_along_axis(e_logits, sel_idx, axis=1)[:, 0]
    e_p, e_sel = lax.top_k(jax.nn.softmax(e_logits, axis=-1), MOE_TOP_K)
    gate = g_p * e_p / jnp.sum(e_p, axis=-1, keepdims=True)
    expert = (g_sel * MOE_EXPERTS_PER_GROUP + e_sel).reshape(-1).astype(jnp.int32)

    n_assign = n * MOE_TOP_K
    a_tok = jnp.repeat(jnp.arange(n, dtype=jnp.int32), MOE_TOP_K)
    a_w = gate.reshape(-1)
    counts = jax.ops.segment_sum(jnp.ones_like(expert), expert, num_segments=MOE_EXPERTS)
    padded = (counts + MOE_BLOCK - 1) // MOE_BLOCK * MOE_BLOCK
    pad_end = jnp.cumsum(padded)
    pad_start = pad_end - padded
    start = jnp.cumsum(counts) - counts
    order = jnp.argsort(expert)
    sorted_exp = expert[order]
    dest = pad_start[sorted_exp] + jnp.arange(n_assign, dtype=jnp.int32) - start[sorted_exp]
    n_blocks = (n_assign + MOE_EXPERTS * (MOE_BLOCK - 1) + MOE_BLOCK - 1) // MOE_BLOCK
    slot_tok = jnp.full((n_blocks * MOE_BLOCK,), n, jnp.int32).at[dest].set(a_tok[order])
    slot_w = jnp.zeros((n_blocks * MOE_BLOCK,), F32).at[dest].set(a_w[order])
    block_starts = jnp.arange(n_blocks, dtype=jnp.int32) * MOE_BLOCK
    block_exp = jnp.minimum(jnp.searchsorted(pad_end, block_starts, side="right"), MOE_EXPERTS - 1)
    t_pad = jnp.concatenate([t, jnp.zeros((1, d), t.dtype)], axis=0)

    def expert_block(args):
        tok, e = args
        xb = t_pad[tok]
        hdn = jax.nn.silu(xb @ w_gate[e]) * (xb @ w_up[e])
        return hdn @ w_down[e]

    ys = lax.map(expert_block, (slot_tok.reshape(n_blocks, MOE_BLOCK), block_exp))
    ys = ys.reshape(-1, d) * slot_w.astype(t.dtype)[:, None]
    return jax.ops.segment_sum(ys, slot_tok, num_segments=n + 1)[:n]


def setup_inputs(seed: int = 0) -> dict:
    key = jax.random.key(seed)
    ks = iter(jax.random.split(key, 48))

    def nrm(shape, scale):
        return jax.random.normal(next(ks), shape, F32) * scale

    def gain(shape):
        return 1.0 + nrm(shape, 0.05)

    D = D_MODEL
    u_a = jax.random.uniform(next(ks), (N_LAYERS_A, 2, SSM_HEADS), F32, minval=1.0, maxval=16.0)
    u_dt = jax.random.uniform(next(ks), (N_LAYERS_A, 2, SSM_HEADS), F32)
    dt0 = jnp.exp(u_dt * (math.log(0.1) - math.log(1e-3)) + math.log(1e-3))
    return {
        "x": nrm((BATCH, SEQ, D), 1.0),
        "c": nrm((BATCH, D), 1.0),
        "ctx": nrm((BATCH, CTX_LEN, D), 1.0),
        "c_ctx": nrm((D,), 1.0),
        "ada_w": nrm((DEPTH, D, 6 * D), 0.5 * D ** -0.5),
        "ada_b": nrm((DEPTH, 6 * D), 0.02),
        "norm_g": gain((DEPTH, 2, D)),
        "ssm_w_in": nrm((N_LAYERS_A, D, SSM_IN_DIM), D ** -0.5),
        "ssm_conv_w": nrm((N_LAYERS_A, SSM_CONV, SSM_CONV_DIM), SSM_CONV ** -0.5),
        "ssm_conv_b": nrm((N_LAYERS_A, SSM_CONV_DIM), 0.02),
        "ssm_a_log": jnp.log(u_a),
        "ssm_dt_bias": dt0 + jnp.log(-jnp.expm1(-dt0)),
        "ssm_d": 1.0 + nrm((N_LAYERS_A, SSM_HEADS), 0.1),
        "ssm_norm_g": gain((N_LAYERS_A, SSM_D_INNER)),
        "ssm_w_out": nrm((N_LAYERS_A, SSM_D_INNER, D), SSM_D_INNER ** -0.5),
        "win_w_qkv": nrm((N_LAYERS_B, D, (WIN_HEADS + 2 * WIN_KV_HEADS) * HEAD_DIM), D ** -0.5),
        "win_q_g": gain((N_LAYERS_B, HEAD_DIM)),
        "win_k_g": gain((N_LAYERS_B, HEAD_DIM)),
        "win_sinks": nrm((N_LAYERS_B, WIN_HEADS), 0.5),
        "win_w_out": nrm((N_LAYERS_B, WIN_HEADS * HEAD_DIM, D), (WIN_HEADS * HEAD_DIM) ** -0.5),
        "diff_w_qkv": nrm((N_LAYERS_C, D, 6 * DIFF_HEADS * HEAD_DIM), D ** -0.5),
        "diff_q_g": gain((N_LAYERS_C, HEAD_DIM)),
        "diff_k_g": gain((N_LAYERS_C, HEAD_DIM)),
        "diff_lam": nrm((N_LAYERS_C, 4, HEAD_DIM), 0.1),
        "diff_subln_g": gain((N_LAYERS_C, 2 * HEAD_DIM)),
        "diff_w_out": nrm((N_LAYERS_C, D, D), D ** -0.5),
        "moe_w_group": nrm((DEPTH, D, MOE_GROUPS), D ** -0.5),
        "moe_b_group": nrm((DEPTH, MOE_GROUPS), 0.01),
        "moe_w_expert": nrm((DEPTH, D, MOE_EXPERTS), D ** -0.5),
        "moe_b_expert": nrm((DEPTH, MOE_EXPERTS), 0.01),
        "moe_w_gate": nrm((DEPTH, MOE_EXPERTS, D, MOE_D_FF), D ** -0.5),
        "moe_w_up": nrm((DEPTH, MOE_EXPERTS, D, MOE_D_FF), D ** -0.5),
        "moe_w_down": nrm((DEPTH, MOE_EXPERTS, MOE_D_FF, D), MOE_D_FF ** -0.5),
    }


def reference(x, c, ctx, c_ctx, ada_w, ada_b, norm_g,
              ssm_w_in, ssm_conv_w, ssm_conv_b, ssm_a_log, ssm_dt_bias, ssm_d, ssm_norm_g, ssm_w_out,
              win_w_qkv, win_q_g, win_k_g, win_sinks, win_w_out,
              diff_w_qkv, diff_q_g, diff_k_g, diff_lam, diff_subln_g, diff_w_out,
              moe_w_group, moe_b_group, moe_w_expert, moe_b_expert, moe_w_gate, moe_w_up, moe_w_down):
    b, s, d = x.shape
    n_ctx = ctx.shape[1]
    cos, sin = axial_rope(s, HEAD_DIM)
    for i in range(DEPTH):
        kind = i % N_MIXERS
        j = i // N_MIXERS
        need_ctx = i < DEPTH - 1
        mx = modulation(c, ada_w[i], ada_b[i])
        mc = modulation(c_ctx, ada_w[i], ada_b[i])
        hx = modulate(rms_norm(x, norm_g[i, 0]), mx[0][:, None], mx[1][:, None])
        hc = modulate(rms_norm(ctx, norm_g[i, 0]), mc[0], mc[1])
        if kind == 0:
            yx, yc = ssd_mixer(hx, hc, ssm_w_in[j], ssm_conv_w[j], ssm_conv_b[j], ssm_a_log[j],
                               ssm_dt_bias[j], ssm_d[j], ssm_norm_g[j], ssm_w_out[j], need_ctx)
        elif kind == 1:
            yx, yc = window_gqa_mixer(hx, hc, win_w_qkv[j], win_q_g[j], win_k_g[j], win_sinks[j],
                                      win_w_out[j], cos, sin, need_ctx)
        else:
            lam_init = 0.8 - 0.6 * math.exp(-0.3 * i)
            yx, yc = diff_attn_mixer(hx, hc, diff_w_qkv[j], diff_q_g[j], diff_k_g[j], diff_lam[j],
                                     diff_subln_g[j], diff_w_out[j], lam_init, cos, sin, need_ctx)
        x = x + mx[2][:, None] * yx
        hx = modulate(rms_norm(x, norm_g[i, 1]), mx[3][:, None], mx[4][:, None])
        if need_ctx:
            ctx = ctx + mc[2] * yc
            hc = modulate(rms_norm(ctx, norm_g[i, 1]), mc[3], mc[4])
            tokens = jnp.concatenate([hx.reshape(-1, d), hc.reshape(-1, d)], axis=0)
        else:
            tokens = hx.reshape(-1, d)
        y = hier_moe(tokens, moe_w_group[i], moe_b_group[i], moe_w_expert[i], moe_b_expert[i],
                     moe_w_gate[i], moe_w_up[i], moe_w_down[i])
        x = x + mx[5][:, None] * y[:b * s].reshape(b, s, d)
        if need_ctx:
            ctx = ctx + mc[5] * y[b * s:].reshape(b, n_ctx, d)
    return x
```

```python
import functools
import math

import jax
import jax.numpy as jnp
from jax import lax
from jax.experimental import pallas as pl
from jax.experimental.pallas import tpu as pltpu

F32 = jnp.float32
BF16 = jnp.bfloat16
I32 = jnp.int32
U32 = jnp.uint32

D_MODEL = 2048
GRID_W = 64
NORM_EPS = 1e-6
ROPE_THETA = 10000.0
N_MIXERS = 3

SSM_D_INNER = 2 * D_MODEL
SSM_HEAD_DIM = 64
SSM_HEADS = SSM_D_INNER // SSM_HEAD_DIM
SSM_STATE = 128
SSM_GROUPS = 8
SSM_CHUNK = 128
SSM_CONV_DIM = SSM_D_INNER + 2 * SSM_GROUPS * SSM_STATE
SSM_GROUP_W = SSM_D_INNER // SSM_GROUPS

HEAD_DIM = 128
WIN_HEADS = D_MODEL // HEAD_DIM
WIN_KV_HEADS = 4
WINDOW = 128
ATT_BLOCK = 128
DIFF_HEADS = D_MODEL // (2 * HEAD_DIM)

MOE_GROUPS = 4
MOE_EPG = 8
MOE_EXPERTS = MOE_GROUPS * MOE_EPG
MOE_D_FF = D_MODEL // 4

LANES = 128
NEG = -1e30
VMEM_LIMIT = 48 * 1024 * 1024

ROW_TILE = 512
MOE_ROWS = 256
COMBINE_ROWS = 256


def _cparams(*sem):
    return pltpu.CompilerParams(dimension_semantics=sem, vmem_limit_bytes=VMEM_LIMIT)


def _variant_of_tile(i, tiles_per_batch, n_latent_tiles):
    return jnp.where(i < n_latent_tiles, 1 + i // tiles_per_batch, 0)


def _silu(v):
    return v * jax.nn.sigmoid(v)


def _split3(v):
    hi = v.astype(BF16)
    r1 = v - hi.astype(F32)
    mid = r1.astype(BF16)
    lo = (r1 - mid.astype(F32)).astype(BF16)
    return hi, mid, lo


def _mod_kernel(c_ref, w_ref, b_ref, o_ref):
    s = _silu(c_ref[...]).astype(BF16)
    o_ref[...] = jnp.dot(s, w_ref[...].astype(BF16), preferred_element_type=F32) + b_ref[...]


def _modulation(cvecs, ada_w, ada_b):
    depth, d, n6 = ada_w.shape
    tn = 1024
    return pl.pallas_call(
        _mod_kernel,
        grid=(depth, n6 // tn),
        in_specs=[
            pl.BlockSpec((8, d), lambda l, j: (0, 0)),
            pl.BlockSpec((None, d, tn), lambda l, j: (l, 0, j)),
            pl.BlockSpec((None, 1, tn), lambda l, j: (l, 0, j)),
        ],
        out_specs=pl.BlockSpec((None, 8, tn), lambda l, j: (l, 0, j)),
        out_shape=jax.ShapeDtypeStruct((depth, 8, n6), F32),
        compiler_params=_cparams("parallel", "parallel"),
        name="adaln_modulation",
    )(cvecs, ada_w, ada_b.reshape(depth, 1, n6))


def _inproj_kernel(x_ref, mod_ref, g_ref, w_ref, *rest, has_tail):
    if has_tail:
        wt_ref, o_ref, ot_ref, h_ref = rest
    else:
        o_ref, h_ref = rest

    @pl.when(pl.program_id(1) == 0)
    def _():
        x = x_ref[...]
        ms = jnp.mean(x * x, axis=-1, keepdims=True)
        y = x * lax.rsqrt(ms + NORM_EPS) * g_ref[...]
        h = (y * (1.0 + mod_ref[1:2, :]) + mod_ref[0:1, :]).astype(BF16)
        h_ref[...] = h
        if has_tail:
            ot_ref[...] = jnp.dot(h, wt_ref[...], preferred_element_type=F32)

    o_ref[...] = jnp.dot(h_ref[...], w_ref[...], preferred_element_type=F32).astype(o_ref.dtype)


def _inproj(xa, modv, gain, w, w_tail, geom):
    t, d = xa.shape
    n = w.shape[1]
    tm, tn = ROW_TILE, 512
    tpb, nlt = geom["s"] // tm, geom["n_lat"] // tm
    var = lambda i, j: (_variant_of_tile(i, tpb, nlt), 0, 0)
    in_specs = [
        pl.BlockSpec((tm, d), lambda i, j: (i, 0)),
        pl.BlockSpec((None, 8, d), var),
        pl.BlockSpec((1, d), lambda i, j: (0, 0)),
        pl.BlockSpec((d, tn), lambda i, j: (0, j)),
    ]
    out_specs = [pl.BlockSpec((tm, tn), lambda i, j: (i, j))]
    out_shape = [jax.ShapeDtypeStruct((t, n), BF16)]
    args = [xa, modv, gain.reshape(1, d), w]
    if w_tail is not None:
        nt = w_tail.shape[1]
        in_specs.append(pl.BlockSpec((d, nt), lambda i, j: (0, 0)))
        out_specs.append(pl.BlockSpec((tm, nt), lambda i, j: (i, 0)))
        out_shape.append(jax.ShapeDtypeStruct((t, nt), F32))
        args.append(w_tail)
    res = pl.pallas_call(
        functools.partial(_inproj_kernel, has_tail=w_tail is not None),
        grid=(t // tm, n // tn),
        in_specs=in_specs,
        out_specs=out_specs,
        out_shape=out_shape,
        scratch_shapes=[pltpu.VMEM((tm, d), BF16)],
        compiler_params=_cparams("parallel", "arbitrary"),
        name="norm_mod_inproj",
    )(*args)
    return res if w_tail is not None else res[0]


def _conv_kernel(xp_ref, x_ref, xn_ref, w_ref, b_ref, o_ref, *, rows, tiles_lat, n_lat_tiles, tiles_ctx):
    i = pl.program_id(0)
    in_lat = i < n_lat_tiles
    k = jnp.where(in_lat, i % tiles_lat, (i - n_lat_tiles) % tiles_ctx)
    n = jnp.where(in_lat, tiles_lat, tiles_ctx)
    x = x_ref[...].astype(F32)
    prev_row = jnp.where(k == 0, 0.0, xp_ref[...].astype(F32)[15:16, :])
    next_row = jnp.where(k == n - 1, 0.0, xn_ref[...].astype(F32)[0:1, :])
    r = lax.broadcasted_iota(I32, (rows, 1), 0)
    xm1 = jnp.where(r == 0, prev_row, pltpu.roll(x, 1, 0))
    xp1 = jnp.where(r == rows - 1, next_row, pltpu.roll(x, rows - 1, 0))
    out = xm1 * w_ref[0:1, :] + x * w_ref[1:2, :] + xp1 * w_ref[2:3, :] + b_ref[...]
    o_ref[...] = _silu(out).astype(o_ref.dtype)


def _ssd_conv(zx, conv_w, conv_b, geom):
    t = zx.shape[0]
    rows, wc = 256, 2048
    col0 = SSM_D_INNER // wc
    halo = 16
    rb = rows // halo
    last_halo = t // halo - 1
    kern = functools.partial(_conv_kernel, rows=rows, tiles_lat=geom["s"] // rows,
                             n_lat_tiles=geom["n_lat"] // rows, tiles_ctx=geom["c"] // rows)
    return pl.pallas_call(
        kern,
        grid=(t // rows, SSM_CONV_DIM // wc),
        in_specs=[
            pl.BlockSpec((halo, wc), lambda i, j: (jnp.maximum(i * rb - 1, 0), col0 + j)),
            pl.BlockSpec((rows, wc), lambda i, j: (i, col0 + j)),
            pl.BlockSpec((halo, wc), lambda i, j: (jnp.minimum((i + 1) * rb, last_halo), col0 + j)),
            pl.BlockSpec((3, wc), lambda i, j: (0, j)),
            pl.BlockSpec((1, wc), lambda i, j: (0, j)),
        ],
        out_specs=pl.BlockSpec((rows, wc), lambda i, j: (i, j)),
        out_shape=jax.ShapeDtypeStruct((t, SSM_CONV_DIM), BF16),
        compiler_params=_cparams("parallel", "parallel"),
        name="ssd_conv_silu",
    )(zx, zx, zx, conv_w, conv_b.reshape(1, SSM_CONV_DIM))


def _ssd_scan_kernel(xs_ref, b_ref, c_ref, dt_ref, a_ref, dtb_ref, y_ref, state_ref, *, reverse):
    q = SSM_CHUNK
    hpg = SSM_HEADS // SSM_GROUPS

    @pl.when(pl.program_id(1) == 0)
    def _():
        state_ref[...] = jnp.zeros_like(state_ref)

    c0 = SSM_HEADS if reverse else 0
    pre = dt_ref[:, c0:c0 + SSM_HEADS] + dtb_ref[...]
    dt = jnp.maximum(pre, 0.0) + jnp.log1p(jnp.exp(-jnp.abs(pre)))
    la = dt * a_ref[...]
    ri = lax.broadcasted_iota(I32, (q, q), 0)
    ci = lax.broadcasted_iota(I32, (q, q), 1)
    tri = (ri <= ci) if reverse else (ri >= ci)
    trib = tri.astype(BF16)
    hi, mid, lo = _split3(la)
    cum = (jnp.dot(trib, hi, preferred_element_type=F32) + jnp.dot(trib, mid, preferred_element_type=F32)
           + jnp.dot(trib, lo, preferred_element_type=F32))
    total = cum[0:1, :] if reverse else cum[q - 1:q, :]
    cum_t = cum.T
    dt_t = dt.T
    exp_cum = jnp.exp(cum)
    w_state = jnp.exp(total - cum) * dt
    exp_total = jnp.exp(total)
    lane = lax.broadcasted_iota(I32, (1, LANES), 1)
    first_half = lane < SSM_HEAD_DIM

    for g in range(SSM_GROUPS):
        bg = b_ref[:, g * SSM_STATE:(g + 1) * SSM_STATE]
        cg = c_ref[:, g * SSM_STATE:(g + 1) * SSM_STATE]
        cb = lax.dot_general(cg, bg, (((1,), (1,)), ((), ())), preferred_element_type=F32)
        st = state_ref[g]
        y_state = jnp.dot(cg, st.astype(BF16), preferred_element_type=F32)
        xw_parts = []
        scale_parts = []
        for pair in range(hpg // 2):
            h0 = g * hpg + 2 * pair
            col = g * SSM_GROUP_W + pair * LANES
            xpair = xs_ref[:, col:col + LANES]
            ws = []
            for h in (h0, h0 + 1):
                seg = cum[:, h:h + 1] - cum_t[h:h + 1, :]
                dec = jnp.exp(jnp.where(tri, seg, NEG))
                ws.append((cb * dec * dt_t[h:h + 1, :]).astype(BF16))
            y0 = jnp.dot(ws[0], xpair, preferred_element_type=F32)
            y1 = jnp.dot(ws[1], xpair, preferred_element_type=F32)
            ec = jnp.where(first_half, exp_cum[:, h0:h0 + 1], exp_cum[:, h0 + 1:h0 + 2])
            ysp = y_state[:, pair * LANES:(pair + 1) * LANES]
            y_ref[:, col:col + LANES] = (jnp.where(first_half, y0, y1) + ysp * ec).astype(y_ref.dtype)
            wsp = jnp.where(first_half, w_state[:, h0:h0 + 1], w_state[:, h0 + 1:h0 + 2])
            xw_parts.append((xpair.astype(F32) * wsp).astype(BF16))
            scale_parts.append(jnp.where(first_half, exp_total[:, h0:h0 + 1], exp_total[:, h0 + 1:h0 + 2]))
        xw = jnp.concatenate(xw_parts, axis=1)
        scale_row = jnp.concatenate(scale_parts, axis=1)
        upd = lax.dot_general(bg, xw, (((0,), (0,)), ((), ())), preferred_element_type=F32)
        state_ref[g] = st * scale_row + upd


def _ssd_scan(xbc, dt_raw, a_dir, dtb_dir, geom, reverse):
    t = xbc.shape[0]
    q = SSM_CHUNK
    bsz = geom["b"]
    cq, lq = geom["c"] // q, geom["s"] // q
    lat_blocks = bsz * lq

    def row_block(b, s):
        cchunk = (cq - 1 - s) if reverse else s
        lchunk = (lq - 1 - (s - cq)) if reverse else (s - cq)
        return jnp.where(s < cq, lat_blocks + b * cq + cchunk, b * lq + lchunk)

    xcols = SSM_D_INNER // SSM_D_INNER
    del xcols
    bc_w = SSM_GROUPS * SSM_STATE
    return pl.pallas_call(
        functools.partial(_ssd_scan_kernel, reverse=reverse),
        grid=(bsz, cq + lq),
        in_specs=[
            pl.BlockSpec((q, SSM_D_INNER), lambda b, s: (row_block(b, s), 0)),
            pl.BlockSpec((q, bc_w), lambda b, s: (row_block(b, s), SSM_D_INNER // bc_w)),
            pl.BlockSpec((q, bc_w), lambda b, s: (row_block(b, s), SSM_D_INNER // bc_w + 1)),
            pl.BlockSpec((q, 2 * SSM_HEADS), lambda b, s: (row_block(b, s), 0)),
            pl.BlockSpec((1, SSM_HEADS), lambda b, s: (0, 0)),
            pl.BlockSpec((1, SSM_HEADS), lambda b, s: (0, 0)),
        ],
        out_specs=pl.BlockSpec((q, SSM_D_INNER), lambda b, s: (row_block(b, s), 0)),
        out_shape=jax.ShapeDtypeStruct((t, SSM_D_INNER), BF16),
        scratch_shapes=[pltpu.VMEM((SSM_GROUPS, SSM_STATE, SSM_GROUP_W), F32)],
        compiler_params=_cparams("parallel", "arbitrary"),
        name="ssd_scan_rev" if reverse else "ssd_scan_fwd",
    )(xbc, xbc, xbc, dt_raw, a_dir, dtb_dir)


def _qk_prep_kernel(x_ref, g_ref, s_ref, cos_ref, sin_ref, o_ref, *, heads):
    cosv = cos_ref[...]
    sinv = sin_ref[...]
    lane = lax.broadcasted_iota(I32, (1, HEAD_DIM), 1)
    low = (lane % (HEAD_DIM // 2)) < (HEAD_DIM // 4)
    for h in range(heads):
        sl = slice(h * HEAD_DIM, (h + 1) * HEAD_DIM)
        x = x_ref[:, sl].astype(F32)
        ms = jnp.mean(x * x, axis=-1, keepdims=True)
        y = x * lax.rsqrt(ms + NORM_EPS) * g_ref[:, sl]
        partner = jnp.where(low, pltpu.roll(y, HEAD_DIM - HEAD_DIM // 4, 1), pltpu.roll(y, HEAD_DIM // 4, 1))
        o_ref[:, sl] = ((y * cosv + partner * sinv) * s_ref[:, sl]).astype(o_ref.dtype)


def _qk_prep(qkv, gains, post_scale, cos_t, sin_t):
    t = qkv.shape[0]
    n = gains.shape[1]
    rows, wb = 512, 512
    return pl.pallas_call(
        functools.partial(_qk_prep_kernel, heads=wb // HEAD_DIM),
        grid=(t // rows, n // wb),
        in_specs=[
            pl.BlockSpec((rows, wb), lambda i, j: (i, j)),
            pl.BlockSpec((1, wb), lambda i, j: (0, j)),
            pl.BlockSpec((1, wb), lambda i, j: (0, j)),
            pl.BlockSpec((rows, HEAD_DIM), lambda i, j: (i, 0)),
            pl.BlockSpec((rows, HEAD_DIM), lambda i, j: (i, 0)),
        ],
        out_specs=pl.BlockSpec((rows, wb), lambda i, j: (i, j)),
        out_shape=jax.ShapeDtypeStruct((t, n), BF16),
        compiler_params=_cparams("parallel", "parallel"),
        name="qk_norm_rope",
    )(qkv, gains, post_scale, cos_t, sin_t)


def _rope_tables(geom):
    s, t = geom["s"], geom["t"]
    pos = jnp.arange(s)
    row = (pos // GRID_W).astype(F32)
    col = (pos % GRID_W).astype(F32)
    quarter = HEAD_DIM // 4
    inv_freq = ROPE_THETA ** (-jnp.arange(quarter, dtype=F32) / quarter)
    ar = row[:, None] * inv_freq
    ac = col[:, None] * inv_freq
    cos_l = jnp.concatenate([jnp.cos(ar), jnp.cos(ar), jnp.cos(ac), jnp.cos(ac)], axis=1)
    sin_l = jnp.concatenate([-jnp.sin(ar), jnp.sin(ar), -jnp.sin(ac), jnp.sin(ac)], axis=1)
    n_ctx_rows = t - geom["n_lat"]
    cos_t = jnp.concatenate([jnp.tile(cos_l, (geom["b"], 1)), jnp.ones((n_ctx_rows, HEAD_DIM), F32)], axis=0)
    sin_t = jnp.concatenate([jnp.tile(sin_l, (geom["b"], 1)), jnp.zeros((n_ctx_rows, HEAD_DIM), F32)], axis=0)
    return cos_t, sin_t


def _win_attn_kernel(sink_ref, q_ref, kp_ref, kc_ref, kn_ref, vp_ref, vc_ref, vn_ref, kx_ref, vx_ref, o_ref,
                     *, n_lat_blocks, ctx_blocks):
    s = pl.program_id(1)
    grp = WIN_HEADS // WIN_KV_HEADS
    blk = ATT_BLOCK
    is_lat = s >= ctx_blocks
    n = s - ctx_blocks
    rows = grp * blk
    qi = lax.broadcasted_iota(I32, (rows, 3 * blk), 0) % blk
    kk = lax.broadcasted_iota(I32, (rows, 3 * blk), 1)
    rel = kk - blk - qi
    in_band = (rel <= WINDOW) & (rel >= -WINDOW)
    lo = jnp.where(n > 0, 0, blk)
    hi = jnp.where(is_lat, jnp.where(n < n_lat_blocks - 1, 3 * blk, 2 * blk), 0)
    mask = in_band & (kk >= lo) & (kk < hi)
    rowi = lax.broadcasted_iota(I32, (rows, 1), 0)
    nt = (((1,), (1,)), ((), ()))
    for kh in range(WIN_KV_HEADS):
        ks = slice(kh * HEAD_DIM, (kh + 1) * HEAD_DIM)
        k_lat = jnp.concatenate([kp_ref[:, ks], kc_ref[:, ks], kn_ref[:, ks]], axis=0)
        v_lat = jnp.concatenate([vp_ref[:, ks], vc_ref[:, ks], vn_ref[:, ks]], axis=0)
        qg = jnp.concatenate(
            [q_ref[:, (kh * grp + j) * HEAD_DIM:(kh * grp + j + 1) * HEAD_DIM] for j in range(grp)], axis=0)
        s_lat = jnp.where(mask, lax.dot_general(qg, k_lat, nt, preferred_element_type=F32), NEG)
        s_ctx = lax.dot_general(qg, kx_ref[:, ks], nt, preferred_element_type=F32)
        sink = jnp.full((rows, 1), sink_ref[kh * grp], F32)
        for j in range(1, grp):
            sink = jnp.where(rowi >= j * blk, sink_ref[kh * grp + j], sink)
        m = jnp.maximum(jnp.maximum(jnp.max(s_lat, axis=-1, keepdims=True),
                                    jnp.max(s_ctx, axis=-1, keepdims=True)), sink)
        p_lat = jnp.exp(s_lat - m)
        p_ctx = jnp.exp(s_ctx - m)
        denom = (jnp.sum(p_lat, axis=-1, keepdims=True) + jnp.sum(p_ctx, axis=-1, keepdims=True)
                 + jnp.exp(sink - m))
        o = (jnp.dot(p_lat.astype(BF16), v_lat, preferred_element_type=F32)
             + jnp.dot(p_ctx.astype(BF16), vx_ref[:, ks], preferred_element_type=F32)) / denom
        for j in range(grp):
            hq = kh * grp + j
            o_ref[:, hq * HEAD_DIM:(hq + 1) * HEAD_DIM] = o[j * blk:(j + 1) * blk].astype(o_ref.dtype)


def _win_attn(qk, qkv, sinks, geom):
    t = qk.shape[0]
    blk = ATT_BLOCK
    bsz, c = geom["b"], geom["c"]
    nb = geom["s"] // blk
    cb = c // blk
    lat_blocks = bsz * nb
    kvw = WIN_KV_HEADS * HEAD_DIM
    qw = WIN_HEADS * HEAD_DIM
    kcol = qw // kvw
    vcol = (qw + kvw) // kvw

    def qrow(b, s, sk):
        return jnp.where(s < cb, lat_blocks + b * cb + s, b * nb + (s - cb))

    def lat(off):
        def f(b, s, sk):
            n = jnp.clip(s - cb + off, 0, nb - 1)
            return b * nb + n
        return f

    ctx_row = lambda b, s, sk: (geom["n_lat"] // c + b)
    grid_spec = pltpu.PrefetchScalarGridSpec(
        num_scalar_prefetch=1,
        grid=(bsz, cb + nb),
        in_specs=[
            pl.BlockSpec((blk, qw), lambda b, s, sk: (qrow(b, s, sk), 0)),
            pl.BlockSpec((blk, kvw), lambda b, s, sk: (lat(-1)(b, s, sk), kcol)),
            pl.BlockSpec((blk, kvw), lambda b, s, sk: (lat(0)(b, s, sk), kcol)),
            pl.BlockSpec((blk, kvw), lambda b, s, sk: (lat(1)(b, s, sk), kcol)),
            pl.BlockSpec((blk, kvw), lambda b, s, sk: (lat(-1)(b, s, sk), vcol)),
            pl.BlockSpec((blk, kvw), lambda b, s, sk: (lat(0)(b, s, sk), vcol)),
            pl.BlockSpec((blk, kvw), lambda b, s, sk: (lat(1)(b, s, sk), vcol)),
            pl.BlockSpec((c, kvw), lambda b, s, sk: (ctx_row(b, s, sk), kcol)),
            pl.BlockSpec((c, kvw), lambda b, s, sk: (ctx_row(b, s, sk), vcol)),
        ],
        out_specs=pl.BlockSpec((blk, qw), lambda b, s, sk: (qrow(b, s, sk), 0)),
    )
    return pl.pallas_call(
        functools.partial(_win_attn_kernel, n_lat_blocks=nb, ctx_blocks=cb),
        grid_spec=grid_spec,
        out_shape=jax.ShapeDtypeStruct((t, qw), BF16),
        compiler_params=_cparams("parallel", "parallel"),
        name="window_gqa",
    )(sinks, qk, qk, qk, qk, qkv, qkv, qkv, qk, qkv)


def _diff_attn_kernel(*refs, use_latent, lam_init, sub):
    if use_latent:
        q_ref, kx_ref, vx_ref, k_ref, v_ref, lam_ref, g_ref, o_ref, m_ref, l_ref, acc_ref = refs
    else:
        q_ref, kx_ref, vx_ref, lam_ref, g_ref, _, o_ref, m_ref, l_ref, acc_ref = refs
    j = pl.program_id(3)
    nt = (((1,), (1,)), ((), ()))
    hd = HEAD_DIM

    def update(idx, qh, kh, v):
        s = lax.dot_general(qh, kh, nt, preferred_element_type=F32)
        m_old = m_ref[idx]
        m_new = jnp.maximum(m_old, jnp.max(s, axis=-1, keepdims=True))
        alpha = jnp.exp(m_old - m_new)
        p = jnp.exp(s - m_new)
        l_ref[idx] = alpha * l_ref[idx] + jnp.sum(p, axis=-1, keepdims=True)
        acc_ref[idx] = alpha * acc_ref[idx] + jnp.dot(p.astype(BF16), v, preferred_element_type=F32)
        m_ref[idx] = m_new

    @pl.when(j == 0)
    def _():
        m_ref[...] = jnp.full_like(m_ref, NEG)
        l_ref[...] = jnp.zeros_like(l_ref)
        acc_ref[...] = jnp.zeros_like(acc_ref)
        for idx in range(2):
            update(idx, q_ref[:, idx * hd:(idx + 1) * hd], kx_ref[:, idx * hd:(idx + 1) * hd], vx_ref[...])

    if use_latent:
        for c in range(k_ref.shape[0] // sub):
            rs = slice(c * sub, (c + 1) * sub)
            for idx in range(2):
                update(idx, q_ref[:, idx * hd:(idx + 1) * hd], k_ref[rs, idx * hd:(idx + 1) * hd], v_ref[rs, :])

    @pl.when(j == pl.num_programs(3) - 1)
    def _():
        lam = lam_ref[...]
        lam_full = (jnp.exp(jnp.sum(lam[0:1] * lam[1:2], axis=-1, keepdims=True))
                    - jnp.exp(jnp.sum(lam[2:3] * lam[3:4], axis=-1, keepdims=True)) + lam_init)
        o = acc_ref[0] / l_ref[0] - lam_full * (acc_ref[1] / l_ref[1])
        ms = jnp.mean(o * o, axis=-1, keepdims=True)
        o_ref[...] = (o * lax.rsqrt(ms + NORM_EPS) * g_ref[...] * (1.0 - lam_init)).astype(o_ref.dtype)


def _diff_attn(qk, qkv, lam, subln_g, lam_init, geom):
    t = qk.shape[0]
    bsz, s, c = geom["b"], geom["s"], geom["c"]
    pw = 2 * HEAD_DIM
    tq = min(512, s)
    tkb = min(2048, s)
    sub = min(512, tkb)
    nq, nk = s // tq, s // tkb
    ctx_row = geom["n_lat"] // c
    kcol0, vcol0 = DIFF_HEADS, 2 * DIFF_HEADS
    g2 = subln_g.reshape(1, pw)
    scratch = lambda rows: [pltpu.VMEM((2, rows, 1), F32), pltpu.VMEM((2, rows, 1), F32),
                            pltpu.VMEM((2, rows, pw), F32)]
    o_lat = pl.pallas_call(
        functools.partial(_diff_attn_kernel, use_latent=True, lam_init=lam_init, sub=sub),
        grid=(bsz, DIFF_HEADS, nq, nk),
        in_specs=[
            pl.BlockSpec((tq, pw), lambda b, h, i, j: (b * nq + i, h)),
            pl.BlockSpec((c, pw), lambda b, h, i, j: (ctx_row + b, kcol0 + h)),
            pl.BlockSpec((c, pw), lambda b, h, i, j: (ctx_row + b, vcol0 + h)),
            pl.BlockSpec((tkb, pw), lambda b, h, i, j: (b * nk + j, kcol0 + h)),
            pl.BlockSpec((tkb, pw), lambda b, h, i, j: (b * nk + j, vcol0 + h)),
            pl.BlockSpec((4, HEAD_DIM), lambda b, h, i, j: (0, 0)),
            pl.BlockSpec((1, pw), lambda b, h, i, j: (0, 0)),
        ],
        out_specs=pl.BlockSpec((tq, pw), lambda b, h, i, j: (b * nq + i, h)),
        out_shape=jax.ShapeDtypeStruct((t, D_MODEL), BF16),
        scratch_shapes=scratch(tq),
        compiler_params=_cparams("parallel", "parallel", "parallel", "arbitrary"),
        name="diff_attn_latent",
    )(qk, qk, qkv, qk, qkv, lam, g2)
    return pl.pallas_call(
        functools.partial(_diff_attn_kernel, use_latent=False, lam_init=lam_init, sub=sub),
        grid=(bsz, DIFF_HEADS, 1, 1),
        in_specs=[
            pl.BlockSpec((c, pw), lambda b, h, i, j: (ctx_row + b, h)),
            pl.BlockSpec((c, pw), lambda b, h, i, j: (ctx_row + b, kcol0 + h)),
            pl.BlockSpec((c, pw), lambda b, h, i, j: (ctx_row + b, vcol0 + h)),
            pl.BlockSpec((4, HEAD_DIM), lambda b, h, i, j: (0, 0)),
            pl.BlockSpec((1, pw), lambda b, h, i, j: (0, 0)),
            pl.BlockSpec(memory_space=pl.ANY),
        ],
        out_specs=pl.BlockSpec((c, pw), lambda b, h, i, j: (ctx_row + b, h)),
        out_shape=jax.ShapeDtypeStruct((t, D_MODEL), BF16),
        scratch_shapes=scratch(c),
        input_output_aliases={5: 0},
        compiler_params=_cparams("parallel", "parallel", "parallel", "arbitrary"),
        name="diff_attn_context",
    )(qk, qk, qkv, lam, g2, o_lat)


def _outproj_kernel(*refs, ssd):
    if ssd:
        (xs_ref, yf_ref, yb_ref, z_ref, dsk_ref, ng_ref, w_ref, x_ref, mod_ref, g_ref, rh_ref, rl_ref, rb_ref,
         xo_ref, hp_ref, lg_ref, acc_ref, ssq_ref) = refs
    else:
        (a_ref, w_ref, x_ref, mod_ref, g_ref, rh_ref, rl_ref, rb_ref,
         xo_ref, hp_ref, lg_ref, acc_ref) = refs
    k = pl.program_id(1)

    @pl.when(k == 0)
    def _():
        acc_ref[...] = jnp.zeros_like(acc_ref)
        if ssd:
            ssq_ref[...] = jnp.zeros_like(ssq_ref)

    if ssd:
        y = dsk_ref[...] * xs_ref[...].astype(F32) + yf_ref[...].astype(F32) + yb_ref[...].astype(F32)
        u = y * _silu(z_ref[...].astype(F32))
        ssq_ref[...] += jnp.sum(u * u, axis=-1, keepdims=True)
        a = (u * ng_ref[...]).astype(BF16)
    else:
        a = a_ref[...]
    acc_ref[...] += jnp.dot(a, w_ref[...], preferred_element_type=F32)

    @pl.when(k == pl.num_programs(1) - 1)
    def _():
        y = acc_ref[...]
        if ssd:
            y = y * lax.rsqrt(ssq_ref[...] * (1.0 / SSM_D_INNER) + NORM_EPS)
        xn = x_ref[...] + mod_ref[2:3, :] * y
        xo_ref[...] = xn
        ms = jnp.mean(xn * xn, axis=-1, keepdims=True)
        h = xn * lax.rsqrt(ms + NORM_EPS) * g_ref[...]
        h = h * (1.0 + mod_ref[4:5, :]) + mod_ref[3:4, :]
        hb = h.astype(BF16)
        hl = (h - hb.astype(F32)).astype(BF16)
        lg_ref[...] = (jnp.dot(hb, rh_ref[...], preferred_element_type=F32)
                       + (jnp.dot(hb, rl_ref[...], preferred_element_type=F32)
                          + jnp.dot(hl, rh_ref[...], preferred_element_type=F32)) + rb_ref[...])
        half = D_MODEL // 2
        lo_bits = lax.bitcast_convert_type(hb[:, :half].astype(F32), U32) >> 16
        hi_bits = lax.bitcast_convert_type(hb[:, half:].astype(F32), U32)
        hp_ref[...] = hi_bits | lo_bits


def _outproj(a_args, w, xa, modv, gain, r_hi, r_lo, r_bias, geom, ssd):
    t, d = xa.shape
    kdim = w.shape[0]
    tm, tk = ROW_TILE, 512
    tpb, nlt = geom["s"] // tm, geom["n_lat"] // tm
    var = lambda i, k: (_variant_of_tile(i, tpb, nlt), 0, 0)
    const = lambda i, k: (0, 0)
    if ssd:
        xbc, yf, yb, zx, dsk, ng = a_args
        zcol = 0
        a_specs = [
            pl.BlockSpec((tm, tk), lambda i, k: (i, k)),
            pl.BlockSpec((tm, tk), lambda i, k: (i, k)),
            pl.BlockSpec((tm, tk), lambda i, k: (i, k)),
            pl.BlockSpec((tm, tk), lambda i, k: (i, zcol + k)),
            pl.BlockSpec((1, tk), lambda i, k: (0, k)),
            pl.BlockSpec((1, tk), lambda i, k: (0, k)),
        ]
        a_in = [xbc, yf, yb, zx, dsk, ng]
        scratch = [pltpu.VMEM((tm, d), F32), pltpu.VMEM((tm, 1), F32)]
    else:
        a_specs = [pl.BlockSpec((tm, tk), lambda i, k: (i, k))]
        a_in = list(a_args)
        scratch = [pltpu.VMEM((tm, d), F32)]
    return pl.pallas_call(
        functools.partial(_outproj_kernel, ssd=ssd),
        grid=(t // tm, kdim // tk),
        in_specs=a_specs + [
            pl.BlockSpec((tk, d), lambda i, k: (k, 0)),
            pl.BlockSpec((tm, d), lambda i, k: (i, 0)),
            pl.BlockSpec((None, 8, d), var),
            pl.BlockSpec((1, d), const),
            pl.BlockSpec((d, LANES), const),
            pl.BlockSpec((d, LANES), const),
            pl.BlockSpec((1, LANES), const),
        ],
        out_specs=[
            pl.BlockSpec((tm, d), lambda i, k: (i, 0)),
            pl.BlockSpec((tm, d // 2), lambda i, k: (i, 0)),
            pl.BlockSpec((tm, LANES), lambda i, k: (i, 0)),
        ],
        out_shape=[
            jax.ShapeDtypeStruct((t, d), F32),
            jax.ShapeDtypeStruct((t, d // 2), U32),
            jax.ShapeDtypeStruct((t, LANES), F32),
        ],
        scratch_shapes=scratch,
        compiler_params=_cparams("parallel", "arbitrary"),
        name="outproj_residual_norm",
    )(*a_in, w, xa, modv, gain.reshape(1, d), r_hi, r_lo, r_bias)


ROUTE_E0, ROUTE_E1, ROUTE_G0, ROUTE_G1, ROUTE_R0, ROUTE_R1 = range(6)


def _route_kernel(lg_ref, o_ref, cnt_ref, carry_ref):
    rows = lg_ref.shape[0]

    @pl.when(pl.program_id(0) == 0)
    def _():
        carry_ref[...] = jnp.zeros_like(carry_ref)

    lg = lg_ref[...]
    lane = lax.broadcasted_iota(I32, (rows, LANES), 1)
    big = jnp.int32(LANES)

    def first_argmax(v, vmax):
        return jnp.min(jnp.where(v == vmax, lane, big), axis=-1, keepdims=True)

    gl = jnp.where(lane < MOE_GROUPS, lg, NEG)
    gmax = jnp.max(gl, axis=-1, keepdims=True)
    gsum = jnp.sum(jnp.exp(gl - gmax), axis=-1, keepdims=True)
    g_sel = first_argmax(gl, gmax)
    g_p = 1.0 / gsum
    e_lo = MOE_GROUPS + g_sel * MOE_EPG
    el = jnp.where((lane >= e_lo) & (lane < e_lo + MOE_EPG), lg, NEG)
    emax = jnp.max(el, axis=-1, keepdims=True)
    esum = jnp.sum(jnp.exp(el - emax), axis=-1, keepdims=True)
    l0 = first_argmax(el, emax)
    el2 = jnp.where(lane == l0, NEG, el)
    emax2 = jnp.max(el2, axis=-1, keepdims=True)
    l1 = first_argmax(el2, emax2)
    p0 = 1.0 / esum
    p1 = jnp.exp(emax2 - emax) / esum
    gate0 = g_p * p0 / (p0 + p1)
    gate1 = g_p * p1 / (p0 + p1)
    e0 = l0 - MOE_GROUPS
    e1 = l1 - MOE_GROUPS
    oh0 = lane == e0
    oh1 = lane == e1
    hits = oh0.astype(F32) + oh1.astype(F32)
    ri = lax.broadcasted_iota(I32, (rows, rows), 0)
    ci = lax.broadcasted_iota(I32, (rows, rows), 1)
    before = (ci < ri).astype(BF16)
    prior = jnp.dot(before, hits.astype(BF16), preferred_element_type=F32) + carry_ref[0:1, :]
    r0 = jnp.sum(jnp.where(oh0, prior, 0.0), axis=-1, keepdims=True)
    r1 = jnp.sum(jnp.where(oh1, prior, 0.0), axis=-1, keepdims=True)
    carry = carry_ref[0:1, :] + jnp.sum(hits, axis=0, keepdims=True)
    carry_ref[0:1, :] = carry
    cnt_ref[...] = jnp.broadcast_to(carry, cnt_ref.shape)
    rec = jnp.where(lane == ROUTE_E0, e0.astype(F32), 0.0)
    rec = jnp.where(lane == ROUTE_E1, e1.astype(F32), rec)
    rec = jnp.where(lane == ROUTE_G0, gate0, rec)
    rec = jnp.where(lane == ROUTE_G1, gate1, rec)
    rec = jnp.where(lane == ROUTE_R0, r0, rec)
    rec = jnp.where(lane == ROUTE_R1, r1, rec)
    o_ref[...] = rec


def _route(logits):
    t = logits.shape[0]
    rows = 512
    return pl.pallas_call(
        _route_kernel,
        grid=(t // rows,),
        in_specs=[pl.BlockSpec((rows, LANES), lambda i: (i, 0))],
        out_specs=[pl.BlockSpec((rows, LANES), lambda i: (i, 0)), pl.BlockSpec((8, LANES), lambda i: (0, 0))],
        out_shape=[jax.ShapeDtypeStruct((t, LANES), F32), jax.ShapeDtypeStruct((8, LANES), F32)],
        scratch_shapes=[pltpu.VMEM((8, LANES), F32)],
        compiler_params=_cparams("arbitrary"),
        name="moe_route",
    )(logits)


def _gather_rows(idx_ref, slot, src_hbm, dst_ref, sem, n):
    def body(r, carry):
        pltpu.make_async_copy(src_hbm.at[pl.ds(idx_ref[slot, r], 1)], dst_ref.at[pl.ds(r, 1)], sem).start()
        return carry
    lax.fori_loop(0, n, body, 0)


def _wait_rows(src_hbm, dst_ref, sem, n):
    def body(r, carry):
        pltpu.make_async_copy(src_hbm.at[pl.ds(0, 1)], dst_ref.at[pl.ds(r, 1)], sem).wait()
        return carry
    lax.fori_loop(0, n, body, 0)


def _pipelined_gather(step, nsteps, idx_hbm, idx_smem, idx_sem, src_hbm, bufs, row_sems, n):
    def idx_copy(s):
        return pltpu.make_async_copy(idx_hbm.at[s], idx_smem.at[s % 3], idx_sem.at[s % 3])

    @pl.when(step == 0)
    def _():
        idx_copy(0).start()
        idx_copy(0).wait()
        _gather_rows(idx_smem, 0, src_hbm, bufs.at[0], row_sems.at[0], n)

        @pl.when(nsteps > 1)
        def _():
            idx_copy(1).start()

    @pl.when(step + 1 < nsteps)
    def _():
        idx_copy(step + 1).wait()
        _gather_rows(idx_smem, (step + 1) % 3, src_hbm, bufs.at[(step + 1) % 2], row_sems.at[(step + 1) % 2], n)

    @pl.when(step + 2 < nsteps)
    def _():
        idx_copy(step + 2).start()

    _wait_rows(src_hbm, bufs.at[step % 2], row_sems.at[step % 2], n)


def _expert_kernel(bexp_ref, nused_ref, idx_hbm, tok_hbm, wg_ref, wu_ref, wd_ref, y_ref,
                   idx_smem, xbuf, idx_sem, row_sems):
    del bexp_ref
    b = pl.program_id(0)
    nused = nused_ref[0]

    @pl.when(b < nused)
    def _():
        _pipelined_gather(b, nused, idx_hbm, idx_smem, idx_sem, tok_hbm, xbuf, row_sems, MOE_ROWS)
        w = xbuf[b % 2]
        x_lo = lax.bitcast_convert_type(w << 16, F32).astype(BF16)
        x_hi = lax.bitcast_convert_type(w & jnp.uint32(0xFFFF0000), F32).astype(BF16)
        half = D_MODEL // 2
        gate = (jnp.dot(x_lo, wg_ref[:half, :], preferred_element_type=F32)
                + jnp.dot(x_hi, wg_ref[half:, :], preferred_element_type=F32))
        up = (jnp.dot(x_lo, wu_ref[:half, :], preferred_element_type=F32)
              + jnp.dot(x_hi, wu_ref[half:, :], preferred_element_type=F32))
        hdn = (_silu(gate) * up).astype(BF16)
        y_ref[...] = jnp.dot(hdn, wd_ref[...], preferred_element_type=F32)

    @pl.when(b >= nused)
    def _():
        y_ref[...] = jnp.zeros_like(y_ref)


def _experts(block_exp, n_used, slot_tok, tokens_packed, wg, wu, wd):
    n_blocks = slot_tok.shape[0]
    rows = MOE_ROWS
    half = D_MODEL // 2
    grid_spec = pltpu.PrefetchScalarGridSpec(
        num_scalar_prefetch=2,
        grid=(n_blocks,),
        in_specs=[
            pl.BlockSpec(memory_space=pl.ANY),
            pl.BlockSpec(memory_space=pl.ANY),
            pl.BlockSpec((None, D_MODEL, MOE_D_FF), lambda b, be, nu: (be[b], 0, 0)),
            pl.BlockSpec((None, D_MODEL, MOE_D_FF), lambda b, be, nu: (be[b], 0, 0)),
            pl.BlockSpec((None, MOE_D_FF, D_MODEL), lambda b, be, nu: (be[b], 0, 0)),
        ],
        out_specs=pl.BlockSpec((rows, D_MODEL), lambda b, be, nu: (b, 0)),
        scratch_shapes=[
            pltpu.SMEM((3, rows), I32),
            pltpu.VMEM((2, rows, half), U32),
            pltpu.SemaphoreType.DMA((3,)),
            pltpu.SemaphoreType.DMA((2,)),
        ],
    )
    return pl.pallas_call(
        _expert_kernel,
        grid_spec=grid_spec,
        out_shape=jax.ShapeDtypeStruct((n_blocks * rows, D_MODEL), F32),
        compiler_params=_cparams("arbitrary"),
        name="moe_experts",
    )(block_exp, n_used, slot_tok, tokens_packed, wg, wu, wd)


def _combine_kernel(idx_hbm, ys_hbm, x_ref, mod_ref, rt_ref, o_ref, idx_smem, ybuf, idx_sem, row_sems):
    i = pl.program_id(0)
    n = COMBINE_ROWS
    _pipelined_gather(i, pl.num_programs(0), idx_hbm, idx_smem, idx_sem, ys_hbm, ybuf, row_sems, 2 * n)
    y = ybuf[i % 2]
    rt = rt_ref[...]
    moe = rt[:, ROUTE_G0:ROUTE_G0 + 1] * y[:n] + rt[:, ROUTE_G1:ROUTE_G1 + 1] * y[n:]
    o_ref[...] = x_ref[...] + mod_ref[5:6, :] * moe


def _combine(dest_tiles, ys, xa, modv, route, geom):
    t, d = xa.shape
    n = COMBINE_ROWS
    tpb, nlt = geom["s"] // n, geom["n_lat"] // n
    return pl.pallas_call(
        _combine_kernel,
        grid=(t // n,),
        in_specs=[
            pl.BlockSpec(memory_space=pl.ANY),
            pl.BlockSpec(memory_space=pl.ANY),
            pl.BlockSpec((n, d), lambda i: (i, 0)),
            pl.BlockSpec((None, 8, d), lambda i: (_variant_of_tile(i, tpb, nlt), 0, 0)),
            pl.BlockSpec((n, LANES), lambda i: (i, 0)),
        ],
        out_specs=pl.BlockSpec((n, d), lambda i: (i, 0)),
        out_shape=jax.ShapeDtypeStruct((t, d), F32),
        scratch_shapes=[
            pltpu.SMEM((3, 2 * n), I32),
            pltpu.VMEM((2, 2 * n, d), F32),
            pltpu.SemaphoreType.DMA((3,)),
            pltpu.SemaphoreType.DMA((2,)),
        ],
        compiler_params=_cparams("arbitrary"),
        name="moe_combine_residual",
    )(dest_tiles, ys, xa, modv, route)


def _moe(xa, tokens_packed, logits, modv, wg, wu, wd, geom):
    t = xa.shape[0]
    rows = MOE_ROWS
    route, counts = _route(logits)
    cnt = counts[0, :MOE_EXPERTS].astype(I32)
    padded = (cnt + rows - 1) // rows * rows
    pad_end = jnp.cumsum(padded)
    pad_start = pad_end - padded
    n_blocks = (2 * t + MOE_EXPERTS * (rows - 1) + rows - 1) // rows
    eid = route[:, ROUTE_E0:ROUTE_E1 + 1].astype(I32)
    rank = route[:, ROUTE_R0:ROUTE_R1 + 1].astype(I32)
    dest = pad_start[eid] + rank
    tok = jnp.broadcast_to(jnp.arange(t, dtype=I32)[:, None], (t, 2))
    slot_tok = jnp.zeros((n_blocks * rows,), I32).at[dest.reshape(-1)].set(
        tok.reshape(-1), unique_indices=True)
    block_exp = jnp.minimum(
        jnp.searchsorted(pad_end, jnp.arange(n_blocks, dtype=I32) * rows, side="right"), MOE_EXPERTS - 1
    ).astype(I32)
    n_used = (pad_end[-1:] // rows).astype(I32)
    ys = _experts(block_exp, n_used, slot_tok.reshape(n_blocks, rows), tokens_packed, wg, wu, wd)
    n = COMBINE_ROWS
    dest_tiles = dest.reshape(t // n, n, 2).transpose(0, 2, 1).reshape(t // n, 2 * n)
    return _combine(dest_tiles, ys, xa, modv, route, geom)


def kernel(x, c, ctx, c_ctx, ada_w, ada_b, norm_g, ssm_w_in, ssm_conv_w, ssm_conv_b, ssm_a_log, ssm_dt_bias, ssm_d, ssm_norm_g, ssm_w_out, win_w_qkv, win_q_g, win_k_g, win_sinks, win_w_out, diff_w_qkv, diff_q_g, diff_k_g, diff_lam, diff_subln_g, diff_w_out, moe_w_group, moe_b_group, moe_w_expert, moe_b_expert, moe_w_gate, moe_w_up, moe_w_down):
    bsz, s, d = x.shape
    n_ctx = ctx.shape[1]
    depth = ada_w.shape[0]
    n_lat = bsz * s
    t = n_lat + bsz * n_ctx
    geom = {"b": bsz, "s": s, "c": n_ctx, "n_lat": n_lat, "t": t}
    assert d == D_MODEL and 1 + bsz <= 8
    assert s % max(ROW_TILE, 2048 if s >= 2048 else ROW_TILE) == 0 and (bsz * n_ctx) % ROW_TILE == 0
    assert n_ctx % 256 == 0 and s % GRID_W == 0

    xa = jnp.concatenate([x.reshape(n_lat, d), ctx.reshape(bsz * n_ctx, d)], axis=0)
    cvecs = jnp.zeros((8, d), F32).at[0].set(c_ctx).at[1:1 + bsz].set(c)
    mods = _modulation(cvecs, ada_w, ada_b)
    mods = mods.reshape(depth, 8, 6, d)
    mods = jnp.concatenate([mods, jnp.zeros((depth, 8, 2, d), F32)], axis=2)
    cos_t, sin_t = _rope_tables(geom)
    scale = HEAD_DIM ** -0.5

    for i in range(depth):
        kind, j = i % N_MIXERS, i // N_MIXERS
        modv = mods[i]
        r_w = jnp.zeros((d, LANES), F32).at[:, :MOE_GROUPS].set(moe_w_group[i])
        r_w = r_w.at[:, MOE_GROUPS:MOE_GROUPS + MOE_EXPERTS].set(moe_w_expert[i])
        r_hi = r_w.astype(BF16)
        r_lo = (r_w - r_hi.astype(F32)).astype(BF16)
        r_b = jnp.zeros((1, LANES), F32).at[0, :MOE_GROUPS].set(moe_b_group[i])
        r_b = r_b.at[0, MOE_GROUPS:MOE_GROUPS + MOE_EXPERTS].set(moe_b_expert[i])
        router = (r_hi, r_lo, r_b)

        if kind == 0:
            w_in = ssm_w_in[j].astype(BF16)
            n_main = SSM_D_INNER + SSM_CONV_DIM
            zx, dt_raw = _inproj(xa, modv, norm_g[i, 0], w_in[:, :n_main], w_in[:, n_main:], geom)
            xbc = _ssd_conv(zx, ssm_conv_w[j], ssm_conv_b[j], geom)
            a = -jnp.exp(ssm_a_log[j].astype(F32))
            ys_dir = [
                _ssd_scan(xbc, dt_raw, a[dr].reshape(1, SSM_HEADS), ssm_dt_bias[j, dr].reshape(1, SSM_HEADS),
                          geom, reverse=(dr == 1))
                for dr in range(2)
            ]
            dsk = jnp.repeat(ssm_d[j].astype(F32), SSM_HEAD_DIM).reshape(1, SSM_D_INNER)
            a_args = (xbc, ys_dir[0], ys_dir[1], zx, dsk, ssm_norm_g[j].reshape(1, SSM_D_INNER))
            xa, tok_p, logits = _outproj(a_args, ssm_w_out[j].astype(BF16), xa, modv, norm_g[i, 1],
                                         *router, geom, ssd=True)
        elif kind == 1:
            qw, kvw = WIN_HEADS * HEAD_DIM, WIN_KV_HEADS * HEAD_DIM
            qkv = _inproj(xa, modv, norm_g[i, 0], win_w_qkv[j].astype(BF16), None, geom)
            gains = jnp.concatenate([jnp.tile(win_q_g[j], WIN_HEADS), jnp.tile(win_k_g[j], WIN_KV_HEADS)])
            post = jnp.concatenate([jnp.full((qw,), scale, F32), jnp.ones((kvw,), F32)])
            qk = _qk_prep(qkv, gains.reshape(1, -1), post.reshape(1, -1), cos_t, sin_t)
            o = _win_attn(qk, qkv, win_sinks[j].astype(F32), geom)
            xa, tok_p, logits = _outproj((o,), win_w_out[j].astype(BF16), xa, modv, norm_g[i, 1],
                                         *router, geom, ssd=False)
        else:
            lam_init = 0.8 - 0.6 * math.exp(-0.3 * i)
            hw = 2 * DIFF_HEADS * HEAD_DIM
            qkv = _inproj(xa, modv, norm_g[i, 0], diff_w_qkv[j].astype(BF16), None, geom)
            gains = jnp.concatenate([jnp.tile(diff_q_g[j], 2 * DIFF_HEADS), jnp.tile(diff_k_g[j], 2 * DIFF_HEADS)])
            post = jnp.concatenate([jnp.full((hw,), scale, F32), jnp.ones((hw,), F32)])
            qk = _qk_prep(qkv, gains.reshape(1, -1), post.reshape(1, -1), cos_t, sin_t)
            o = _diff_attn(qk, qkv, diff_lam[j].astype(F32), diff_subln_g[j].astype(F32), lam_init, geom)
            xa, tok_p, logits = _outproj((o,), diff_w_out[j].astype(BF16), xa, modv, norm_g[i, 1],
                                         *router, geom, ssd=False)

        xa = _moe(xa, tok_p, logits, modv, moe_w_gate[i].astype(BF16), moe_w_up[i].astype(BF16),
                  moe_w_down[i].astype(BF16), geom)

    return xa[:n_lat].reshape(bsz, s, d)
```

```python
import functools
import math

import jax
import jax.numpy as jnp
from jax import lax
from jax.experimental import pallas as pl
from jax.experimental.pallas import tpu as pltpu

F32 = jnp.float32
BF16 = jnp.bfloat16
I32 = jnp.int32
U32 = jnp.uint32

D_MODEL = 2048
GRID_W = 64
NORM_EPS = 1e-6
ROPE_THETA = 10000.0
N_MIXERS = 3

SSM_D_INNER = 2 * D_MODEL
SSM_HEAD_DIM = 64
SSM_HEADS = SSM_D_INNER // SSM_HEAD_DIM
SSM_STATE = 128
SSM_GROUPS = 8
SSM_CHUNK = 128
SSM_CONV_DIM = SSM_D_INNER + 2 * SSM_GROUPS * SSM_STATE
SSM_GROUP_W = SSM_D_INNER // SSM_GROUPS

HEAD_DIM = 128
WIN_HEADS = D_MODEL // HEAD_DIM
WIN_KV_HEADS = 4
WINDOW = 128
ATT_BLOCK = 128
DIFF_HEADS = D_MODEL // (2 * HEAD_DIM)

MOE_GROUPS = 4
MOE_EPG = 8
MOE_EXPERTS = MOE_GROUPS * MOE_EPG
MOE_D_FF = D_MODEL // 4

LANES = 128
NEG = -1e30
VMEM_LIMIT = 48 * 1024 * 1024

ROW_TILE = 512
MOE_ROWS = 256
COMBINE_ROWS = 256
TOKEN_SUBROWS = D_MODEL // 2 // LANES


def _cparams(*sem):
    return pltpu.CompilerParams(dimension_semantics=sem, vmem_limit_bytes=VMEM_LIMIT)


def _variant_of_tile(i, tiles_per_batch, n_latent_tiles):
    return jnp.where(i < n_latent_tiles, 1 + i // tiles_per_batch, 0)


def _silu(v):
    return v * jax.nn.sigmoid(v)


def _split3(v):
    hi = v.astype(BF16)
    r1 = v - hi.astype(F32)
    mid = r1.astype(BF16)
    lo = (r1 - mid.astype(F32)).astype(BF16)
    return hi, mid, lo


def _mod_kernel(c_ref, w_ref, b_ref, o_ref):
    s = _silu(c_ref[...]).astype(BF16)
    o_ref[...] = jnp.dot(s, w_ref[...].astype(BF16), preferred_element_type=F32) + b_ref[...]


def _modulation(cvecs, ada_w, ada_b):
    depth, d, n6 = ada_w.shape
    tn = 1024
    return pl.pallas_call(
        _mod_kernel,
        grid=(depth, n6 // tn),
        in_specs=[
            pl.BlockSpec((8, d), lambda l, j: (0, 0)),
            pl.BlockSpec((None, d, tn), lambda l, j: (l, 0, j)),
            pl.BlockSpec((None, 1, tn), lambda l, j: (l, 0, j)),
        ],
        out_specs=pl.BlockSpec((None, 8, tn), lambda l, j: (l, 0, j)),
        out_shape=jax.ShapeDtypeStruct((depth, 8, n6), F32),
        compiler_params=_cparams("parallel", "parallel"),
        name="adaln_modulation",
    )(cvecs, ada_w, ada_b.reshape(depth, 1, n6))


def _inproj_kernel(x_ref, mod_ref, g_ref, w_ref, *rest, has_tail):
    if has_tail:
        wt_ref, o_ref, ot_ref, h_ref = rest
    else:
        o_ref, h_ref = rest

    @pl.when(pl.program_id(1) == 0)
    def _():
        x = x_ref[...]
        ms = jnp.mean(x * x, axis=-1, keepdims=True)
        y = x * lax.rsqrt(ms + NORM_EPS) * g_ref[...]
        h = (y * (1.0 + mod_ref[1:2, :]) + mod_ref[0:1, :]).astype(BF16)
        h_ref[...] = h
        if has_tail:
            ot_ref[...] = jnp.dot(h, wt_ref[...], preferred_element_type=F32)

    o_ref[...] = jnp.dot(h_ref[...], w_ref[...], preferred_element_type=F32).astype(o_ref.dtype)


def _inproj(xa, modv, gain, w, w_tail, geom):
    t, d = xa.shape
    n = w.shape[1]
    tm, tn = ROW_TILE, 512
    tpb, nlt = geom["s"] // tm, geom["n_lat"] // tm
    var = lambda i, j: (_variant_of_tile(i, tpb, nlt), 0, 0)
    in_specs = [
        pl.BlockSpec((tm, d), lambda i, j: (i, 0)),
        pl.BlockSpec((None, 8, d), var),
        pl.BlockSpec((1, d), lambda i, j: (0, 0)),
        pl.BlockSpec((d, tn), lambda i, j: (0, j)),
    ]
    out_specs = [pl.BlockSpec((tm, tn), lambda i, j: (i, j))]
    out_shape = [jax.ShapeDtypeStruct((t, n), BF16)]
    args = [xa, modv, gain.reshape(1, d), w]
    if w_tail is not None:
        nt = w_tail.shape[1]
        in_specs.append(pl.BlockSpec((d, nt), lambda i, j: (0, 0)))
        out_specs.append(pl.BlockSpec((tm, nt), lambda i, j: (i, 0)))
        out_shape.append(jax.ShapeDtypeStruct((t, nt), F32))
        args.append(w_tail)
    res = pl.pallas_call(
        functools.partial(_inproj_kernel, has_tail=w_tail is not None),
        grid=(t // tm, n // tn),
        in_specs=in_specs,
        out_specs=out_specs,
        out_shape=out_shape,
        scratch_shapes=[pltpu.VMEM((tm, d), BF16)],
        compiler_params=_cparams("parallel", "arbitrary"),
        name="norm_mod_inproj",
    )(*args)
    return res if w_tail is not None else res[0]


def _conv_kernel(xp_ref, x_ref, xn_ref, w_ref, b_ref, o_ref, *, rows, tiles_lat, n_lat_tiles, tiles_ctx):
    i = pl.program_id(0)
    in_lat = i < n_lat_tiles
    k = jnp.where(in_lat, i % tiles_lat, (i - n_lat_tiles) % tiles_ctx)
    n = jnp.where(in_lat, tiles_lat, tiles_ctx)
    x = x_ref[...].astype(F32)
    prev_row = jnp.where(k == 0, 0.0, xp_ref[...].astype(F32)[15:16, :])
    next_row = jnp.where(k == n - 1, 0.0, xn_ref[...].astype(F32)[0:1, :])
    r = lax.broadcasted_iota(I32, (rows, 1), 0)
    xm1 = jnp.where(r == 0, prev_row, pltpu.roll(x, 1, 0))
    xp1 = jnp.where(r == rows - 1, next_row, pltpu.roll(x, rows - 1, 0))
    out = xm1 * w_ref[0:1, :] + x * w_ref[1:2, :] + xp1 * w_ref[2:3, :] + b_ref[...]
    o_ref[...] = _silu(out).astype(o_ref.dtype)


def _ssd_conv(zx, conv_w, conv_b, geom):
    t = zx.shape[0]
    rows, wc = 256, 2048
    col0 = SSM_D_INNER // wc
    halo = 16
    rb = rows // halo
    last_halo = t // halo - 1
    kern = functools.partial(_conv_kernel, rows=rows, tiles_lat=geom["s"] // rows,
                             n_lat_tiles=geom["n_lat"] // rows, tiles_ctx=geom["c"] // rows)
    return pl.pallas_call(
        kern,
        grid=(t // rows, SSM_CONV_DIM // wc),
        in_specs=[
            pl.BlockSpec((halo, wc), lambda i, j: (jnp.maximum(i * rb - 1, 0), col0 + j)),
            pl.BlockSpec((rows, wc), lambda i, j: (i, col0 + j)),
            pl.BlockSpec((halo, wc), lambda i, j: (jnp.minimum((i + 1) * rb, last_halo), col0 + j)),
            pl.BlockSpec((3, wc), lambda i, j: (0, j)),
            pl.BlockSpec((1, wc), lambda i, j: (0, j)),
        ],
        out_specs=pl.BlockSpec((rows, wc), lambda i, j: (i, j)),
        out_shape=jax.ShapeDtypeStruct((t, SSM_CONV_DIM), BF16),
        compiler_params=_cparams("parallel", "parallel"),
        name="ssd_conv_silu",
    )(zx, zx, zx, conv_w, conv_b.reshape(1, SSM_CONV_DIM))


def _ssd_scan_kernel(xs_ref, b_ref, c_ref, dt_ref, a_ref, dtb_ref, y_ref, state_ref, *, reverse):
    q = SSM_CHUNK
    hpg = SSM_HEADS // SSM_GROUPS

    @pl.when(pl.program_id(1) == 0)
    def _():
        state_ref[...] = jnp.zeros_like(state_ref)

    c0 = SSM_HEADS if reverse else 0
    pre = dt_ref[:, c0:c0 + SSM_HEADS] + dtb_ref[...]
    dt = jnp.maximum(pre, 0.0) + jnp.log1p(jnp.exp(-jnp.abs(pre)))
    la = dt * a_ref[...]
    ri = lax.broadcasted_iota(I32, (q, q), 0)
    ci = lax.broadcasted_iota(I32, (q, q), 1)
    tri = (ri <= ci) if reverse else (ri >= ci)
    trib = tri.astype(BF16)
    hi, mid, lo = _split3(la)
    cum = (jnp.dot(trib, hi, preferred_element_type=F32) + jnp.dot(trib, mid, preferred_element_type=F32)
           + jnp.dot(trib, lo, preferred_element_type=F32))
    total = cum[0:1, :] if reverse else cum[q - 1:q, :]
    cum_t = cum.T
    dt_t = dt.T
    exp_cum = jnp.exp(cum)
    w_state = jnp.exp(total - cum) * dt
    exp_total = jnp.exp(total)
    lane = lax.broadcasted_iota(I32, (1, LANES), 1)
    first_half = lane < SSM_HEAD_DIM

    for g in range(SSM_GROUPS):
        bg = b_ref[:, g * SSM_STATE:(g + 1) * SSM_STATE]
        cg = c_ref[:, g * SSM_STATE:(g + 1) * SSM_STATE]
        cb = lax.dot_general(cg, bg, (((1,), (1,)), ((), ())), preferred_element_type=F32)
        st = state_ref[g]
        y_state = jnp.dot(cg, st.astype(BF16), preferred_element_type=F32)
        xw_parts = []
        scale_parts = []
        for pair in range(hpg // 2):
            h0 = g * hpg + 2 * pair
            col = g * SSM_GROUP_W + pair * LANES
            xpair = xs_ref[:, col:col + LANES]
            ws = []
            for h in (h0, h0 + 1):
                seg = cum[:, h:h + 1] - cum_t[h:h + 1, :]
                dec = jnp.exp(jnp.where(tri, seg, NEG))
                ws.append((cb * dec * dt_t[h:h + 1, :]).astype(BF16))
            y0 = jnp.dot(ws[0], xpair, preferred_element_type=F32)
            y1 = jnp.dot(ws[1], xpair, preferred_element_type=F32)
            ec = jnp.where(first_half, exp_cum[:, h0:h0 + 1], exp_cum[:, h0 + 1:h0 + 2])
            ysp = y_state[:, pair * LANES:(pair + 1) * LANES]
            y_ref[:, col:col + LANES] = (jnp.where(first_half, y0, y1) + ysp * ec).astype(y_ref.dtype)
            wsp = jnp.where(first_half, w_state[:, h0:h0 + 1], w_state[:, h0 + 1:h0 + 2])
            xw_parts.append((xpair.astype(F32) * wsp).astype(BF16))
            scale_parts.append(jnp.where(first_half, exp_total[:, h0:h0 + 1], exp_total[:, h0 + 1:h0 + 2]))
        xw = jnp.concatenate(xw_parts, axis=1)
        scale_row = jnp.concatenate(scale_parts, axis=1)
        upd = lax.dot_general(bg, xw, (((0,), (0,)), ((), ())), preferred_element_type=F32)
        state_ref[g] = st * scale_row + upd


def _ssd_scan(xbc, dt_raw, a_dir, dtb_dir, geom, reverse):
    t = xbc.shape[0]
    q = SSM_CHUNK
    bsz = geom["b"]
    cq, lq = geom["c"] // q, geom["s"] // q
    lat_blocks = bsz * lq

    def row_block(b, s):
        cchunk = (cq - 1 - s) if reverse else s
        lchunk = (lq - 1 - (s - cq)) if reverse else (s - cq)
        return jnp.where(s < cq, lat_blocks + b * cq + cchunk, b * lq + lchunk)

    xcols = SSM_D_INNER // SSM_D_INNER
    del xcols
    bc_w = SSM_GROUPS * SSM_STATE
    return pl.pallas_call(
        functools.partial(_ssd_scan_kernel, reverse=reverse),
        grid=(bsz, cq + lq),
        in_specs=[
            pl.BlockSpec((q, SSM_D_INNER), lambda b, s: (row_block(b, s), 0)),
            pl.BlockSpec((q, bc_w), lambda b, s: (row_block(b, s), SSM_D_INNER // bc_w)),
            pl.BlockSpec((q, bc_w), lambda b, s: (row_block(b, s), SSM_D_INNER // bc_w + 1)),
            pl.BlockSpec((q, 2 * SSM_HEADS), lambda b, s: (row_block(b, s), 0)),
            pl.BlockSpec((1, SSM_HEADS), lambda b, s: (0, 0)),
            pl.BlockSpec((1, SSM_HEADS), lambda b, s: (0, 0)),
        ],
        out_specs=pl.BlockSpec((q, SSM_D_INNER), lambda b, s: (row_block(b, s), 0)),
        out_shape=jax.ShapeDtypeStruct((t, SSM_D_INNER), BF16),
        scratch_shapes=[pltpu.VMEM((SSM_GROUPS, SSM_STATE, SSM_GROUP_W), F32)],
        compiler_params=_cparams("parallel", "arbitrary"),
        name="ssd_scan_rev" if reverse else "ssd_scan_fwd",
    )(xbc, xbc, xbc, dt_raw, a_dir, dtb_dir)


def _qk_prep_kernel(x_ref, g_ref, s_ref, cos_ref, sin_ref, o_ref, *, heads):
    cosv = cos_ref[...]
    sinv = sin_ref[...]
    lane = lax.broadcasted_iota(I32, (1, HEAD_DIM), 1)
    low = (lane % (HEAD_DIM // 2)) < (HEAD_DIM // 4)
    for h in range(heads):
        sl = slice(h * HEAD_DIM, (h + 1) * HEAD_DIM)
        x = x_ref[:, sl].astype(F32)
        ms = jnp.mean(x * x, axis=-1, keepdims=True)
        y = x * lax.rsqrt(ms + NORM_EPS) * g_ref[:, sl]
        partner = jnp.where(low, pltpu.roll(y, HEAD_DIM - HEAD_DIM // 4, 1), pltpu.roll(y, HEAD_DIM // 4, 1))
        o_ref[:, sl] = ((y * cosv + partner * sinv) * s_ref[:, sl]).astype(o_ref.dtype)


def _qk_prep(qkv, gains, post_scale, cos_t, sin_t):
    t = qkv.shape[0]
    n = gains.shape[1]
    rows, wb = 512, 512
    return pl.pallas_call(
        functools.partial(_qk_prep_kernel, heads=wb // HEAD_DIM),
        grid=(t // rows, n // wb),
        in_specs=[
            pl.BlockSpec((rows, wb), lambda i, j: (i, j)),
            pl.BlockSpec((1, wb), lambda i, j: (0, j)),
            pl.BlockSpec((1, wb), lambda i, j: (0, j)),
            pl.BlockSpec((rows, HEAD_DIM), lambda i, j: (i, 0)),
            pl.BlockSpec((rows, HEAD_DIM), lambda i, j: (i, 0)),
        ],
        out_specs=pl.BlockSpec((rows, wb), lambda i, j: (i, j)),
        out_shape=jax.ShapeDtypeStruct((t, n), BF16),
        compiler_params=_cparams("parallel", "parallel"),
        name="qk_norm_rope",
    )(qkv, gains, post_scale, cos_t, sin_t)


def _rope_tables(geom):
    s, t = geom["s"], geom["t"]
    pos = jnp.arange(s)
    row = (pos // GRID_W).astype(F32)
    col = (pos % GRID_W).astype(F32)
    quarter = HEAD_DIM // 4
    inv_freq = ROPE_THETA ** (-jnp.arange(quarter, dtype=F32) / quarter)
    ar = row[:, None] * inv_freq
    ac = col[:, None] * inv_freq
    cos_l = jnp.concatenate([jnp.cos(ar), jnp.cos(ar), jnp.cos(ac), jnp.cos(ac)], axis=1)
    sin_l = jnp.concatenate([-jnp.sin(ar), jnp.sin(ar), -jnp.sin(ac), jnp.sin(ac)], axis=1)
    n_ctx_rows = t - geom["n_lat"]
    cos_t = jnp.concatenate([jnp.tile(cos_l, (geom["b"], 1)), jnp.ones((n_ctx_rows, HEAD_DIM), F32)], axis=0)
    sin_t = jnp.concatenate([jnp.tile(sin_l, (geom["b"], 1)), jnp.zeros((n_ctx_rows, HEAD_DIM), F32)], axis=0)
    return cos_t, sin_t


def _win_attn_kernel(sink_ref, q_ref, kp_ref, kc_ref, kn_ref, vp_ref, vc_ref, vn_ref, kx_ref, vx_ref, o_ref,
                     *, n_lat_blocks, ctx_blocks):
    s = pl.program_id(1)
    grp = WIN_HEADS // WIN_KV_HEADS
    blk = ATT_BLOCK
    is_lat = s >= ctx_blocks
    n = s - ctx_blocks
    rows = grp * blk
    qi = lax.broadcasted_iota(I32, (rows, 3 * blk), 0) % blk
    kk = lax.broadcasted_iota(I32, (rows, 3 * blk), 1)
    rel = kk - blk - qi
    in_band = (rel <= WINDOW) & (rel >= -WINDOW)
    lo = jnp.where(n > 0, 0, blk)
    hi = jnp.where(is_lat, jnp.where(n < n_lat_blocks - 1, 3 * blk, 2 * blk), 0)
    mask = in_band & (kk >= lo) & (kk < hi)
    rowi = lax.broadcasted_iota(I32, (rows, 1), 0)
    nt = (((1,), (1,)), ((), ()))
    for kh in range(WIN_KV_HEADS):
        ks = slice(kh * HEAD_DIM, (kh + 1) * HEAD_DIM)
        k_lat = jnp.concatenate([kp_ref[:, ks], kc_ref[:, ks], kn_ref[:, ks]], axis=0)
        v_lat = jnp.concatenate([vp_ref[:, ks], vc_ref[:, ks], vn_ref[:, ks]], axis=0)
        qg = jnp.concatenate(
            [q_ref[:, (kh * grp + j) * HEAD_DIM:(kh * grp + j + 1) * HEAD_DIM] for j in range(grp)], axis=0)
        s_lat = jnp.where(mask, lax.dot_general(qg, k_lat, nt, preferred_element_type=F32), NEG)
        s_ctx = lax.dot_general(qg, kx_ref[:, ks], nt, preferred_element_type=F32)
        sink = jnp.full((rows, 1), sink_ref[kh * grp], F32)
        for j in range(1, grp):
            sink = jnp.where(rowi >= j * blk, sink_ref[kh * grp + j], sink)
        m = jnp.maximum(jnp.maximum(jnp.max(s_lat, axis=-1, keepdims=True),
                                    jnp.max(s_ctx, axis=-1, keepdims=True)), sink)
        p_lat = jnp.exp(s_lat - m)
        p_ctx = jnp.exp(s_ctx - m)
        denom = (jnp.sum(p_lat, axis=-1, keepdims=True) + jnp.sum(p_ctx, axis=-1, keepdims=True)
                 + jnp.exp(sink - m))
        o = (jnp.dot(p_lat.astype(BF16), v_lat, preferred_element_type=F32)
             + jnp.dot(p_ctx.astype(BF16), vx_ref[:, ks], preferred_element_type=F32)) / denom
        for j in range(grp):
            hq = kh * grp + j
            o_ref[:, hq * HEAD_DIM:(hq + 1) * HEAD_DIM] = o[j * blk:(j + 1) * blk].astype(o_ref.dtype)


def _win_attn(qk, qkv, sinks, geom):
    t = qk.shape[0]
    blk = ATT_BLOCK
    bsz, c = geom["b"], geom["c"]
    nb = geom["s"] // blk
    cb = c // blk
    lat_blocks = bsz * nb
    kvw = WIN_KV_HEADS * HEAD_DIM
    qw = WIN_HEADS * HEAD_DIM
    kcol = qw // kvw
    vcol = (qw + kvw) // kvw

    def qrow(b, s, sk):
        return jnp.where(s < cb, lat_blocks + b * cb + s, b * nb + (s - cb))

    def lat(off):
        def f(b, s, sk):
            n = jnp.clip(s - cb + off, 0, nb - 1)
            return b * nb + n
        return f

    ctx_row = lambda b, s, sk: (geom["n_lat"] // c + b)
    grid_spec = pltpu.PrefetchScalarGridSpec(
        num_scalar_prefetch=1,
        grid=(bsz, cb + nb),
        in_specs=[
            pl.BlockSpec((blk, qw), lambda b, s, sk: (qrow(b, s, sk), 0)),
            pl.BlockSpec((blk, kvw), lambda b, s, sk: (lat(-1)(b, s, sk), kcol)),
            pl.BlockSpec((blk, kvw), lambda b, s, sk: (lat(0)(b, s, sk), kcol)),
            pl.BlockSpec((blk, kvw), lambda b, s, sk: (lat(1)(b, s, sk), kcol)),
            pl.BlockSpec((blk, kvw), lambda b, s, sk: (lat(-1)(b, s, sk), vcol)),
            pl.BlockSpec((blk, kvw), lambda b, s, sk: (lat(0)(b, s, sk), vcol)),
            pl.BlockSpec((blk, kvw), lambda b, s, sk: (lat(1)(b, s, sk), vcol)),
            pl.BlockSpec((c, kvw), lambda b, s, sk: (ctx_row(b, s, sk), kcol)),
            pl.BlockSpec((c, kvw), lambda b, s, sk: (ctx_row(b, s, sk), vcol)),
        ],
        out_specs=pl.BlockSpec((blk, qw), lambda b, s, sk: (qrow(b, s, sk), 0)),
    )
    return pl.pallas_call(
        functools.partial(_win_attn_kernel, n_lat_blocks=nb, ctx_blocks=cb),
        grid_spec=grid_spec,
        out_shape=jax.ShapeDtypeStruct((t, qw), BF16),
        compiler_params=_cparams("parallel", "parallel"),
        name="window_gqa",
    )(sinks, qk, qk, qk, qk, qkv, qkv, qkv, qk, qkv)


def _diff_attn_kernel(*refs, use_latent, lam_init, sub):
    if use_latent:
        q_ref, kx_ref, vx_ref, k_ref, v_ref, lam_ref, g_ref, o_ref, m_ref, l_ref, acc_ref = refs
    else:
        q_ref, kx_ref, vx_ref, lam_ref, g_ref, _, o_ref, m_ref, l_ref, acc_ref = refs
    j = pl.program_id(3)
    nt = (((1,), (1,)), ((), ()))
    hd = HEAD_DIM

    def update(idx, qh, kh, v):
        s = lax.dot_general(qh, kh, nt, preferred_element_type=F32)
        m_old = m_ref[idx]
        m_new = jnp.maximum(m_old, jnp.max(s, axis=-1, keepdims=True))
        alpha = jnp.exp2(m_old - m_new)
        p = jnp.exp2(s - m_new)
        l_ref[idx] = alpha * l_ref[idx] + jnp.sum(p, axis=-1, keepdims=True)
        acc_ref[idx] = alpha * acc_ref[idx] + jnp.dot(p.astype(BF16), v, preferred_element_type=F32)
        m_ref[idx] = m_new

    @pl.when(j == 0)
    def _():
        m_ref[...] = jnp.full_like(m_ref, NEG)
        l_ref[...] = jnp.zeros_like(l_ref)
        acc_ref[...] = jnp.zeros_like(acc_ref)
        for idx in range(2):
            update(idx, q_ref[:, idx * hd:(idx + 1) * hd], kx_ref[:, idx * hd:(idx + 1) * hd], vx_ref[...])

    if use_latent:
        state = [(m_ref[idx], l_ref[idx], acc_ref[idx]) for idx in range(2)]
        for c in range(k_ref.shape[0] // sub):
            rs = slice(c * sub, (c + 1) * sub)
            for idx in range(2):
                m_old, l_old, acc_old = state[idx]
                s = lax.dot_general(q_ref[:, idx * hd:(idx + 1) * hd], k_ref[rs, idx * hd:(idx + 1) * hd], nt,
                                    preferred_element_type=F32)
                m_new = jnp.maximum(m_old, jnp.max(s, axis=-1, keepdims=True))
                alpha = jnp.exp2(m_old - m_new)
                p = jnp.exp2(s - m_new)
                l_new = alpha * l_old + jnp.sum(p, axis=-1, keepdims=True)
                acc_new = alpha * acc_old + jnp.dot(p.astype(BF16), v_ref[rs, :], preferred_element_type=F32)
                state[idx] = (m_new, l_new, acc_new)
        for idx in range(2):
            m_ref[idx], l_ref[idx], acc_ref[idx] = state[idx]

    @pl.when(j == pl.num_programs(3) - 1)
    def _():
        lam = lam_ref[...]
        lam_full = (jnp.exp(jnp.sum(lam[0:1] * lam[1:2], axis=-1, keepdims=True))
                    - jnp.exp(jnp.sum(lam[2:3] * lam[3:4], axis=-1, keepdims=True)) + lam_init)
        o = acc_ref[0] / l_ref[0] - lam_full * (acc_ref[1] / l_ref[1])
        ms = jnp.mean(o * o, axis=-1, keepdims=True)
        o_ref[...] = (o * lax.rsqrt(ms + NORM_EPS) * g_ref[...] * (1.0 - lam_init)).astype(o_ref.dtype)


def _diff_attn(qk, qkv, lam, subln_g, lam_init, geom):
    t = qk.shape[0]
    bsz, s, c = geom["b"], geom["s"], geom["c"]
    pw = 2 * HEAD_DIM
    tq = min(512, s)
    tkb = min(2048, s)
    sub = min(1024, tkb)
    nq, nk = s // tq, s // tkb
    ctx_row = geom["n_lat"] // c
    kcol0, vcol0 = DIFF_HEADS, 2 * DIFF_HEADS
    g2 = subln_g.reshape(1, pw)
    scratch = lambda rows: [pltpu.VMEM((2, rows, 1), F32), pltpu.VMEM((2, rows, 1), F32),
                            pltpu.VMEM((2, rows, pw), F32)]
    o_lat = pl.pallas_call(
        functools.partial(_diff_attn_kernel, use_latent=True, lam_init=lam_init, sub=sub),
        grid=(bsz, DIFF_HEADS, nq, nk),
        in_specs=[
            pl.BlockSpec((tq, pw), lambda b, h, i, j: (b * nq + i, h)),
            pl.BlockSpec((c, pw), lambda b, h, i, j: (ctx_row + b, kcol0 + h)),
            pl.BlockSpec((c, pw), lambda b, h, i, j: (ctx_row + b, vcol0 + h)),
            pl.BlockSpec((tkb, pw), lambda b, h, i, j: (b * nk + j, kcol0 + h)),
            pl.BlockSpec((tkb, pw), lambda b, h, i, j: (b * nk + j, vcol0 + h)),
            pl.BlockSpec((4, HEAD_DIM), lambda b, h, i, j: (0, 0)),
            pl.BlockSpec((1, pw), lambda b, h, i, j: (0, 0)),
        ],
        out_specs=pl.BlockSpec((tq, pw), lambda b, h, i, j: (b * nq + i, h)),
        out_shape=jax.ShapeDtypeStruct((t, D_MODEL), BF16),
        scratch_shapes=scratch(tq),
        compiler_params=_cparams("parallel", "parallel", "parallel", "arbitrary"),
        name="diff_attn_latent",
    )(qk, qk, qkv, qk, qkv, lam, g2)
    return pl.pallas_call(
        functools.partial(_diff_attn_kernel, use_latent=False, lam_init=lam_init, sub=sub),
        grid=(bsz, DIFF_HEADS, 1, 1),
        in_specs=[
            pl.BlockSpec((c, pw), lambda b, h, i, j: (ctx_row + b, h)),
            pl.BlockSpec((c, pw), lambda b, h, i, j: (ctx_row + b, kcol0 + h)),
            pl.BlockSpec((c, pw), lambda b, h, i, j: (ctx_row + b, vcol0 + h)),
            pl.BlockSpec((4, HEAD_DIM), lambda b, h, i, j: (0, 0)),
            pl.BlockSpec((1, pw), lambda b, h, i, j: (0, 0)),
            pl.BlockSpec(memory_space=pl.ANY),
        ],
        out_specs=pl.BlockSpec((c, pw), lambda b, h, i, j: (ctx_row + b, h)),
        out_shape=jax.ShapeDtypeStruct((t, D_MODEL), BF16),
        scratch_shapes=scratch(c),
        input_output_aliases={5: 0},
        compiler_params=_cparams("parallel", "parallel", "parallel", "arbitrary"),
        name="diff_attn_context",
    )(qk, qk, qkv, lam, g2, o_lat)


def _outproj_kernel(*refs, ssd):
    if ssd:
        (xs_ref, yf_ref, yb_ref, z_ref, dsk_ref, ng_ref, w_ref, x_ref, mod_ref, g_ref, rw_ref, rb_ref,
         xo_ref, hp_ref, lg_ref, acc_ref, ssq_ref) = refs
    else:
        (a_ref, w_ref, x_ref, mod_ref, g_ref, rw_ref, rb_ref,
         xo_ref, hp_ref, lg_ref, acc_ref) = refs
    k = pl.program_id(1)

    @pl.when(k == 0)
    def _():
        acc_ref[...] = jnp.zeros_like(acc_ref)
        if ssd:
            ssq_ref[...] = jnp.zeros_like(ssq_ref)

    if ssd:
        y = dsk_ref[...] * xs_ref[...].astype(F32) + yf_ref[...].astype(F32) + yb_ref[...].astype(F32)
        u = y * _silu(z_ref[...].astype(F32))
        ssq_ref[...] += jnp.sum(u * u, axis=-1, keepdims=True)
        a = (u * ng_ref[...]).astype(BF16)
    else:
        a = a_ref[...]
    acc_ref[...] += jnp.dot(a, w_ref[...], preferred_element_type=F32)

    @pl.when(k == pl.num_programs(1) - 1)
    def _():
        y = acc_ref[...]
        if ssd:
            y = y * lax.rsqrt(ssq_ref[...] * (1.0 / SSM_D_INNER) + NORM_EPS)
        xn = x_ref[...] + mod_ref[2:3, :] * y
        xo_ref[...] = xn
        ms = jnp.mean(xn * xn, axis=-1, keepdims=True)
        h = xn * lax.rsqrt(ms + NORM_EPS) * g_ref[...]
        h = h * (1.0 + mod_ref[4:5, :]) + mod_ref[3:4, :]
        hb = h.astype(BF16)
        hl = (h - hb.astype(F32)).astype(BF16)
        r2 = (jnp.dot(hb, rw_ref[...], preferred_element_type=F32)
              + jnp.dot(hl, rw_ref[...], preferred_element_type=F32))
        lg_ref[...] = r2[:, :LANES] + r2[:, LANES:] + rb_ref[...]
        half = D_MODEL // 2
        lo_bits = lax.bitcast_convert_type(hb[:, :half].astype(F32), U32) >> 16
        hi_bits = lax.bitcast_convert_type(hb[:, half:].astype(F32), U32)
        _tile_store(hp_ref, hi_bits | lo_bits, TOKEN_SUBROWS)


def _outproj(a_args, w, xa, modv, gain, r_w2, r_bias, geom, ssd):
    t, d = xa.shape
    kdim = w.shape[0]
    tm, tk = ROW_TILE, 512
    tpb, nlt = geom["s"] // tm, geom["n_lat"] // tm
    var = lambda i, k: (_variant_of_tile(i, tpb, nlt), 0, 0)
    const = lambda i, k: (0, 0)
    if ssd:
        xbc, yf, yb, zx, dsk, ng = a_args
        zcol = 0
        a_specs = [
            pl.BlockSpec((tm, tk), lambda i, k: (i, k)),
            pl.BlockSpec((tm, tk), lambda i, k: (i, k)),
            pl.BlockSpec((tm, tk), lambda i, k: (i, k)),
            pl.BlockSpec((tm, tk), lambda i, k: (i, zcol + k)),
            pl.BlockSpec((1, tk), lambda i, k: (0, k)),
            pl.BlockSpec((1, tk), lambda i, k: (0, k)),
        ]
        a_in = [xbc, yf, yb, zx, dsk, ng]
        scratch = [pltpu.VMEM((tm, d), F32), pltpu.VMEM((tm, 1), F32)]
    else:
        a_specs = [pl.BlockSpec((tm, tk), lambda i, k: (i, k))]
        a_in = list(a_args)
        scratch = [pltpu.VMEM((tm, d), F32)]
    return pl.pallas_call(
        functools.partial(_outproj_kernel, ssd=ssd),
        grid=(t // tm, kdim // tk),
        in_specs=a_specs + [
            pl.BlockSpec((tk, d), lambda i, k: (k, 0)),
            pl.BlockSpec((tm, d), lambda i, k: (i, 0)),
            pl.BlockSpec((None, 8, d), var),
            pl.BlockSpec((1, d), const),
            pl.BlockSpec((d, 2 * LANES), const),
            pl.BlockSpec((1, LANES), const),
        ],
        out_specs=[
            pl.BlockSpec((tm, d), lambda i, k: (i, 0)),
            pl.BlockSpec((tm * TOKEN_SUBROWS, LANES), lambda i, k: (i, 0)),
            pl.BlockSpec((tm, LANES), lambda i, k: (i, 0)),
        ],
        out_shape=[
            jax.ShapeDtypeStruct((t, d), F32),
            jax.ShapeDtypeStruct((t * TOKEN_SUBROWS, LANES), U32),
            jax.ShapeDtypeStruct((t, LANES), F32),
        ],
        scratch_shapes=scratch,
        compiler_params=_cparams("parallel", "arbitrary"),
        name="outproj_residual_norm",
    )(*a_in, w, xa, modv, gain.reshape(1, d), r_w2, r_bias)


ROUTE_E0, ROUTE_E1, ROUTE_G0, ROUTE_G1, ROUTE_R0, ROUTE_R1 = range(6)


def _route_kernel(lg_ref, o_ref, cnt_ref, carry_ref):
    rows = lg_ref.shape[0]

    @pl.when(pl.program_id(0) == 0)
    def _():
        carry_ref[...] = jnp.zeros_like(carry_ref)

    lg = lg_ref[...]
    lane = lax.broadcasted_iota(I32, (rows, LANES), 1)
    big = jnp.int32(LANES)

    def first_argmax(v, vmax):
        return jnp.min(jnp.where(v == vmax, lane, big), axis=-1, keepdims=True)

    gl = jnp.where(lane < MOE_GROUPS, lg, NEG)
    gmax = jnp.max(gl, axis=-1, keepdims=True)
    gsum = jnp.sum(jnp.exp(gl - gmax), axis=-1, keepdims=True)
    g_sel = first_argmax(gl, gmax)
    g_p = 1.0 / gsum
    e_lo = MOE_GROUPS + g_sel * MOE_EPG
    el = jnp.where((lane >= e_lo) & (lane < e_lo + MOE_EPG), lg, NEG)
    emax = jnp.max(el, axis=-1, keepdims=True)
    esum = jnp.sum(jnp.exp(el - emax), axis=-1, keepdims=True)
    l0 = first_argmax(el, emax)
    el2 = jnp.where(lane == l0, NEG, el)
    emax2 = jnp.max(el2, axis=-1, keepdims=True)
    l1 = first_argmax(el2, emax2)
    p0 = 1.0 / esum
    p1 = jnp.exp(emax2 - emax) / esum
    gate0 = g_p * p0 / (p0 + p1)
    gate1 = g_p * p1 / (p0 + p1)
    e0 = l0 - MOE_GROUPS
    e1 = l1 - MOE_GROUPS
    oh0 = lane == e0
    oh1 = lane == e1
    hits = oh0.astype(F32) + oh1.astype(F32)
    ri = lax.broadcasted_iota(I32, (rows, rows), 0)
    ci = lax.broadcasted_iota(I32, (rows, rows), 1)
    before = (ci < ri).astype(BF16)
    prior = jnp.dot(before, hits.astype(BF16), preferred_element_type=F32) + carry_ref[0:1, :]
    r0 = jnp.sum(jnp.where(oh0, prior, 0.0), axis=-1, keepdims=True)
    r1 = jnp.sum(jnp.where(oh1, prior, 0.0), axis=-1, keepdims=True)
    carry = carry_ref[0:1, :] + jnp.sum(hits, axis=0, keepdims=True)
    carry_ref[0:1, :] = carry
    cnt_ref[...] = jnp.broadcast_to(carry, cnt_ref.shape)
    rec = jnp.where(lane == ROUTE_E0, e0.astype(F32), 0.0)
    rec = jnp.where(lane == ROUTE_E1, e1.astype(F32), rec)
    rec = jnp.where(lane == ROUTE_G0, gate0, rec)
    rec = jnp.where(lane == ROUTE_G1, gate1, rec)
    rec = jnp.where(lane == ROUTE_R0, r0, rec)
    rec = jnp.where(lane == ROUTE_R1, r1, rec)
    o_ref[...] = rec


def _route(logits):
    t = logits.shape[0]
    rows = 512
    return pl.pallas_call(
        _route_kernel,
        grid=(t // rows,),
        in_specs=[pl.BlockSpec((rows, LANES), lambda i: (i, 0))],
        out_specs=[pl.BlockSpec((rows, LANES), lambda i: (i, 0)), pl.BlockSpec((8, LANES), lambda i: (0, 0))],
        out_shape=[jax.ShapeDtypeStruct((t, LANES), F32), jax.ShapeDtypeStruct((8, LANES), F32)],
        scratch_shapes=[pltpu.VMEM((8, LANES), F32)],
        compiler_params=_cparams("arbitrary"),
        name="moe_route",
    )(logits)


def _gather_tiles(idx_smem, islot, src_hbm, dst_ref, sem, n):
    sub = dst_ref.shape[0] // n
    for r in range(n):
        off = pl.multiple_of(idx_smem[islot, r], sub)
        pltpu.make_async_copy(src_hbm.at[pl.ds(off, sub)], dst_ref.at[pl.ds(r * sub, sub)], sem).start()


def _whole_buffer_copy(src_hbm, buf, sem, n):
    del n
    return pltpu.make_async_copy(src_hbm.at[pl.ds(0, buf.shape[0])], buf, sem)


def _gather_step(step, nsteps, idx_hbm, idx_smem, idx_sem, src_hbm, bufs, row_sems, n):
    def idx_copy(s, slot):
        return pltpu.make_async_copy(idx_hbm.at[s], idx_smem.at[slot], idx_sem.at[slot])

    last = nsteps - 1

    @pl.when(step == 0)
    def _():
        idx_copy(0, 0).start()
        idx_copy(0, 0).wait()
        _gather_tiles(idx_smem, 0, src_hbm, bufs.at[0], row_sems.at[0], n)
        idx_copy(jnp.minimum(1, last), 1).start()

    nxt_i, nxt_b = (step + 1) % 3, (step + 1) % 2
    idx_copy(jnp.minimum(step + 1, last), nxt_i).wait()
    _gather_tiles(idx_smem, nxt_i, src_hbm, bufs.at[nxt_b], row_sems.at[nxt_b], n)
    idx_copy(jnp.minimum(step + 2, last), (step + 2) % 3).start()
    _whole_buffer_copy(src_hbm, bufs.at[step % 2], row_sems.at[step % 2], n).wait()


def _gather_drain(step, nsteps, idx_hbm, idx_smem, idx_sem, src_hbm, bufs, row_sems, n):
    @pl.when(step == nsteps - 1)
    def _():
        nxt_b, nxt_i = (step + 1) % 2, (step + 2) % 3
        _whole_buffer_copy(src_hbm, bufs.at[nxt_b], row_sems.at[nxt_b], n).wait()
        pltpu.make_async_copy(idx_hbm.at[0], idx_smem.at[nxt_i], idx_sem.at[nxt_i]).wait()


def _untile(buf, slot, first, n, sub):
    return jnp.concatenate([buf[slot, pl.ds(first * sub + c, n, stride=sub), :] for c in range(sub)], axis=1)


def _tile_store(ref, val, sub):
    n = val.shape[0]
    for c in range(sub):
        ref[pl.ds(c, n, stride=sub), :] = val[:, c * LANES:(c + 1) * LANES]


def _expert_kernel(bexp_ref, nused_ref, idx_hbm, tok_hbm, wg_ref, wu_ref, wd_ref, y_ref,
                   idx_smem, xbuf, idx_sem, row_sems):
    del bexp_ref
    b = pl.program_id(0)
    nused = nused_ref[0]

    @pl.when(b < nused)
    def _():
        rows = MOE_ROWS
        gather_args = (b, nused, idx_hbm, idx_smem, idx_sem, tok_hbm, xbuf, row_sems, rows)
        _gather_step(*gather_args)
        w = _untile(xbuf, b % 2, 0, rows, TOKEN_SUBROWS)
        x_lo = lax.bitcast_convert_type(w << 16, F32).astype(BF16)
        x_hi = lax.bitcast_convert_type(w & jnp.uint32(0xFFFF0000), F32).astype(BF16)
        half = D_MODEL // 2
        gate = (jnp.dot(x_lo, wg_ref[:half, :], preferred_element_type=F32)
                + jnp.dot(x_hi, wg_ref[half:, :], preferred_element_type=F32))
        up = (jnp.dot(x_lo, wu_ref[:half, :], preferred_element_type=F32)
              + jnp.dot(x_hi, wu_ref[half:, :], preferred_element_type=F32))
        hdn = (_silu(gate) * up).astype(BF16)
        y = jnp.dot(hdn, wd_ref[...], preferred_element_type=F32)
        y_ref[...] = y
        _gather_drain(*gather_args)

    @pl.when(b >= nused)
    def _():
        y_ref[...] = jnp.zeros_like(y_ref)


def _experts(block_exp, n_used, slot_tok, tokens_packed, wg, wu, wd):
    n_blocks = slot_tok.shape[0]
    rows = MOE_ROWS
    half = D_MODEL // 2
    sub = TOKEN_SUBROWS
    grid_spec = pltpu.PrefetchScalarGridSpec(
        num_scalar_prefetch=2,
        grid=(n_blocks,),
        in_specs=[
            pl.BlockSpec(memory_space=pl.ANY),
            pl.BlockSpec(memory_space=pl.ANY),
            pl.BlockSpec((None, D_MODEL, MOE_D_FF), lambda b, be, nu: (be[b], 0, 0)),
            pl.BlockSpec((None, D_MODEL, MOE_D_FF), lambda b, be, nu: (be[b], 0, 0)),
            pl.BlockSpec((None, MOE_D_FF, D_MODEL), lambda b, be, nu: (be[b], 0, 0)),
        ],
        out_specs=pl.BlockSpec((rows, D_MODEL), lambda b, be, nu: (b, 0)),
        scratch_shapes=[
            pltpu.SMEM((3, rows), I32),
            pltpu.VMEM((2, rows * sub, half // sub), U32),
            pltpu.SemaphoreType.DMA((3,)),
            pltpu.SemaphoreType.DMA((2,)),
        ],
    )
    return pl.pallas_call(
        _expert_kernel,
        grid_spec=grid_spec,
        out_shape=jax.ShapeDtypeStruct((n_blocks * rows, D_MODEL), F32),
        compiler_params=_cparams("arbitrary"),
        name="moe_experts",
    )(block_exp, n_used, slot_tok, tokens_packed, wg, wu, wd)


def _combine_kernel(idx_hbm, ys_hbm, x_ref, mod_ref, rt_ref, o_ref, idx_smem, ybuf, idx_sem, row_sems):
    i = pl.program_id(0)
    n = COMBINE_ROWS
    gather_args = (i, pl.num_programs(0), idx_hbm, idx_smem, idx_sem, ys_hbm, ybuf, row_sems, 2 * n)
    _gather_step(*gather_args)
    y0 = ybuf[i % 2, pl.ds(0, n), :]
    y1 = ybuf[i % 2, pl.ds(n, n), :]
    rt = rt_ref[...]
    moe = rt[:, ROUTE_G0:ROUTE_G0 + 1] * y0 + rt[:, ROUTE_G1:ROUTE_G1 + 1] * y1
    o_ref[...] = x_ref[...] + mod_ref[5:6, :] * moe
    _gather_drain(*gather_args)


def _combine(dest_tiles, ys, xa, modv, route, geom):
    t, d = xa.shape
    n = COMBINE_ROWS
    tpb, nlt = geom["s"] // n, geom["n_lat"] // n
    return pl.pallas_call(
        _combine_kernel,
        grid=(t // n,),
        in_specs=[
            pl.BlockSpec(memory_space=pl.ANY),
            pl.BlockSpec(memory_space=pl.ANY),
            pl.BlockSpec((n, d), lambda i: (i, 0)),
            pl.BlockSpec((None, 8, d), lambda i: (_variant_of_tile(i, tpb, nlt), 0, 0)),
            pl.BlockSpec((n, LANES), lambda i: (i, 0)),
        ],
        out_specs=pl.BlockSpec((n, d), lambda i: (i, 0)),
        out_shape=jax.ShapeDtypeStruct((t, d), F32),
        scratch_shapes=[
            pltpu.SMEM((3, 2 * n), I32),
            pltpu.VMEM((2, 2 * n, d), F32),
            pltpu.SemaphoreType.DMA((3,)),
            pltpu.SemaphoreType.DMA((2,)),
        ],
        compiler_params=_cparams("arbitrary"),
        name="moe_combine_residual",
    )(dest_tiles, ys, xa, modv, route)


def _moe(xa, tokens_packed, logits, modv, wg, wu, wd, geom):
    t = xa.shape[0]
    rows = MOE_ROWS
    route, counts = _route(logits)
    cnt = counts[0, :MOE_EXPERTS].astype(I32)
    padded = (cnt + rows - 1) // rows * rows
    pad_end = jnp.cumsum(padded)
    pad_start = pad_end - padded
    n_blocks = (2 * t + MOE_EXPERTS * (rows - 1) + rows - 1) // rows
    eid = route[:, ROUTE_E0:ROUTE_E1 + 1].astype(I32)
    rank = route[:, ROUTE_R0:ROUTE_R1 + 1].astype(I32)
    dest = pad_start[eid] + rank
    tok = jnp.broadcast_to(jnp.arange(t, dtype=I32)[:, None], (t, 2))
    slot_tok = jnp.zeros((n_blocks * rows,), I32).at[dest.reshape(-1)].set(
        tok.reshape(-1), unique_indices=True)
    block_start = jnp.arange(n_blocks, dtype=I32) * rows
    block_exp = jnp.minimum(
        jnp.sum((pad_end[None, :] <= block_start[:, None]).astype(I32), axis=1), MOE_EXPERTS - 1)
    n_used = (pad_end[-1:] // rows).astype(I32)
    ys = _experts(block_exp, n_used, slot_tok.reshape(n_blocks, rows) * TOKEN_SUBROWS, tokens_packed, wg, wu, wd)
    n = COMBINE_ROWS
    dest_tiles = dest.reshape(t // n, n, 2).transpose(0, 2, 1).reshape(t // n, 2 * n)
    return _combine(dest_tiles, ys, xa, modv, route, geom)


def kernel(x, c, ctx, c_ctx, ada_w, ada_b, norm_g, ssm_w_in, ssm_conv_w, ssm_conv_b, ssm_a_log, ssm_dt_bias, ssm_d, ssm_norm_g, ssm_w_out, win_w_qkv, win_q_g, win_k_g, win_sinks, win_w_out, diff_w_qkv, diff_q_g, diff_k_g, diff_lam, diff_subln_g, diff_w_out, moe_w_group, moe_b_group, moe_w_expert, moe_b_expert, moe_w_gate, moe_w_up, moe_w_down):
    bsz, s, d = x.shape
    n_ctx = ctx.shape[1]
    depth = ada_w.shape[0]
    n_lat = bsz * s
    t = n_lat + bsz * n_ctx
    geom = {"b": bsz, "s": s, "c": n_ctx, "n_lat": n_lat, "t": t}
    assert d == D_MODEL and 1 + bsz <= 8
    assert s % max(ROW_TILE, 2048 if s >= 2048 else ROW_TILE) == 0 and (bsz * n_ctx) % ROW_TILE == 0
    assert n_ctx % 256 == 0 and s % GRID_W == 0

    xa = jnp.concatenate([x.reshape(n_lat, d), ctx.reshape(bsz * n_ctx, d)], axis=0)
    cvecs = jnp.zeros((8, d), F32).at[0].set(c_ctx).at[1:1 + bsz].set(c)
    mods = _modulation(cvecs, ada_w, ada_b)
    mods = mods.reshape(depth, 8, 6, d)
    mods = jnp.concatenate([mods, jnp.zeros((depth, 8, 2, d), F32)], axis=2)
    cos_t, sin_t = _rope_tables(geom)
    scale = HEAD_DIM ** -0.5

    for i in range(depth):
        kind, j = i % N_MIXERS, i // N_MIXERS
        modv = mods[i]
        r_w = jnp.zeros((d, LANES), F32).at[:, :MOE_GROUPS].set(moe_w_group[i])
        r_w = r_w.at[:, MOE_GROUPS:MOE_GROUPS + MOE_EXPERTS].set(moe_w_expert[i])
        r_hi = r_w.astype(BF16)
        r_lo = (r_w - r_hi.astype(F32)).astype(BF16)
        r_b = jnp.zeros((1, LANES), F32).at[0, :MOE_GROUPS].set(moe_b_group[i])
        r_b = r_b.at[0, MOE_GROUPS:MOE_GROUPS + MOE_EXPERTS].set(moe_b_expert[i])
        router = (jnp.concatenate([r_hi, r_lo], axis=1), r_b)

        if kind == 0:
            w_in = ssm_w_in[j].astype(BF16)
            n_main = SSM_D_INNER + SSM_CONV_DIM
            zx, dt_raw = _inproj(xa, modv, norm_g[i, 0], w_in[:, :n_main], w_in[:, n_main:], geom)
            xbc = _ssd_conv(zx, ssm_conv_w[j], ssm_conv_b[j], geom)
            a = -jnp.exp(ssm_a_log[j].astype(F32))
            ys_dir = [
                _ssd_scan(xbc, dt_raw, a[dr].reshape(1, SSM_HEADS), ssm_dt_bias[j, dr].reshape(1, SSM_HEADS),
                          geom, reverse=(dr == 1))
                for dr in range(2)
            ]
            dsk = jnp.repeat(ssm_d[j].astype(F32), SSM_HEAD_DIM).reshape(1, SSM_D_INNER)
            a_args = (xbc, ys_dir[0], ys_dir[1], zx, dsk, ssm_norm_g[j].reshape(1, SSM_D_INNER))
            xa, tok_p, logits = _outproj(a_args, ssm_w_out[j].astype(BF16), xa, modv, norm_g[i, 1],
                                         *router, geom, ssd=True)
        elif kind == 1:
            qw, kvw = WIN_HEADS * HEAD_DIM, WIN_KV_HEADS * HEAD_DIM
            qkv = _inproj(xa, modv, norm_g[i, 0], win_w_qkv[j].astype(BF16), None, geom)
            gains = jnp.concatenate([jnp.tile(win_q_g[j], WIN_HEADS), jnp.tile(win_k_g[j], WIN_KV_HEADS)])
            post = jnp.concatenate([jnp.full((qw,), scale, F32), jnp.ones((kvw,), F32)])
            qk = _qk_prep(qkv, gains.reshape(1, -1), post.reshape(1, -1), cos_t, sin_t)
            o = _win_attn(qk, qkv, win_sinks[j].astype(F32), geom)
            xa, tok_p, logits = _outproj((o,), win_w_out[j].astype(BF16), xa, modv, norm_g[i, 1],
                                         *router, geom, ssd=False)
        else:
            lam_init = 0.8 - 0.6 * math.exp(-0.3 * i)
            hw = 2 * DIFF_HEADS * HEAD_DIM
            qkv = _inproj(xa, modv, norm_g[i, 0], diff_w_qkv[j].astype(BF16), None, geom)
            gains = jnp.concatenate([jnp.tile(diff_q_g[j], 2 * DIFF_HEADS), jnp.tile(diff_k_g[j], 2 * DIFF_HEADS)])
            post = jnp.concatenate([jnp.full((hw,), scale * math.log2(math.e), F32), jnp.ones((hw,), F32)])
            qk = _qk_prep(qkv, gains.reshape(1, -1), post.reshape(1, -1), cos_t, sin_t)
            o = _diff_attn(qk, qkv, diff_lam[j].astype(F32), diff_subln_g[j].astype(F32), lam_init, geom)
            xa, tok_p, logits = _outproj((o,), diff_w_out[j].astype(BF16), xa, modv, norm_g[i, 1],
                                         *router, geom, ssd=False)

        xa = _moe(xa, tok_p, logits, modv, moe_w_gate[i].astype(BF16), moe_w_up[i].astype(BF16),
                  moe_w_down[i].astype(BF16), geom)

    return xa[:n_lat].reshape(bsz, s, d)
```

```python
import functools
import math

import jax
import jax.numpy as jnp
from jax import lax
from jax.experimental import pallas as pl
from jax.experimental.pallas import tpu as pltpu

F32 = jnp.float32
BF16 = jnp.bfloat16
I32 = jnp.int32
U32 = jnp.uint32

D_MODEL = 2048
GRID_W = 64
NORM_EPS = 1e-6
ROPE_THETA = 10000.0
N_MIXERS = 3

SSM_D_INNER = 2 * D_MODEL
SSM_HEAD_DIM = 64
SSM_HEADS = SSM_D_INNER // SSM_HEAD_DIM
SSM_STATE = 128
SSM_GROUPS = 8
SSM_CHUNK = 128
SSM_CONV_DIM = SSM_D_INNER + 2 * SSM_GROUPS * SSM_STATE
SSM_GROUP_W = SSM_D_INNER // SSM_GROUPS

HEAD_DIM = 128
WIN_HEADS = D_MODEL // HEAD_DIM
WIN_KV_HEADS = 4
WINDOW = 128
ATT_BLOCK = 128
DIFF_HEADS = D_MODEL // (2 * HEAD_DIM)

MOE_GROUPS = 4
MOE_EPG = 8
MOE_EXPERTS = MOE_GROUPS * MOE_EPG
MOE_D_FF = D_MODEL // 4

LANES = 128
NEG = -1e30
VMEM_LIMIT = 48 * 1024 * 1024

ROW_TILE = 512
MOE_ROWS = 256
COMBINE_ROWS = 256
TOKEN_SUBROWS = D_MODEL // 2 // LANES


def _cparams(*sem):
    return pltpu.CompilerParams(dimension_semantics=sem, vmem_limit_bytes=VMEM_LIMIT)


def _variant_of_tile(i, tiles_per_batch, n_latent_tiles):
    return jnp.where(i < n_latent_tiles, 1 + i // tiles_per_batch, 0)


def _silu(v):
    return v * jax.nn.sigmoid(v)


def _split3(v):
    hi = v.astype(BF16)
    r1 = v - hi.astype(F32)
    mid = r1.astype(BF16)
    lo = (r1 - mid.astype(F32)).astype(BF16)
    return hi, mid, lo


def _mod_kernel(c_ref, w_ref, b_ref, o_ref):
    s = _silu(c_ref[...]).astype(BF16)
    o_ref[...] = jnp.dot(s, w_ref[...].astype(BF16), preferred_element_type=F32) + b_ref[...]


def _modulation(cvecs, ada_w, ada_b):
    depth, d, n6 = ada_w.shape
    tn = 1024
    return pl.pallas_call(
        _mod_kernel,
        grid=(depth, n6 // tn),
        in_specs=[
            pl.BlockSpec((8, d), lambda l, j: (0, 0)),
            pl.BlockSpec((None, d, tn), lambda l, j: (l, 0, j)),
            pl.BlockSpec((None, 1, tn), lambda l, j: (l, 0, j)),
        ],
        out_specs=pl.BlockSpec((None, 8, tn), lambda l, j: (l, 0, j)),
        out_shape=jax.ShapeDtypeStruct((depth, 8, n6), F32),
        compiler_params=_cparams("parallel", "parallel"),
        name="adaln_modulation",
    )(cvecs, ada_w, ada_b.reshape(depth, 1, n6))


def _inproj_kernel(x_ref, mod_ref, g_ref, w_ref, *rest, has_tail):
    if has_tail:
        wt_ref, o_ref, ot_ref, h_ref = rest
    else:
        o_ref, h_ref = rest

    @pl.when(pl.program_id(1) == 0)
    def _():
        x = x_ref[...]
        ms = jnp.mean(x * x, axis=-1, keepdims=True)
        y = x * lax.rsqrt(ms + NORM_EPS) * g_ref[...]
        h = (y * (1.0 + mod_ref[1:2, :]) + mod_ref[0:1, :]).astype(BF16)
        h_ref[...] = h
        if has_tail:
            ot_ref[...] = jnp.dot(h, wt_ref[...], preferred_element_type=F32)

    o_ref[...] = jnp.dot(h_ref[...], w_ref[...], preferred_element_type=F32).astype(o_ref.dtype)


def _inproj(xa, modv, gain, w, w_tail, geom):
    t, d = xa.shape
    n = w.shape[1]
    tm = ROW_TILE
    tn = 2048 if n % 2048 == 0 else 1024
    tpb, nlt = geom["s"] // tm, geom["n_lat"] // tm
    var = lambda i, j: (_variant_of_tile(i, tpb, nlt), 0, 0)
    in_specs = [
        pl.BlockSpec((tm, d), lambda i, j: (i, 0)),
        pl.BlockSpec((None, 8, d), var),
        pl.BlockSpec((1, d), lambda i, j: (0, 0)),
        pl.BlockSpec((d, tn), lambda i, j: (0, j)),
    ]
    out_specs = [pl.BlockSpec((tm, tn), lambda i, j: (i, j))]
    out_shape = [jax.ShapeDtypeStruct((t, n), BF16)]
    args = [xa, modv, gain.reshape(1, d), w]
    if w_tail is not None:
        nt = w_tail.shape[1]
        in_specs.append(pl.BlockSpec((d, nt), lambda i, j: (0, 0)))
        out_specs.append(pl.BlockSpec((tm, nt), lambda i, j: (i, 0)))
        out_shape.append(jax.ShapeDtypeStruct((t, nt), F32))
        args.append(w_tail)
    res = pl.pallas_call(
        functools.partial(_inproj_kernel, has_tail=w_tail is not None),
        grid=(t // tm, n // tn),
        in_specs=in_specs,
        out_specs=out_specs,
        out_shape=out_shape,
        scratch_shapes=[pltpu.VMEM((tm, d), BF16)],
        compiler_params=_cparams("parallel", "arbitrary"),
        name="norm_mod_inproj",
    )(*args)
    return res if w_tail is not None else res[0]


def _conv_kernel(xp_ref, x_ref, xn_ref, w_ref, b_ref, o_ref, *, rows, tiles_lat, n_lat_tiles, tiles_ctx):
    i = pl.program_id(0)
    in_lat = i < n_lat_tiles
    k = jnp.where(in_lat, i % tiles_lat, (i - n_lat_tiles) % tiles_ctx)
    n = jnp.where(in_lat, tiles_lat, tiles_ctx)
    x = x_ref[...].astype(F32)
    prev_row = jnp.where(k == 0, 0.0, xp_ref[...].astype(F32)[15:16, :])
    next_row = jnp.where(k == n - 1, 0.0, xn_ref[...].astype(F32)[0:1, :])
    r = lax.broadcasted_iota(I32, (rows, 1), 0)
    xm1 = jnp.where(r == 0, prev_row, pltpu.roll(x, 1, 0))
    xp1 = jnp.where(r == rows - 1, next_row, pltpu.roll(x, rows - 1, 0))
    out = xm1 * w_ref[0:1, :] + x * w_ref[1:2, :] + xp1 * w_ref[2:3, :] + b_ref[...]
    o_ref[...] = _silu(out).astype(o_ref.dtype)


def _ssd_conv(zx, conv_w, conv_b, geom):
    t = zx.shape[0]
    rows, wc = 256, 2048
    col0 = SSM_D_INNER // wc
    halo = 16
    rb = rows // halo
    last_halo = t // halo - 1
    kern = functools.partial(_conv_kernel, rows=rows, tiles_lat=geom["s"] // rows,
                             n_lat_tiles=geom["n_lat"] // rows, tiles_ctx=geom["c"] // rows)
    return pl.pallas_call(
        kern,
        grid=(t // rows, SSM_CONV_DIM // wc),
        in_specs=[
            pl.BlockSpec((halo, wc), lambda i, j: (jnp.maximum(i * rb - 1, 0), col0 + j)),
            pl.BlockSpec((rows, wc), lambda i, j: (i, col0 + j)),
            pl.BlockSpec((halo, wc), lambda i, j: (jnp.minimum((i + 1) * rb, last_halo), col0 + j)),
            pl.BlockSpec((3, wc), lambda i, j: (0, j)),
            pl.BlockSpec((1, wc), lambda i, j: (0, j)),
        ],
        out_specs=pl.BlockSpec((rows, wc), lambda i, j: (i, j)),
        out_shape=jax.ShapeDtypeStruct((t, SSM_CONV_DIM), BF16),
        compiler_params=_cparams("parallel", "parallel"),
        name="ssd_conv_silu",
    )(zx, zx, zx, conv_w, conv_b.reshape(1, SSM_CONV_DIM))


def _ssd_scan_kernel(xs_ref, b_ref, c_ref, dt_ref, a_ref, dtb_ref, e_ref, y_ref, state_ref, *, reverse):
    q = SSM_CHUNK
    hpg = SSM_HEADS // SSM_GROUPS

    @pl.when(pl.program_id(1) == 0)
    def _():
        state_ref[...] = jnp.zeros_like(state_ref)

    c0 = SSM_HEADS if reverse else 0
    pre = dt_ref[:, c0:c0 + SSM_HEADS] + dtb_ref[...]
    dt = jnp.maximum(pre, 0.0) + jnp.log1p(jnp.exp(-jnp.abs(pre)))
    la = dt * a_ref[...]
    ri = lax.broadcasted_iota(I32, (q, q), 0)
    ci = lax.broadcasted_iota(I32, (q, q), 1)
    tri = (ri <= ci) if reverse else (ri >= ci)
    trib = tri.astype(BF16)
    hi, mid, lo = _split3(la)
    cum = (jnp.dot(trib, hi, preferred_element_type=F32) + jnp.dot(trib, mid, preferred_element_type=F32)
           + jnp.dot(trib, lo, preferred_element_type=F32))
    total = cum[0:1, :] if reverse else cum[q - 1:q, :]
    cum_t = cum.T
    dt_t = dt.T
    lane = lax.broadcasted_iota(I32, (1, LANES), 1)
    first_half = lane < SSM_HEAD_DIM

    def per_head_to_columns(v, with_lo=True):
        hi = v.astype(BF16)
        out = jnp.dot(hi, e_ref[...], preferred_element_type=F32)
        if with_lo:
            lo = (v - hi.astype(F32)).astype(BF16)
            out = out + jnp.dot(lo, e_ref[...], preferred_element_type=F32)
        return out

    exp_cum_cols = per_head_to_columns(jnp.exp(cum))
    w_state_cols = per_head_to_columns(jnp.exp(total - cum) * dt, with_lo=False)
    exp_total_cols = per_head_to_columns(jnp.broadcast_to(jnp.exp(total), (8, SSM_HEADS)))[0:1, :]

    for g in range(SSM_GROUPS):
        bg = b_ref[:, g * SSM_STATE:(g + 1) * SSM_STATE]
        cg = c_ref[:, g * SSM_STATE:(g + 1) * SSM_STATE]
        cb = lax.dot_general(cg, bg, (((1,), (1,)), ((), ())), preferred_element_type=F32)
        st = state_ref[g]
        y_state = jnp.dot(cg, st.astype(BF16), preferred_element_type=F32)
        xw_parts = []
        for pair in range(hpg // 2):
            h0 = g * hpg + 2 * pair
            col = g * SSM_GROUP_W + pair * LANES
            xpair = xs_ref[:, col:col + LANES]
            ws = []
            for h in (h0, h0 + 1):
                seg = cum[:, h:h + 1] - cum_t[h:h + 1, :]
                dec = jnp.exp(jnp.where(tri, seg, NEG))
                ws.append((cb * dec * dt_t[h:h + 1, :]).astype(BF16))
            zero = jnp.zeros_like(xpair)
            x_diag = jnp.concatenate([jnp.where(first_half, xpair, zero), jnp.where(first_half, zero, xpair)],
                                     axis=0)
            y_intra = jnp.dot(jnp.concatenate(ws, axis=1), x_diag, preferred_element_type=F32)
            ysp = y_state[:, pair * LANES:(pair + 1) * LANES]
            y_ref[:, col:col + LANES] = (y_intra + ysp * exp_cum_cols[:, col:col + LANES]).astype(y_ref.dtype)
            xw_parts.append((xpair.astype(F32) * w_state_cols[:, col:col + LANES]).astype(BF16))
        xw = jnp.concatenate(xw_parts, axis=1)
        scale_row = exp_total_cols[:, g * SSM_GROUP_W:(g + 1) * SSM_GROUP_W]
        upd = lax.dot_general(bg, xw, (((0,), (0,)), ((), ())), preferred_element_type=F32)
        state_ref[g] = st * scale_row + upd


def _ssd_scan(xbc, dt_raw, a_dir, dtb_dir, geom, reverse):
    t = xbc.shape[0]
    q = SSM_CHUNK
    bsz = geom["b"]
    cq, lq = geom["c"] // q, geom["s"] // q
    lat_blocks = bsz * lq

    def row_block(b, s):
        cchunk = (cq - 1 - s) if reverse else s
        lchunk = (lq - 1 - (s - cq)) if reverse else (s - cq)
        return jnp.where(s < cq, lat_blocks + b * cq + cchunk, b * lq + lchunk)

    bc_w = SSM_GROUPS * SSM_STATE
    head_of_col = jnp.arange(SSM_D_INNER, dtype=I32) // SSM_HEAD_DIM
    expand = (head_of_col[None, :] == jnp.arange(SSM_HEADS, dtype=I32)[:, None]).astype(BF16)
    return pl.pallas_call(
        functools.partial(_ssd_scan_kernel, reverse=reverse),
        grid=(bsz, cq + lq),
        in_specs=[
            pl.BlockSpec((q, SSM_D_INNER), lambda b, s: (row_block(b, s), 0)),
            pl.BlockSpec((q, bc_w), lambda b, s: (row_block(b, s), SSM_D_INNER // bc_w)),
            pl.BlockSpec((q, bc_w), lambda b, s: (row_block(b, s), SSM_D_INNER // bc_w + 1)),
            pl.BlockSpec((q, 2 * SSM_HEADS), lambda b, s: (row_block(b, s), 0)),
            pl.BlockSpec((1, SSM_HEADS), lambda b, s: (0, 0)),
            pl.BlockSpec((1, SSM_HEADS), lambda b, s: (0, 0)),
            pl.BlockSpec((SSM_HEADS, SSM_D_INNER), lambda b, s: (0, 0)),
        ],
        out_specs=pl.BlockSpec((q, SSM_D_INNER), lambda b, s: (row_block(b, s), 0)),
        out_shape=jax.ShapeDtypeStruct((t, SSM_D_INNER), BF16),
        scratch_shapes=[pltpu.VMEM((SSM_GROUPS, SSM_STATE, SSM_GROUP_W), F32)],
        compiler_params=_cparams("parallel", "arbitrary"),
        name="ssd_scan_rev" if reverse else "ssd_scan_fwd",
    )(xbc, xbc, xbc, dt_raw, a_dir, dtb_dir, expand)


def _qk_prep_kernel(x_ref, g_ref, s_ref, cos_ref, sin_ref, o_ref, *, heads):
    cosv = cos_ref[...]
    sinv = sin_ref[...]
    lane = lax.broadcasted_iota(I32, (1, HEAD_DIM), 1)
    low = (lane % (HEAD_DIM // 2)) < (HEAD_DIM // 4)
    for h in range(heads):
        sl = slice(h * HEAD_DIM, (h + 1) * HEAD_DIM)
        x = x_ref[:, sl].astype(F32)
        ms = jnp.mean(x * x, axis=-1, keepdims=True)
        y = x * lax.rsqrt(ms + NORM_EPS) * g_ref[:, sl]
        partner = jnp.where(low, pltpu.roll(y, HEAD_DIM - HEAD_DIM // 4, 1), pltpu.roll(y, HEAD_DIM // 4, 1))
        o_ref[:, sl] = ((y * cosv + partner * sinv) * s_ref[:, sl]).astype(o_ref.dtype)


def _qk_prep(qkv, gains, post_scale, cos_t, sin_t):
    t = qkv.shape[0]
    n = gains.shape[1]
    rows, wb = 512, 512
    return pl.pallas_call(
        functools.partial(_qk_prep_kernel, heads=wb // HEAD_DIM),
        grid=(t // rows, n // wb),
        in_specs=[
            pl.BlockSpec((rows, wb), lambda i, j: (i, j)),
            pl.BlockSpec((1, wb), lambda i, j: (0, j)),
            pl.BlockSpec((1, wb), lambda i, j: (0, j)),
            pl.BlockSpec((rows, HEAD_DIM), lambda i, j: (i, 0)),
            pl.BlockSpec((rows, HEAD_DIM), lambda i, j: (i, 0)),
        ],
        out_specs=pl.BlockSpec((rows, wb), lambda i, j: (i, j)),
        out_shape=jax.ShapeDtypeStruct((t, n), BF16),
        compiler_params=_cparams("parallel", "parallel"),
        name="qk_norm_rope",
    )(qkv, gains, post_scale, cos_t, sin_t)


def _rope_tables(geom):
    s, t = geom["s"], geom["t"]
    pos = jnp.arange(s)
    row = (pos // GRID_W).astype(F32)
    col = (pos % GRID_W).astype(F32)
    quarter = HEAD_DIM // 4
    inv_freq = ROPE_THETA ** (-jnp.arange(quarter, dtype=F32) / quarter)
    ar = row[:, None] * inv_freq
    ac = col[:, None] * inv_freq
    cos_l = jnp.concatenate([jnp.cos(ar), jnp.cos(ar), jnp.cos(ac), jnp.cos(ac)], axis=1)
    sin_l = jnp.concatenate([-jnp.sin(ar), jnp.sin(ar), -jnp.sin(ac), jnp.sin(ac)], axis=1)
    n_ctx_rows = t - geom["n_lat"]
    cos_t = jnp.concatenate([jnp.tile(cos_l, (geom["b"], 1)), jnp.ones((n_ctx_rows, HEAD_DIM), F32)], axis=0)
    sin_t = jnp.concatenate([jnp.tile(sin_l, (geom["b"], 1)), jnp.zeros((n_ctx_rows, HEAD_DIM), F32)], axis=0)
    return cos_t, sin_t


def _win_attn_kernel(sink_ref, q_ref, kp_ref, kc_ref, kn_ref, vp_ref, vc_ref, vn_ref, kx_ref, vx_ref, o_ref,
                     *, n_lat_blocks, ctx_blocks):
    s = pl.program_id(1)
    grp = WIN_HEADS // WIN_KV_HEADS
    blk = ATT_BLOCK
    is_lat = s >= ctx_blocks
    n = s - ctx_blocks
    rows = grp * blk
    qi = lax.broadcasted_iota(I32, (rows, 3 * blk), 0) % blk
    kk = lax.broadcasted_iota(I32, (rows, 3 * blk), 1)
    rel = kk - blk - qi
    in_band = (rel <= WINDOW) & (rel >= -WINDOW)
    lo = jnp.where(n > 0, 0, blk)
    hi = jnp.where(is_lat, jnp.where(n < n_lat_blocks - 1, 3 * blk, 2 * blk), 0)
    mask = in_band & (kk >= lo) & (kk < hi)
    rowi = lax.broadcasted_iota(I32, (rows, 1), 0)
    nt = (((1,), (1,)), ((), ()))
    for kh in range(WIN_KV_HEADS):
        ks = slice(kh * HEAD_DIM, (kh + 1) * HEAD_DIM)
        k_lat = jnp.concatenate([kp_ref[:, ks], kc_ref[:, ks], kn_ref[:, ks]], axis=0)
        v_lat = jnp.concatenate([vp_ref[:, ks], vc_ref[:, ks], vn_ref[:, ks]], axis=0)
        qg = jnp.concatenate(
            [q_ref[:, (kh * grp + j) * HEAD_DIM:(kh * grp + j + 1) * HEAD_DIM] for j in range(grp)], axis=0)
        s_lat = jnp.where(mask, lax.dot_general(qg, k_lat, nt, preferred_element_type=F32), NEG)
        s_ctx = lax.dot_general(qg, kx_ref[:, ks], nt, preferred_element_type=F32)
        sink = jnp.full((rows, 1), sink_ref[kh * grp], F32)
        for j in range(1, grp):
            sink = jnp.where(rowi >= j * blk, sink_ref[kh * grp + j], sink)
        m = jnp.maximum(jnp.maximum(jnp.max(s_lat, axis=-1, keepdims=True),
                                    jnp.max(s_ctx, axis=-1, keepdims=True)), sink)
        p_lat = jnp.exp(s_lat - m)
        p_ctx = jnp.exp(s_ctx - m)
        denom = (jnp.sum(p_lat, axis=-1, keepdims=True) + jnp.sum(p_ctx, axis=-1, keepdims=True)
                 + jnp.exp(sink - m))
        o = (jnp.dot(p_lat.astype(BF16), v_lat, preferred_element_type=F32)
             + jnp.dot(p_ctx.astype(BF16), vx_ref[:, ks], preferred_element_type=F32)) / denom
        for j in range(grp):
            hq = kh * grp + j
            o_ref[:, hq * HEAD_DIM:(hq + 1) * HEAD_DIM] = o[j * blk:(j + 1) * blk].astype(o_ref.dtype)


def _win_attn(qk, qkv, sinks, geom):
    t = qk.shape[0]
    blk = ATT_BLOCK
    bsz, c = geom["b"], geom["c"]
    nb = geom["s"] // blk
    cb = c // blk
    lat_blocks = bsz * nb
    kvw = WIN_KV_HEADS * HEAD_DIM
    qw = WIN_HEADS * HEAD_DIM
    kcol = qw // kvw
    vcol = (qw + kvw) // kvw

    def qrow(b, s, sk):
        return jnp.where(s < cb, lat_blocks + b * cb + s, b * nb + (s - cb))

    def lat(off):
        def f(b, s, sk):
            n = jnp.clip(s - cb + off, 0, nb - 1)
            return b * nb + n
        return f

    ctx_row = lambda b, s, sk: (geom["n_lat"] // c + b)
    grid_spec = pltpu.PrefetchScalarGridSpec(
        num_scalar_prefetch=1,
        grid=(bsz, cb + nb),
        in_specs=[
            pl.BlockSpec((blk, qw), lambda b, s, sk: (qrow(b, s, sk), 0)),
            pl.BlockSpec((blk, kvw), lambda b, s, sk: (lat(-1)(b, s, sk), kcol)),
            pl.BlockSpec((blk, kvw), lambda b, s, sk: (lat(0)(b, s, sk), kcol)),
            pl.BlockSpec((blk, kvw), lambda b, s, sk: (lat(1)(b, s, sk), kcol)),
            pl.BlockSpec((blk, kvw), lambda b, s, sk: (lat(-1)(b, s, sk), vcol)),
            pl.BlockSpec((blk, kvw), lambda b, s, sk: (lat(0)(b, s, sk), vcol)),
            pl.BlockSpec((blk, kvw), lambda b, s, sk: (lat(1)(b, s, sk), vcol)),
            pl.BlockSpec((c, kvw), lambda b, s, sk: (ctx_row(b, s, sk), kcol)),
            pl.BlockSpec((c, kvw), lambda b, s, sk: (ctx_row(b, s, sk), vcol)),
        ],
        out_specs=pl.BlockSpec((blk, qw), lambda b, s, sk: (qrow(b, s, sk), 0)),
    )
    return pl.pallas_call(
        functools.partial(_win_attn_kernel, n_lat_blocks=nb, ctx_blocks=cb),
        grid_spec=grid_spec,
        out_shape=jax.ShapeDtypeStruct((t, qw), BF16),
        compiler_params=_cparams("parallel", "parallel"),
        name="window_gqa",
    )(sinks, qk, qk, qk, qk, qkv, qkv, qkv, qk, qkv)


def _diff_attn_kernel(*refs, use_latent, lam_init, sub):
    if use_latent:
        q_ref, kx_ref, vx_ref, k_ref, v_ref, lam_ref, g_ref, o_ref, m_ref, l_ref, acc_ref = refs
    else:
        q_ref, kx_ref, vx_ref, lam_ref, g_ref, _, o_ref, m_ref, l_ref, acc_ref = refs
    j = pl.program_id(3)
    nt = (((1,), (1,)), ((), ()))
    hd = HEAD_DIM

    def update(idx, qh, kh, v):
        s = lax.dot_general(qh, kh, nt, preferred_element_type=F32)
        m_old = m_ref[idx]
        m_new = jnp.maximum(m_old, jnp.max(s, axis=-1, keepdims=True))
        alpha = jnp.exp2(m_old - m_new)
        p = jnp.exp2(s - m_new)
        l_ref[idx] = alpha * l_ref[idx] + jnp.sum(p, axis=-1, keepdims=True)
        acc_ref[idx] = alpha * acc_ref[idx] + jnp.dot(p.astype(BF16), v, preferred_element_type=F32)
        m_ref[idx] = m_new

    @pl.when(j == 0)
    def _():
        m_ref[...] = jnp.full_like(m_ref, NEG)
        l_ref[...] = jnp.zeros_like(l_ref)
        acc_ref[...] = jnp.zeros_like(acc_ref)
        for idx in range(2):
            update(idx, q_ref[:, idx * hd:(idx + 1) * hd], kx_ref[:, idx * hd:(idx + 1) * hd], vx_ref[...])

    if use_latent:
        state = [(m_ref[idx], l_ref[idx], acc_ref[idx]) for idx in range(2)]
        for c in range(k_ref.shape[0] // sub):
            rs = slice(c * sub, (c + 1) * sub)
            for idx in range(2):
                m_old, l_old, acc_old = state[idx]
                s = lax.dot_general(q_ref[:, idx * hd:(idx + 1) * hd], k_ref[rs, idx * hd:(idx + 1) * hd], nt,
                                    preferred_element_type=F32)
                m_new = jnp.maximum(m_old, jnp.max(s, axis=-1, keepdims=True))
                alpha = jnp.exp2(m_old - m_new)
                p = jnp.exp2(s - m_new)
                l_new = alpha * l_old + jnp.sum(p, axis=-1, keepdims=True)
                acc_new = alpha * acc_old + jnp.dot(p.astype(BF16), v_ref[rs, :], preferred_element_type=F32)
                state[idx] = (m_new, l_new, acc_new)
        for idx in range(2):
            m_ref[idx], l_ref[idx], acc_ref[idx] = state[idx]

    @pl.when(j == pl.num_programs(3) - 1)
    def _():
        lam = lam_ref[...]
        lam_full = (jnp.exp(jnp.sum(lam[0:1] * lam[1:2], axis=-1, keepdims=True))
                    - jnp.exp(jnp.sum(lam[2:3] * lam[3:4], axis=-1, keepdims=True)) + lam_init)
        o = acc_ref[0] / l_ref[0] - lam_full * (acc_ref[1] / l_ref[1])
        ms = jnp.mean(o * o, axis=-1, keepdims=True)
        o_ref[...] = (o * lax.rsqrt(ms + NORM_EPS) * g_ref[...] * (1.0 - lam_init)).astype(o_ref.dtype)


def _diff_attn(qk, qkv, lam, subln_g, lam_init, geom):
    t = qk.shape[0]
    bsz, s, c = geom["b"], geom["s"], geom["c"]
    pw = 2 * HEAD_DIM
    tq = min(512, s)
    tkb = min(2048, s)
    sub = min(1024, tkb)
    nq, nk = s // tq, s // tkb
    ctx_row = geom["n_lat"] // c
    kcol0, vcol0 = DIFF_HEADS, 2 * DIFF_HEADS
    g2 = subln_g.reshape(1, pw)
    scratch = lambda rows: [pltpu.VMEM((2, rows, 1), F32), pltpu.VMEM((2, rows, 1), F32),
                            pltpu.VMEM((2, rows, pw), F32)]
    o_lat = pl.pallas_call(
        functools.partial(_diff_attn_kernel, use_latent=True, lam_init=lam_init, sub=sub),
        grid=(bsz, DIFF_HEADS, nq, nk),
        in_specs=[
            pl.BlockSpec((tq, pw), lambda b, h, i, j: (b * nq + i, h)),
            pl.BlockSpec((c, pw), lambda b, h, i, j: (ctx_row + b, kcol0 + h)),
            pl.BlockSpec((c, pw), lambda b, h, i, j: (ctx_row + b, vcol0 + h)),
            pl.BlockSpec((tkb, pw), lambda b, h, i, j: (b * nk + j, kcol0 + h)),
            pl.BlockSpec((tkb, pw), lambda b, h, i, j: (b * nk + j, vcol0 + h)),
            pl.BlockSpec((4, HEAD_DIM), lambda b, h, i, j: (0, 0)),
            pl.BlockSpec((1, pw), lambda b, h, i, j: (0, 0)),
        ],
        out_specs=pl.BlockSpec((tq, pw), lambda b, h, i, j: (b * nq + i, h)),
        out_shape=jax.ShapeDtypeStruct((t, D_MODEL), BF16),
        scratch_shapes=scratch(tq),
        compiler_params=_cparams("parallel", "parallel", "parallel", "arbitrary"),
        name="diff_attn_latent",
    )(qk, qk, qkv, qk, qkv, lam, g2)
    return pl.pallas_call(
        functools.partial(_diff_attn_kernel, use_latent=False, lam_init=lam_init, sub=sub),
        grid=(bsz, DIFF_HEADS, 1, 1),
        in_specs=[
            pl.BlockSpec((c, pw), lambda b, h, i, j: (ctx_row + b, h)),
            pl.BlockSpec((c, pw), lambda b, h, i, j: (ctx_row + b, kcol0 + h)),
            pl.BlockSpec((c, pw), lambda b, h, i, j: (ctx_row + b, vcol0 + h)),
            pl.BlockSpec((4, HEAD_DIM), lambda b, h, i, j: (0, 0)),
            pl.BlockSpec((1, pw), lambda b, h, i, j: (0, 0)),
            pl.BlockSpec(memory_space=pl.ANY),
        ],
        out_specs=pl.BlockSpec((c, pw), lambda b, h, i, j: (ctx_row + b, h)),
        out_shape=jax.ShapeDtypeStruct((t, D_MODEL), BF16),
        scratch_shapes=scratch(c),
        input_output_aliases={5: 0},
        compiler_params=_cparams("parallel", "parallel", "parallel", "arbitrary"),
        name="diff_attn_context",
    )(qk, qk, qkv, lam, g2, o_lat)


def _outproj_kernel(*refs, ssd):
    if ssd:
        (xs_ref, yf_ref, yb_ref, z_ref, dsk_ref, ng_ref, w_ref, x_ref, mod_ref, g_ref, rw_ref, rb_ref,
         xo_ref, hp_ref, lg_ref, acc_ref, ssq_ref) = refs
    else:
        (a_ref, w_ref, x_ref, mod_ref, g_ref, rw_ref, rb_ref,
         xo_ref, hp_ref, lg_ref, acc_ref) = refs
    k = pl.program_id(1)

    @pl.when(k == 0)
    def _():
        acc_ref[...] = jnp.zeros_like(acc_ref)
        if ssd:
            ssq_ref[...] = jnp.zeros_like(ssq_ref)

    if ssd:
        y = dsk_ref[...] * xs_ref[...].astype(F32) + yf_ref[...].astype(F32) + yb_ref[...].astype(F32)
        u = y * _silu(z_ref[...].astype(F32))
        ssq_ref[...] += jnp.sum(u * u, axis=-1, keepdims=True)
        a = (u * ng_ref[...]).astype(BF16)
    else:
        a = a_ref[...]
    acc_ref[...] += jnp.dot(a, w_ref[...], preferred_element_type=F32)

    @pl.when(k == pl.num_programs(1) - 1)
    def _():
        y = acc_ref[...]
        if ssd:
            y = y * lax.rsqrt(ssq_ref[...] * (1.0 / SSM_D_INNER) + NORM_EPS)
        xn = x_ref[...] + mod_ref[2:3, :] * y
        xo_ref[...] = xn
        ms = jnp.mean(xn * xn, axis=-1, keepdims=True)
        h = xn * lax.rsqrt(ms + NORM_EPS) * g_ref[...]
        h = h * (1.0 + mod_ref[4:5, :]) + mod_ref[3:4, :]
        hb = h.astype(BF16)
        hl = (h - hb.astype(F32)).astype(BF16)
        r2 = (jnp.dot(hb, rw_ref[...], preferred_element_type=F32)
              + jnp.dot(hl, rw_ref[...], preferred_element_type=F32))
        lg_ref[...] = r2[:, :LANES] + r2[:, LANES:] + rb_ref[...]
        half = D_MODEL // 2
        lo_bits = lax.bitcast_convert_type(hb[:, :half].astype(F32), U32) >> 16
        hi_bits = lax.bitcast_convert_type(hb[:, half:].astype(F32), U32)
        _tile_store(hp_ref, hi_bits | lo_bits, TOKEN_SUBROWS)


def _outproj(a_args, w, xa, modv, gain, r_w2, r_bias, geom, ssd):
    t, d = xa.shape
    kdim = w.shape[0]
    tm, tk = ROW_TILE, 1024
    tpb, nlt = geom["s"] // tm, geom["n_lat"] // tm
    var = lambda i, k: (_variant_of_tile(i, tpb, nlt), 0, 0)
    const = lambda i, k: (0, 0)
    if ssd:
        xbc, yf, yb, zx, dsk, ng = a_args
        zcol = 0
        a_specs = [
            pl.BlockSpec((tm, tk), lambda i, k: (i, k)),
            pl.BlockSpec((tm, tk), lambda i, k: (i, k)),
            pl.BlockSpec((tm, tk), lambda i, k: (i, k)),
            pl.BlockSpec((tm, tk), lambda i, k: (i, zcol + k)),
            pl.BlockSpec((1, tk), lambda i, k: (0, k)),
            pl.BlockSpec((1, tk), lambda i, k: (0, k)),
        ]
        a_in = [xbc, yf, yb, zx, dsk, ng]
        scratch = [pltpu.VMEM((tm, d), F32), pltpu.VMEM((tm, 1), F32)]
    else:
        a_specs = [pl.BlockSpec((tm, tk), lambda i, k: (i, k))]
        a_in = list(a_args)
        scratch = [pltpu.VMEM((tm, d), F32)]
    return pl.pallas_call(
        functools.partial(_outproj_kernel, ssd=ssd),
        grid=(t // tm, kdim // tk),
        in_specs=a_specs + [
            pl.BlockSpec((tk, d), lambda i, k: (k, 0)),
            pl.BlockSpec((tm, d), lambda i, k: (i, 0)),
            pl.BlockSpec((None, 8, d), var),
            pl.BlockSpec((1, d), const),
            pl.BlockSpec((d, 2 * LANES), const),
            pl.BlockSpec((1, LANES), const),
        ],
        out_specs=[
            pl.BlockSpec((tm, d), lambda i, k: (i, 0)),
            pl.BlockSpec((tm * TOKEN_SUBROWS, LANES), lambda i, k: (i, 0)),
            pl.BlockSpec((tm, LANES), lambda i, k: (i, 0)),
        ],
        out_shape=[
            jax.ShapeDtypeStruct((t, d), F32),
            jax.ShapeDtypeStruct((t * TOKEN_SUBROWS, LANES), U32),
            jax.ShapeDtypeStruct((t, LANES), F32),
        ],
        scratch_shapes=scratch,
        compiler_params=_cparams("parallel", "arbitrary"),
        name="outproj_residual_norm",
    )(*a_in, w, xa, modv, gain.reshape(1, d), r_w2, r_bias)


ROUTE_E0, ROUTE_E1, ROUTE_G0, ROUTE_G1, ROUTE_R0, ROUTE_R1 = range(6)


def _route_kernel(lg_ref, o_ref, cnt_ref, carry_ref):
    rows = lg_ref.shape[0]

    @pl.when(pl.program_id(0) == 0)
    def _():
        carry_ref[...] = jnp.zeros_like(carry_ref)

    lg = lg_ref[...]
    lane = lax.broadcasted_iota(I32, (rows, LANES), 1)
    big = jnp.int32(LANES)

    def first_argmax(v, vmax):
        return jnp.min(jnp.where(v == vmax, lane, big), axis=-1, keepdims=True)

    gl = jnp.where(lane < MOE_GROUPS, lg, NEG)
    gmax = jnp.max(gl, axis=-1, keepdims=True)
    gsum = jnp.sum(jnp.exp(gl - gmax), axis=-1, keepdims=True)
    g_sel = first_argmax(gl, gmax)
    g_p = 1.0 / gsum
    e_lo = MOE_GROUPS + g_sel * MOE_EPG
    el = jnp.where((lane >= e_lo) & (lane < e_lo + MOE_EPG), lg, NEG)
    emax = jnp.max(el, axis=-1, keepdims=True)
    esum = jnp.sum(jnp.exp(el - emax), axis=-1, keepdims=True)
    l0 = first_argmax(el, emax)
    el2 = jnp.where(lane == l0, NEG, el)
    emax2 = jnp.max(el2, axis=-1, keepdims=True)
    l1 = first_argmax(el2, emax2)
    p0 = 1.0 / esum
    p1 = jnp.exp(emax2 - emax) / esum
    gate0 = g_p * p0 / (p0 + p1)
    gate1 = g_p * p1 / (p0 + p1)
    e0 = l0 - MOE_GROUPS
    e1 = l1 - MOE_GROUPS
    oh0 = lane == e0
    oh1 = lane == e1
    hits = oh0.astype(F32) + oh1.astype(F32)
    ri = lax.broadcasted_iota(I32, (rows, rows), 0)
    ci = lax.broadcasted_iota(I32, (rows, rows), 1)
    before = (ci < ri).astype(BF16)
    prior = jnp.dot(before, hits.astype(BF16), preferred_element_type=F32) + carry_ref[0:1, :]
    r0 = jnp.sum(jnp.where(oh0, prior, 0.0), axis=-1, keepdims=True)
    r1 = jnp.sum(jnp.where(oh1, prior, 0.0), axis=-1, keepdims=True)
    carry = carry_ref[0:1, :] + jnp.sum(hits, axis=0, keepdims=True)
    carry_ref[0:1, :] = carry
    cnt_ref[...] = jnp.broadcast_to(carry, cnt_ref.shape)
    rec = jnp.where(lane == ROUTE_E0, e0.astype(F32), 0.0)
    rec = jnp.where(lane == ROUTE_E1, e1.astype(F32), rec)
    rec = jnp.where(lane == ROUTE_G0, gate0, rec)
    rec = jnp.where(lane == ROUTE_G1, gate1, rec)
    rec = jnp.where(lane == ROUTE_R0, r0, rec)
    rec = jnp.where(lane == ROUTE_R1, r1, rec)
    o_ref[...] = rec


def _route(logits):
    t = logits.shape[0]
    rows = 512
    return pl.pallas_call(
        _route_kernel,
        grid=(t // rows,),
        in_specs=[pl.BlockSpec((rows, LANES), lambda i: (i, 0))],
        out_specs=[pl.BlockSpec((rows, LANES), lambda i: (i, 0)), pl.BlockSpec((8, LANES), lambda i: (0, 0))],
        out_shape=[jax.ShapeDtypeStruct((t, LANES), F32), jax.ShapeDtypeStruct((8, LANES), F32)],
        scratch_shapes=[pltpu.VMEM((8, LANES), F32)],
        compiler_params=_cparams("arbitrary"),
        name="moe_route",
    )(logits)


def _gather_tiles(idx_smem, islot, src_hbm, dst_ref, sem, n):
    sub = dst_ref.shape[0] // n
    for r in range(n):
        off = pl.multiple_of(idx_smem[islot, r], sub)
        pltpu.make_async_copy(src_hbm.at[pl.ds(off, sub)], dst_ref.at[pl.ds(r * sub, sub)], sem).start()


def _whole_buffer_copy(src_hbm, buf, sem, n):
    del n
    return pltpu.make_async_copy(src_hbm.at[pl.ds(0, buf.shape[0])], buf, sem)


def _gather_step(step, nsteps, idx_hbm, idx_smem, idx_sem, src_hbm, bufs, row_sems, n):
    def idx_copy(s, slot):
        return pltpu.make_async_copy(idx_hbm.at[s], idx_smem.at[slot], idx_sem.at[slot])

    last = nsteps - 1

    @pl.when(step == 0)
    def _():
        idx_copy(0, 0).start()
        idx_copy(0, 0).wait()
        _gather_tiles(idx_smem, 0, src_hbm, bufs.at[0], row_sems.at[0], n)
        idx_copy(jnp.minimum(1, last), 1).start()

    nxt_i, nxt_b = (step + 1) % 3, (step + 1) % 2
    idx_copy(jnp.minimum(step + 1, last), nxt_i).wait()
    _gather_tiles(idx_smem, nxt_i, src_hbm, bufs.at[nxt_b], row_sems.at[nxt_b], n)
    idx_copy(jnp.minimum(step + 2, last), (step + 2) % 3).start()
    _whole_buffer_copy(src_hbm, bufs.at[step % 2], row_sems.at[step % 2], n).wait()


def _gather_drain(step, nsteps, idx_hbm, idx_smem, idx_sem, src_hbm, bufs, row_sems, n):
    @pl.when(step == nsteps - 1)
    def _():
        nxt_b, nxt_i = (step + 1) % 2, (step + 2) % 3
        _whole_buffer_copy(src_hbm, bufs.at[nxt_b], row_sems.at[nxt_b], n).wait()
        pltpu.make_async_copy(idx_hbm.at[0], idx_smem.at[nxt_i], idx_sem.at[nxt_i]).wait()


def _untile(buf, slot, first, n, sub):
    return jnp.concatenate([buf[slot, pl.ds(first * sub + c, n, stride=sub), :] for c in range(sub)], axis=1)


def _tile_store(ref, val, sub):
    n = val.shape[0]
    for c in range(sub):
        ref[pl.ds(c, n, stride=sub), :] = val[:, c * LANES:(c + 1) * LANES]


def _expert_kernel(bexp_ref, nused_ref, idx_hbm, tok_hbm, wg32_ref, wu32_ref, wd32_ref, y_ref,
                   idx_smem, xbuf, idx_sem, row_sems, wg_ref, wu_ref, wd_ref):
    b = pl.program_id(0)
    nused = nused_ref[0]

    @pl.when((b < nused) & ((b == 0) | (bexp_ref[b] != bexp_ref[jnp.maximum(b - 1, 0)])))
    def _():
        wg_ref[...] = wg32_ref[...].astype(BF16)
        wu_ref[...] = wu32_ref[...].astype(BF16)
        wd_ref[...] = wd32_ref[...].astype(BF16)

    @pl.when(b < nused)
    def _():
        rows = MOE_ROWS
        gather_args = (b, nused, idx_hbm, idx_smem, idx_sem, tok_hbm, xbuf, row_sems, rows)
        _gather_step(*gather_args)
        w = _untile(xbuf, b % 2, 0, rows, TOKEN_SUBROWS)
        x_lo = lax.bitcast_convert_type(w << 16, F32).astype(BF16)
        x_hi = lax.bitcast_convert_type(w & jnp.uint32(0xFFFF0000), F32).astype(BF16)
        half = D_MODEL // 2
        gate = (jnp.dot(x_lo, wg_ref[:half, :], preferred_element_type=F32)
                + jnp.dot(x_hi, wg_ref[half:, :], preferred_element_type=F32))
        up = (jnp.dot(x_lo, wu_ref[:half, :], preferred_element_type=F32)
              + jnp.dot(x_hi, wu_ref[half:, :], preferred_element_type=F32))
        hdn = (_silu(gate) * up).astype(BF16)
        y = jnp.dot(hdn, wd_ref[...], preferred_element_type=F32)
        y_ref[...] = y
        _gather_drain(*gather_args)

    @pl.when(b >= nused)
    def _():
        y_ref[...] = jnp.zeros_like(y_ref)


def _experts(block_exp, n_used, slot_tok, tokens_packed, wg, wu, wd, layer):
    n_blocks = slot_tok.shape[0]
    rows = MOE_ROWS
    half = D_MODEL // 2
    sub = TOKEN_SUBROWS
    grid_spec = pltpu.PrefetchScalarGridSpec(
        num_scalar_prefetch=2,
        grid=(n_blocks,),
        in_specs=[
            pl.BlockSpec(memory_space=pl.ANY),
            pl.BlockSpec(memory_space=pl.ANY),
            pl.BlockSpec((None, None, D_MODEL, MOE_D_FF), lambda b, be, nu: (layer, be[b], 0, 0)),
            pl.BlockSpec((None, None, D_MODEL, MOE_D_FF), lambda b, be, nu: (layer, be[b], 0, 0)),
            pl.BlockSpec((None, None, MOE_D_FF, D_MODEL), lambda b, be, nu: (layer, be[b], 0, 0)),
        ],
        out_specs=pl.BlockSpec((rows, D_MODEL), lambda b, be, nu: (b, 0)),
        scratch_shapes=[
            pltpu.SMEM((3, rows), I32),
            pltpu.VMEM((2, rows * sub, half // sub), U32),
            pltpu.SemaphoreType.DMA((3,)),
            pltpu.SemaphoreType.DMA((2,)),
            pltpu.VMEM((D_MODEL, MOE_D_FF), BF16),
            pltpu.VMEM((D_MODEL, MOE_D_FF), BF16),
            pltpu.VMEM((MOE_D_FF, D_MODEL), BF16),
        ],
    )
    return pl.pallas_call(
        _expert_kernel,
        grid_spec=grid_spec,
        out_shape=jax.ShapeDtypeStruct((n_blocks * rows, D_MODEL), F32),
        compiler_params=_cparams("arbitrary"),
        name="moe_experts",
    )(block_exp, n_used, slot_tok, tokens_packed, wg, wu, wd)


def _combine_kernel(idx_hbm, ys_hbm, x_ref, mod_ref, rt_ref, o_ref, idx_smem, ybuf, idx_sem, row_sems):
    i = pl.program_id(0)
    n = COMBINE_ROWS
    gather_args = (i, pl.num_programs(0), idx_hbm, idx_smem, idx_sem, ys_hbm, ybuf, row_sems, 2 * n)
    _gather_step(*gather_args)
    y0 = ybuf[i % 2, pl.ds(0, n), :]
    y1 = ybuf[i % 2, pl.ds(n, n), :]
    rt = rt_ref[...]
    moe = rt[:, ROUTE_G0:ROUTE_G0 + 1] * y0 + rt[:, ROUTE_G1:ROUTE_G1 + 1] * y1
    o_ref[...] = x_ref[...] + mod_ref[5:6, :] * moe
    _gather_drain(*gather_args)


def _combine(dest_tiles, ys, xa, modv, route, geom, rows_out):
    d = xa.shape[1]
    t = rows_out
    n = COMBINE_ROWS
    tpb, nlt = geom["s"] // n, geom["n_lat"] // n
    return pl.pallas_call(
        _combine_kernel,
        grid=(t // n,),
        in_specs=[
            pl.BlockSpec(memory_space=pl.ANY),
            pl.BlockSpec(memory_space=pl.ANY),
            pl.BlockSpec((n, d), lambda i: (i, 0)),
            pl.BlockSpec((None, 8, d), lambda i: (_variant_of_tile(i, tpb, nlt), 0, 0)),
            pl.BlockSpec((n, LANES), lambda i: (i, 0)),
        ],
        out_specs=pl.BlockSpec((n, d), lambda i: (i, 0)),
        out_shape=jax.ShapeDtypeStruct((t, d), F32),
        scratch_shapes=[
            pltpu.SMEM((3, 2 * n), I32),
            pltpu.VMEM((2, 2 * n, d), F32),
            pltpu.SemaphoreType.DMA((3,)),
            pltpu.SemaphoreType.DMA((2,)),
        ],
        compiler_params=_cparams("arbitrary"),
        name="moe_combine_residual",
    )(dest_tiles, ys, xa, modv, route)


def _moe(xa, tokens_packed, logits, modv, wg, wu, wd, layer, geom, rows_out):
    t = xa.shape[0]
    rows = MOE_ROWS
    route, counts = _route(logits)
    cnt = counts[0, :MOE_EXPERTS].astype(I32)
    padded = (cnt + rows - 1) // rows * rows
    pad_end = jnp.cumsum(padded)
    pad_start = pad_end - padded
    n_blocks = (2 * t + MOE_EXPERTS * (rows - 1) + rows - 1) // rows
    eid = route[:, ROUTE_E0:ROUTE_E1 + 1].astype(I32)
    rank = route[:, ROUTE_R0:ROUTE_R1 + 1].astype(I32)
    dest = pad_start[eid] + rank
    tok = jnp.broadcast_to(jnp.arange(t, dtype=I32)[:, None], (t, 2))
    slot_tok = jnp.zeros((n_blocks * rows,), I32).at[dest.reshape(-1)].set(
        tok.reshape(-1), unique_indices=True)
    block_start = jnp.arange(n_blocks, dtype=I32) * rows
    block_exp = jnp.minimum(
        jnp.sum((pad_end[None, :] <= block_start[:, None]).astype(I32), axis=1), MOE_EXPERTS - 1)
    n_used = (pad_end[-1:] // rows).astype(I32)
    ys = _experts(block_exp, n_used, slot_tok.reshape(n_blocks, rows) * TOKEN_SUBROWS, tokens_packed, wg, wu, wd,
                  layer)
    n = COMBINE_ROWS
    dest_tiles = dest.reshape(t // n, n, 2).transpose(0, 2, 1).reshape(t // n, 2 * n)
    return _combine(dest_tiles, ys, xa, modv, route, geom, rows_out)


def kernel(x, c, ctx, c_ctx, ada_w, ada_b, norm_g, ssm_w_in, ssm_conv_w, ssm_conv_b, ssm_a_log, ssm_dt_bias, ssm_d, ssm_norm_g, ssm_w_out, win_w_qkv, win_q_g, win_k_g, win_sinks, win_w_out, diff_w_qkv, diff_q_g, diff_k_g, diff_lam, diff_subln_g, diff_w_out, moe_w_group, moe_b_group, moe_w_expert, moe_b_expert, moe_w_gate, moe_w_up, moe_w_down):
    bsz, s, d = x.shape
    n_ctx = ctx.shape[1]
    depth = ada_w.shape[0]
    n_lat = bsz * s
    t = n_lat + bsz * n_ctx
    geom = {"b": bsz, "s": s, "c": n_ctx, "n_lat": n_lat, "t": t}
    assert d == D_MODEL and 1 + bsz <= 8
    assert s % max(ROW_TILE, 2048 if s >= 2048 else ROW_TILE) == 0 and (bsz * n_ctx) % ROW_TILE == 0
    assert n_ctx % 256 == 0 and s % GRID_W == 0

    xa = jnp.concatenate([x.reshape(n_lat, d), ctx.reshape(bsz * n_ctx, d)], axis=0)
    cvecs = jnp.zeros((8, d), F32).at[0].set(c_ctx).at[1:1 + bsz].set(c)
    mods = _modulation(cvecs, ada_w, ada_b)
    mods = mods.reshape(depth, 8, 6, d)
    mods = jnp.concatenate([mods, jnp.zeros((depth, 8, 2, d), F32)], axis=2)
    cos_t, sin_t = _rope_tables(geom)
    scale = HEAD_DIM ** -0.5

    for i in range(depth):
        kind, j = i % N_MIXERS, i // N_MIXERS
        modv = mods[i]
        r_w = jnp.zeros((d, LANES), F32).at[:, :MOE_GROUPS].set(moe_w_group[i])
        r_w = r_w.at[:, MOE_GROUPS:MOE_GROUPS + MOE_EXPERTS].set(moe_w_expert[i])
        r_hi = r_w.astype(BF16)
        r_lo = (r_w - r_hi.astype(F32)).astype(BF16)
        r_b = jnp.zeros((1, LANES), F32).at[0, :MOE_GROUPS].set(moe_b_group[i])
        r_b = r_b.at[0, MOE_GROUPS:MOE_GROUPS + MOE_EXPERTS].set(moe_b_expert[i])
        router = (jnp.concatenate([r_hi, r_lo], axis=1), r_b)

        if kind == 0:
            w_in = ssm_w_in[j].astype(BF16)
            n_main = SSM_D_INNER + SSM_CONV_DIM
            zx, dt_raw = _inproj(xa, modv, norm_g[i, 0], w_in[:, :n_main], w_in[:, n_main:], geom)
            xbc = _ssd_conv(zx, ssm_conv_w[j], ssm_conv_b[j], geom)
            a = -jnp.exp(ssm_a_log[j].astype(F32))
            ys_dir = [
                _ssd_scan(xbc, dt_raw, a[dr].reshape(1, SSM_HEADS), ssm_dt_bias[j, dr].reshape(1, SSM_HEADS),
                          geom, reverse=(dr == 1))
                for dr in range(2)
            ]
            dsk = jnp.repeat(ssm_d[j].astype(F32), SSM_HEAD_DIM).reshape(1, SSM_D_INNER)
            a_args = (xbc, ys_dir[0], ys_dir[1], zx, dsk, ssm_norm_g[j].reshape(1, SSM_D_INNER))
            xa, tok_p, logits = _outproj(a_args, ssm_w_out[j].astype(BF16), xa, modv, norm_g[i, 1],
                                         *router, geom, ssd=True)
        elif kind == 1:
            qw, kvw = WIN_HEADS * HEAD_DIM, WIN_KV_HEADS * HEAD_DIM
            qkv = _inproj(xa, modv, norm_g[i, 0], win_w_qkv[j].astype(BF16), None, geom)
            gains = jnp.concatenate([jnp.tile(win_q_g[j], WIN_HEADS), jnp.tile(win_k_g[j], WIN_KV_HEADS)])
            post = jnp.concatenate([jnp.full((qw,), scale, F32), jnp.ones((kvw,), F32)])
            qk = _qk_prep(qkv, gains.reshape(1, -1), post.reshape(1, -1), cos_t, sin_t)
            o = _win_attn(qk, qkv, win_sinks[j].astype(F32), geom)
            xa, tok_p, logits = _outproj((o,), win_w_out[j].astype(BF16), xa, modv, norm_g[i, 1],
                                         *router, geom, ssd=False)
        else:
            lam_init = 0.8 - 0.6 * math.exp(-0.3 * i)
            hw = 2 * DIFF_HEADS * HEAD_DIM
            qkv = _inproj(xa, modv, norm_g[i, 0], diff_w_qkv[j].astype(BF16), None, geom)
            gains = jnp.concatenate([jnp.tile(diff_q_g[j], 2 * DIFF_HEADS), jnp.tile(diff_k_g[j], 2 * DIFF_HEADS)])
            post = jnp.concatenate([jnp.full((hw,), scale * math.log2(math.e), F32), jnp.ones((hw,), F32)])
            qk = _qk_prep(qkv, gains.reshape(1, -1), post.reshape(1, -1), cos_t, sin_t)
            o = _diff_attn(qk, qkv, diff_lam[j].astype(F32), diff_subln_g[j].astype(F32), lam_init, geom)
            xa, tok_p, logits = _outproj((o,), diff_w_out[j].astype(BF16), xa, modv, norm_g[i, 1],
                                         *router, geom, ssd=False)

        rows_out = n_lat if i == depth - 1 else t
        xa = _moe(xa, tok_p, logits, modv, moe_w_gate, moe_w_up, moe_w_down, i, geom, rows_out)

    return xa.reshape(bsz, s, d)
```

```python
import functools
import math

import jax
import jax.numpy as jnp
from jax import lax
from jax.experimental import pallas as pl
from jax.experimental.pallas import tpu as pltpu

F32 = jnp.float32
BF16 = jnp.bfloat16
I32 = jnp.int32
U32 = jnp.uint32

D_MODEL = 2048
GRID_W = 64
NORM_EPS = 1e-6
ROPE_THETA = 10000.0
N_MIXERS = 3

SSM_D_INNER = 2 * D_MODEL
SSM_HEAD_DIM = 64
SSM_HEADS = SSM_D_INNER // SSM_HEAD_DIM
SSM_STATE = 128
SSM_GROUPS = 8
SSM_CHUNK = 128
SSM_CONV_DIM = SSM_D_INNER + 2 * SSM_GROUPS * SSM_STATE
SSM_GROUP_W = SSM_D_INNER // SSM_GROUPS

HEAD_DIM = 128
WIN_HEADS = D_MODEL // HEAD_DIM
WIN_KV_HEADS = 4
WINDOW = 128
ATT_BLOCK = 128
DIFF_HEADS = D_MODEL // (2 * HEAD_DIM)

MOE_GROUPS = 4
MOE_EPG = 8
MOE_EXPERTS = MOE_GROUPS * MOE_EPG
MOE_D_FF = D_MODEL // 4

LANES = 128
NEG = -1e30
VMEM_LIMIT = 48 * 1024 * 1024

ROW_TILE = 512
MOE_ROWS = 256
COMBINE_ROWS = 256
TOKEN_SUBROWS = D_MODEL // 2 // LANES


def _cparams(*sem):
    return pltpu.CompilerParams(dimension_semantics=sem, vmem_limit_bytes=VMEM_LIMIT)


def _variant_of_tile(i, tiles_per_batch, n_latent_tiles):
    return jnp.where(i < n_latent_tiles, 1 + i // tiles_per_batch, 0)


def _silu(v):
    return v * jax.nn.sigmoid(v)


def _split3(v):
    hi = v.astype(BF16)
    r1 = v - hi.astype(F32)
    mid = r1.astype(BF16)
    lo = (r1 - mid.astype(F32)).astype(BF16)
    return hi, mid, lo


def _mod_kernel(c_ref, w_ref, b_ref, o_ref):
    s = _silu(c_ref[...]).astype(BF16)
    o_ref[...] = jnp.dot(s, w_ref[...].astype(BF16), preferred_element_type=F32) + b_ref[...]


def _modulation(cvecs, ada_w, ada_b):
    depth, d, n6 = ada_w.shape
    tn = 1024
    return pl.pallas_call(
        _mod_kernel,
        grid=(depth, n6 // tn),
        in_specs=[
            pl.BlockSpec((8, d), lambda l, j: (0, 0)),
            pl.BlockSpec((None, d, tn), lambda l, j: (l, 0, j)),
            pl.BlockSpec((None, 1, tn), lambda l, j: (l, 0, j)),
        ],
        out_specs=pl.BlockSpec((None, 8, tn), lambda l, j: (l, 0, j)),
        out_shape=jax.ShapeDtypeStruct((depth, 8, n6), F32),
        compiler_params=_cparams("parallel", "parallel"),
        name="adaln_modulation",
    )(cvecs, ada_w, ada_b.reshape(depth, 1, n6))


def _inproj_kernel(x_ref, mod_ref, g_ref, w_ref, *rest, has_tail):
    if has_tail:
        wt_ref, o_ref, ot_ref, h_ref = rest
    else:
        o_ref, h_ref = rest

    @pl.when(pl.program_id(1) == 0)
    def _():
        x = x_ref[...]
        ms = jnp.mean(x * x, axis=-1, keepdims=True)
        y = x * lax.rsqrt(ms + NORM_EPS) * g_ref[...]
        h = (y * (1.0 + mod_ref[1:2, :]) + mod_ref[0:1, :]).astype(BF16)
        h_ref[...] = h
        if has_tail:
            ot_ref[...] = jnp.dot(h, wt_ref[...], preferred_element_type=F32)

    o_ref[...] = jnp.dot(h_ref[...], w_ref[...], preferred_element_type=F32).astype(o_ref.dtype)


def _inproj(xa, modv, gain, w, w_tail, geom):
    t, d = xa.shape
    n = w.shape[1]
    tm = ROW_TILE
    tn = 2048 if n % 2048 == 0 else 1024
    tpb, nlt = geom["s"] // tm, geom["n_lat"] // tm
    var = lambda i, j: (_variant_of_tile(i, tpb, nlt), 0, 0)
    in_specs = [
        pl.BlockSpec((tm, d), lambda i, j: (i, 0)),
        pl.BlockSpec((None, 8, d), var),
        pl.BlockSpec((1, d), lambda i, j: (0, 0)),
        pl.BlockSpec((d, tn), lambda i, j: (0, j)),
    ]
    out_specs = [pl.BlockSpec((tm, tn), lambda i, j: (i, j))]
    out_shape = [jax.ShapeDtypeStruct((t, n), BF16)]
    args = [xa, modv, gain.reshape(1, d), w]
    if w_tail is not None:
        nt = w_tail.shape[1]
        in_specs.append(pl.BlockSpec((d, nt), lambda i, j: (0, 0)))
        out_specs.append(pl.BlockSpec((tm, nt), lambda i, j: (i, 0)))
        out_shape.append(jax.ShapeDtypeStruct((t, nt), F32))
        args.append(w_tail)
    res = pl.pallas_call(
        functools.partial(_inproj_kernel, has_tail=w_tail is not None),
        grid=(t // tm, n // tn),
        in_specs=in_specs,
        out_specs=out_specs,
        out_shape=out_shape,
        scratch_shapes=[pltpu.VMEM((tm, d), BF16)],
        compiler_params=_cparams("parallel", "arbitrary"),
        name="norm_mod_inproj",
    )(*args)
    return res if w_tail is not None else res[0]


def _conv_kernel(xp_ref, x_ref, xn_ref, w_ref, b_ref, o_ref, *, rows, tiles_lat, n_lat_tiles, tiles_ctx):
    i = pl.program_id(0)
    in_lat = i < n_lat_tiles
    k = jnp.where(in_lat, i % tiles_lat, (i - n_lat_tiles) % tiles_ctx)
    n = jnp.where(in_lat, tiles_lat, tiles_ctx)
    x = x_ref[...].astype(F32)
    prev_row = jnp.where(k == 0, 0.0, xp_ref[...].astype(F32)[15:16, :])
    next_row = jnp.where(k == n - 1, 0.0, xn_ref[...].astype(F32)[0:1, :])
    r = lax.broadcasted_iota(I32, (rows, 1), 0)
    xm1 = jnp.where(r == 0, prev_row, pltpu.roll(x, 1, 0))
    xp1 = jnp.where(r == rows - 1, next_row, pltpu.roll(x, rows - 1, 0))
    out = xm1 * w_ref[0:1, :] + x * w_ref[1:2, :] + xp1 * w_ref[2:3, :] + b_ref[...]
    o_ref[...] = _silu(out).astype(o_ref.dtype)


def _ssd_conv(zx, conv_w, conv_b, geom):
    t = zx.shape[0]
    rows, wc = 256, 2048
    col0 = SSM_D_INNER // wc
    halo = 16
    rb = rows // halo
    last_halo = t // halo - 1
    kern = functools.partial(_conv_kernel, rows=rows, tiles_lat=geom["s"] // rows,
                             n_lat_tiles=geom["n_lat"] // rows, tiles_ctx=geom["c"] // rows)
    return pl.pallas_call(
        kern,
        grid=(t // rows, SSM_CONV_DIM // wc),
        in_specs=[
            pl.BlockSpec((halo, wc), lambda i, j: (jnp.maximum(i * rb - 1, 0), col0 + j)),
            pl.BlockSpec((rows, wc), lambda i, j: (i, col0 + j)),
            pl.BlockSpec((halo, wc), lambda i, j: (jnp.minimum((i + 1) * rb, last_halo), col0 + j)),
            pl.BlockSpec((3, wc), lambda i, j: (0, j)),
            pl.BlockSpec((1, wc), lambda i, j: (0, j)),
        ],
        out_specs=pl.BlockSpec((rows, wc), lambda i, j: (i, j)),
        out_shape=jax.ShapeDtypeStruct((t, SSM_CONV_DIM), BF16),
        compiler_params=_cparams("parallel", "parallel"),
        name="ssd_conv_silu",
    )(zx, zx, zx, conv_w, conv_b.reshape(1, SSM_CONV_DIM))


def _ssd_scan_kernel(xs_ref, b_ref, c_ref, dt_ref, a_ref, dtb_ref, e_ref, y_ref, state_ref, *, reverse):
    q = SSM_CHUNK
    hpg = SSM_HEADS // SSM_GROUPS

    @pl.when(pl.program_id(1) == 0)
    def _():
        state_ref[...] = jnp.zeros_like(state_ref)

    c0 = SSM_HEADS if reverse else 0
    pre = dt_ref[:, c0:c0 + SSM_HEADS] + dtb_ref[...]
    dt = jnp.maximum(pre, 0.0) + jnp.log1p(jnp.exp(-jnp.abs(pre)))
    la = dt * a_ref[...]
    ri = lax.broadcasted_iota(I32, (q, q), 0)
    ci = lax.broadcasted_iota(I32, (q, q), 1)
    tri = (ri <= ci) if reverse else (ri >= ci)
    trib = tri.astype(BF16)
    hi, mid, lo = _split3(la)
    cum = (jnp.dot(trib, hi, preferred_element_type=F32) + jnp.dot(trib, mid, preferred_element_type=F32)
           + jnp.dot(trib, lo, preferred_element_type=F32))
    total = cum[0:1, :] if reverse else cum[q - 1:q, :]
    cum_t = cum.T
    dt_t = dt.T
    lane = lax.broadcasted_iota(I32, (1, LANES), 1)
    first_half = lane < SSM_HEAD_DIM

    def per_head_to_columns(v, with_lo=True):
        hi = v.astype(BF16)
        out = jnp.dot(hi, e_ref[...], preferred_element_type=F32)
        if with_lo:
            lo = (v - hi.astype(F32)).astype(BF16)
            out = out + jnp.dot(lo, e_ref[...], preferred_element_type=F32)
        return out

    exp_cum_cols = per_head_to_columns(jnp.exp(cum))
    w_state_cols = per_head_to_columns(jnp.exp(total - cum) * dt, with_lo=False)
    exp_total_cols = per_head_to_columns(jnp.broadcast_to(jnp.exp(total), (8, SSM_HEADS)))[0:1, :]

    for g in range(SSM_GROUPS):
        bg = b_ref[:, g * SSM_STATE:(g + 1) * SSM_STATE]
        cg = c_ref[:, g * SSM_STATE:(g + 1) * SSM_STATE]
        cb = lax.dot_general(cg, bg, (((1,), (1,)), ((), ())), preferred_element_type=F32)
        st = state_ref[g]
        y_state = jnp.dot(cg, st.astype(BF16), preferred_element_type=F32)
        xw_parts = []
        for pair in range(hpg // 2):
            h0 = g * hpg + 2 * pair
            col = g * SSM_GROUP_W + pair * LANES
            xpair = xs_ref[:, col:col + LANES]
            ws = []
            for h in (h0, h0 + 1):
                seg = cum[:, h:h + 1] - cum_t[h:h + 1, :]
                dec = jnp.exp(jnp.where(tri, seg, NEG))
                ws.append((cb * dec * dt_t[h:h + 1, :]).astype(BF16))
            zero = jnp.zeros_like(xpair)
            x_diag = jnp.concatenate([jnp.where(first_half, xpair, zero), jnp.where(first_half, zero, xpair)],
                                     axis=0)
            y_intra = jnp.dot(jnp.concatenate(ws, axis=1), x_diag, preferred_element_type=F32)
            ysp = y_state[:, pair * LANES:(pair + 1) * LANES]
            y_ref[:, col:col + LANES] = (y_intra + ysp * exp_cum_cols[:, col:col + LANES]).astype(y_ref.dtype)
            xw_parts.append((xpair.astype(F32) * w_state_cols[:, col:col + LANES]).astype(BF16))
        xw = jnp.concatenate(xw_parts, axis=1)
        scale_row = exp_total_cols[:, g * SSM_GROUP_W:(g + 1) * SSM_GROUP_W]
        upd = lax.dot_general(bg, xw, (((0,), (0,)), ((), ())), preferred_element_type=F32)
        state_ref[g] = st * scale_row + upd


def _ssd_scan(xbc, dt_raw, a_dir, dtb_dir, geom, reverse):
    t = xbc.shape[0]
    q = SSM_CHUNK
    bsz = geom["b"]
    cq, lq = geom["c"] // q, geom["s"] // q
    lat_blocks = bsz * lq

    def row_block(b, s):
        cchunk = (cq - 1 - s) if reverse else s
        lchunk = (lq - 1 - (s - cq)) if reverse else (s - cq)
        return jnp.where(s < cq, lat_blocks + b * cq + cchunk, b * lq + lchunk)

    bc_w = SSM_GROUPS * SSM_STATE
    head_of_col = jnp.arange(SSM_D_INNER, dtype=I32) // SSM_HEAD_DIM
    expand = (head_of_col[None, :] == jnp.arange(SSM_HEADS, dtype=I32)[:, None]).astype(BF16)
    return pl.pallas_call(
        functools.partial(_ssd_scan_kernel, reverse=reverse),
        grid=(bsz, cq + lq),
        in_specs=[
            pl.BlockSpec((q, SSM_D_INNER), lambda b, s: (row_block(b, s), 0)),
            pl.BlockSpec((q, bc_w), lambda b, s: (row_block(b, s), SSM_D_INNER // bc_w)),
            pl.BlockSpec((q, bc_w), lambda b, s: (row_block(b, s), SSM_D_INNER // bc_w + 1)),
            pl.BlockSpec((q, 2 * SSM_HEADS), lambda b, s: (row_block(b, s), 0)),
            pl.BlockSpec((1, SSM_HEADS), lambda b, s: (0, 0)),
            pl.BlockSpec((1, SSM_HEADS), lambda b, s: (0, 0)),
            pl.BlockSpec((SSM_HEADS, SSM_D_INNER), lambda b, s: (0, 0)),
        ],
        out_specs=pl.BlockSpec((q, SSM_D_INNER), lambda b, s: (row_block(b, s), 0)),
        out_shape=jax.ShapeDtypeStruct((t, SSM_D_INNER), BF16),
        scratch_shapes=[pltpu.VMEM((SSM_GROUPS, SSM_STATE, SSM_GROUP_W), F32)],
        compiler_params=_cparams("parallel", "arbitrary"),
        name="ssd_scan_rev" if reverse else "ssd_scan_fwd",
    )(xbc, xbc, xbc, dt_raw, a_dir, dtb_dir, expand)


def _qk_prep_kernel(x_ref, g_ref, s_ref, cos_ref, sin_ref, o_ref, n_ref, *, heads):
    j = pl.program_id(1)
    cosv = cos_ref[...]
    sinv = sin_ref[...]
    lane = lax.broadcasted_iota(I32, (1, HEAD_DIM), 1)
    low = (lane % (HEAD_DIM // 2)) < (HEAD_DIM // 4)

    @pl.when(j == 0)
    def _():
        n_ref[...] = jnp.zeros_like(n_ref)

    sq_norms = n_ref[...]
    for h in range(heads):
        sl = slice(h * HEAD_DIM, (h + 1) * HEAD_DIM)
        x = x_ref[:, sl].astype(F32)
        ms = jnp.mean(x * x, axis=-1, keepdims=True)
        y = x * lax.rsqrt(ms + NORM_EPS) * g_ref[:, sl]
        partner = jnp.where(low, pltpu.roll(y, HEAD_DIM - HEAD_DIM // 4, 1), pltpu.roll(y, HEAD_DIM // 4, 1))
        ob = ((y * cosv + partner * sinv) * s_ref[:, sl]).astype(o_ref.dtype)
        o_ref[:, sl] = ob
        of = ob.astype(F32)
        sq_norms = jnp.where(lane == j * heads + h, jnp.sum(of * of, axis=-1, keepdims=True), sq_norms)
    n_ref[...] = sq_norms


def _qk_prep(qkv, gains, post_scale, cos_t, sin_t):
    t = qkv.shape[0]
    n = gains.shape[1]
    rows, wb = 512, 512
    return pl.pallas_call(
        functools.partial(_qk_prep_kernel, heads=wb // HEAD_DIM),
        grid=(t // rows, n // wb),
        in_specs=[
            pl.BlockSpec((rows, wb), lambda i, j: (i, j)),
            pl.BlockSpec((1, wb), lambda i, j: (0, j)),
            pl.BlockSpec((1, wb), lambda i, j: (0, j)),
            pl.BlockSpec((rows, HEAD_DIM), lambda i, j: (i, 0)),
            pl.BlockSpec((rows, HEAD_DIM), lambda i, j: (i, 0)),
        ],
        out_specs=[pl.BlockSpec((rows, wb), lambda i, j: (i, j)),
                   pl.BlockSpec((rows, HEAD_DIM), lambda i, j: (i, 0))],
        out_shape=[jax.ShapeDtypeStruct((t, n), BF16), jax.ShapeDtypeStruct((t, HEAD_DIM), F32)],
        compiler_params=_cparams("parallel", "arbitrary"),
        name="qk_norm_rope",
    )(qkv, gains, post_scale, cos_t, sin_t)


def _rope_tables(geom):
    s, t = geom["s"], geom["t"]
    pos = jnp.arange(s)
    row = (pos // GRID_W).astype(F32)
    col = (pos % GRID_W).astype(F32)
    quarter = HEAD_DIM // 4
    inv_freq = ROPE_THETA ** (-jnp.arange(quarter, dtype=F32) / quarter)
    ar = row[:, None] * inv_freq
    ac = col[:, None] * inv_freq
    cos_l = jnp.concatenate([jnp.cos(ar), jnp.cos(ar), jnp.cos(ac), jnp.cos(ac)], axis=1)
    sin_l = jnp.concatenate([-jnp.sin(ar), jnp.sin(ar), -jnp.sin(ac), jnp.sin(ac)], axis=1)
    n_ctx_rows = t - geom["n_lat"]
    cos_t = jnp.concatenate([jnp.tile(cos_l, (geom["b"], 1)), jnp.ones((n_ctx_rows, HEAD_DIM), F32)], axis=0)
    sin_t = jnp.concatenate([jnp.tile(sin_l, (geom["b"], 1)), jnp.zeros((n_ctx_rows, HEAD_DIM), F32)], axis=0)
    return cos_t, sin_t


def _win_attn_kernel(sink_ref, q_ref, kp_ref, kc_ref, kn_ref, vp_ref, vc_ref, vn_ref, kx_ref, vx_ref, o_ref,
                     *, n_lat_blocks, ctx_blocks):
    s = pl.program_id(1)
    grp = WIN_HEADS // WIN_KV_HEADS
    blk = ATT_BLOCK
    is_lat = s >= ctx_blocks
    n = s - ctx_blocks
    rows = grp * blk
    qi = lax.broadcasted_iota(I32, (rows, 3 * blk), 0) % blk
    kk = lax.broadcasted_iota(I32, (rows, 3 * blk), 1)
    rel = kk - blk - qi
    in_band = (rel <= WINDOW) & (rel >= -WINDOW)
    lo = jnp.where(n > 0, 0, blk)
    hi = jnp.where(is_lat, jnp.where(n < n_lat_blocks - 1, 3 * blk, 2 * blk), 0)
    mask = in_band & (kk >= lo) & (kk < hi)
    rowi = lax.broadcasted_iota(I32, (rows, 1), 0)
    nt = (((1,), (1,)), ((), ()))
    for kh in range(WIN_KV_HEADS):
        ks = slice(kh * HEAD_DIM, (kh + 1) * HEAD_DIM)
        k_lat = jnp.concatenate([kp_ref[:, ks], kc_ref[:, ks], kn_ref[:, ks]], axis=0)
        v_lat = jnp.concatenate([vp_ref[:, ks], vc_ref[:, ks], vn_ref[:, ks]], axis=0)
        qg = jnp.concatenate(
            [q_ref[:, (kh * grp + j) * HEAD_DIM:(kh * grp + j + 1) * HEAD_DIM] for j in range(grp)], axis=0)
        s_lat = jnp.where(mask, lax.dot_general(qg, k_lat, nt, preferred_element_type=F32), NEG)
        s_ctx = lax.dot_general(qg, kx_ref[:, ks], nt, preferred_element_type=F32)
        sink = jnp.full((rows, 1), sink_ref[kh * grp], F32)
        for j in range(1, grp):
            sink = jnp.where(rowi >= j * blk, sink_ref[kh * grp + j], sink)
        m = jnp.maximum(jnp.maximum(jnp.max(s_lat, axis=-1, keepdims=True),
                                    jnp.max(s_ctx, axis=-1, keepdims=True)), sink)
        p_lat = jnp.exp(s_lat - m)
        p_ctx = jnp.exp(s_ctx - m)
        denom = (jnp.sum(p_lat, axis=-1, keepdims=True) + jnp.sum(p_ctx, axis=-1, keepdims=True)
                 + jnp.exp(sink - m))
        o = (jnp.dot(p_lat.astype(BF16), v_lat, preferred_element_type=F32)
             + jnp.dot(p_ctx.astype(BF16), vx_ref[:, ks], preferred_element_type=F32)) / denom
        for j in range(grp):
            hq = kh * grp + j
            o_ref[:, hq * HEAD_DIM:(hq + 1) * HEAD_DIM] = o[j * blk:(j + 1) * blk].astype(o_ref.dtype)


def _win_attn(qk, qkv, sinks, geom):
    t = qk.shape[0]
    blk = ATT_BLOCK
    bsz, c = geom["b"], geom["c"]
    nb = geom["s"] // blk
    cb = c // blk
    lat_blocks = bsz * nb
    kvw = WIN_KV_HEADS * HEAD_DIM
    qw = WIN_HEADS * HEAD_DIM
    kcol = qw // kvw
    vcol = (qw + kvw) // kvw

    def qrow(b, s, sk):
        return jnp.where(s < cb, lat_blocks + b * cb + s, b * nb + (s - cb))

    def lat(off):
        def f(b, s, sk):
            n = jnp.clip(s - cb + off, 0, nb - 1)
            return b * nb + n
        return f

    ctx_row = lambda b, s, sk: (geom["n_lat"] // c + b)
    grid_spec = pltpu.PrefetchScalarGridSpec(
        num_scalar_prefetch=1,
        grid=(bsz, cb + nb),
        in_specs=[
            pl.BlockSpec((blk, qw), lambda b, s, sk: (qrow(b, s, sk), 0)),
            pl.BlockSpec((blk, kvw), lambda b, s, sk: (lat(-1)(b, s, sk), kcol)),
            pl.BlockSpec((blk, kvw), lambda b, s, sk: (lat(0)(b, s, sk), kcol)),
            pl.BlockSpec((blk, kvw), lambda b, s, sk: (lat(1)(b, s, sk), kcol)),
            pl.BlockSpec((blk, kvw), lambda b, s, sk: (lat(-1)(b, s, sk), vcol)),
            pl.BlockSpec((blk, kvw), lambda b, s, sk: (lat(0)(b, s, sk), vcol)),
            pl.BlockSpec((blk, kvw), lambda b, s, sk: (lat(1)(b, s, sk), vcol)),
            pl.BlockSpec((c, kvw), lambda b, s, sk: (ctx_row(b, s, sk), kcol)),
            pl.BlockSpec((c, kvw), lambda b, s, sk: (ctx_row(b, s, sk), vcol)),
        ],
        out_specs=pl.BlockSpec((blk, qw), lambda b, s, sk: (qrow(b, s, sk), 0)),
    )
    return pl.pallas_call(
        functools.partial(_win_attn_kernel, n_lat_blocks=nb, ctx_blocks=cb),
        grid_spec=grid_spec,
        out_shape=jax.ShapeDtypeStruct((t, qw), BF16),
        compiler_params=_cparams("parallel", "parallel"),
        name="window_gqa",
    )(sinks, qk, qk, qk, qk, qkv, qkv, qkv, qk, qkv)


DIFF_SAFE_EXCESS = 96.0


def _diff_attn_kernel(*refs, use_latent, lam_init, sub):
    if use_latent:
        (kmax_ref, q_ref, kx_ref, vx_ref, k_ref, v_ref, qn_ref, lam_ref, g_ref, o_ref,
         m_ref, l_ref, acc_ref, stat_ref) = refs
    else:
        q_ref, kx_ref, vx_ref, lam_ref, g_ref, _, o_ref, m_ref, l_ref, acc_ref = refs
    j = pl.program_id(3)
    nt = (((1,), (1,)), ((), ()))
    hd = HEAD_DIM

    def update(idx, qh, kh, v):
        s = lax.dot_general(qh, kh, nt, preferred_element_type=F32)
        m_old = m_ref[idx]
        m_new = jnp.maximum(m_old, jnp.max(s, axis=-1, keepdims=True))
        alpha = jnp.exp2(m_old - m_new)
        p = jnp.exp2(s - m_new)
        l_ref[idx] = alpha * l_ref[idx] + jnp.sum(p, axis=-1, keepdims=True)
        acc_ref[idx] = alpha * acc_ref[idx] + jnp.dot(p.astype(BF16), v, preferred_element_type=F32)
        m_ref[idx] = m_new

    @pl.when(j == 0)
    def _():
        m_ref[...] = jnp.full_like(m_ref, NEG)
        l_ref[...] = jnp.zeros_like(l_ref)
        acc_ref[...] = jnp.zeros_like(acc_ref)
        for idx in range(2):
            update(idx, q_ref[:, idx * hd:(idx + 1) * hd], kx_ref[:, idx * hd:(idx + 1) * hd], vx_ref[...])
        if use_latent:
            lane = lax.broadcasted_iota(I32, qn_ref.shape, 1)
            for idx in range(2):
                head = 2 * pl.program_id(1) + idx
                q_sq = jnp.sum(jnp.where(lane == head, qn_ref[...], 0.0), axis=-1, keepdims=True)
                stat_ref[idx] = jnp.sqrt(jnp.max(q_sq))
                stat_ref[2 + idx] = jnp.min(m_ref[idx])

    if use_latent:
        b, h = pl.program_id(0), pl.program_id(1)
        excess = []
        for idx in range(2):
            k_norm = kmax_ref[(b * pl.num_programs(3) + j) * (2 * DIFF_HEADS) + 2 * h + idx]
            excess.append(stat_ref[idx] * k_norm - stat_ref[2 + idx])
        safe = jnp.maximum(excess[0], excess[1]) <= DIFF_SAFE_EXCESS

        @pl.when(safe)
        def _():
            for idx in range(2):
                qh = q_ref[:, idx * hd:(idx + 1) * hd]
                shift = m_ref[idx]
                ps = []
                lsum = jnp.zeros_like(shift)
                for c in range(k_ref.shape[0] // sub):
                    rs = slice(c * sub, (c + 1) * sub)
                    s = lax.dot_general(qh, k_ref[rs, idx * hd:(idx + 1) * hd], nt, preferred_element_type=F32)
                    p = jnp.exp2(s - shift)
                    lsum = lsum + jnp.sum(p, axis=-1, keepdims=True)
                    ps.append(p.astype(BF16))
                l_ref[idx] += lsum
                acc_ref[idx] += jnp.dot(jnp.concatenate(ps, axis=1), v_ref[...], preferred_element_type=F32)

        @pl.when(jnp.logical_not(safe))
        def _():
            for idx in range(2):
                update(idx, q_ref[:, idx * hd:(idx + 1) * hd], k_ref[:, idx * hd:(idx + 1) * hd], v_ref[...])
                stat_ref[2 + idx] = jnp.min(m_ref[idx])

    @pl.when(j == pl.num_programs(3) - 1)
    def _():
        lam = lam_ref[...]
        lam_full = (jnp.exp(jnp.sum(lam[0:1] * lam[1:2], axis=-1, keepdims=True))
                    - jnp.exp(jnp.sum(lam[2:3] * lam[3:4], axis=-1, keepdims=True)) + lam_init)
        o = acc_ref[0] / l_ref[0] - lam_full * (acc_ref[1] / l_ref[1])
        ms = jnp.mean(o * o, axis=-1, keepdims=True)
        o_ref[...] = (o * lax.rsqrt(ms + NORM_EPS) * g_ref[...] * (1.0 - lam_init)).astype(o_ref.dtype)


def _diff_attn(qk, qkv, norms, lam, subln_g, lam_init, geom):
    t = qk.shape[0]
    bsz, s, c = geom["b"], geom["s"], geom["c"]
    pw = 2 * HEAD_DIM
    tq = min(512, s)
    tkb = min(2048, s)
    sub = min(256, tkb)
    nq, nk = s // tq, s // tkb
    ctx_row = geom["n_lat"] // c
    kcol0, vcol0 = DIFF_HEADS, 2 * DIFF_HEADS
    g2 = subln_g.reshape(1, pw)
    scratch = lambda rows: [pltpu.VMEM((2, rows, 1), F32), pltpu.VMEM((2, rows, 1), F32),
                            pltpu.VMEM((2, rows, pw), F32)]
    nh = 2 * DIFF_HEADS
    k_sq = norms[:geom["n_lat"], nh:2 * nh].reshape(bsz, nk, tkb, nh)
    kmax = (jnp.sqrt(jnp.max(k_sq, axis=2)) * 1.001).reshape(-1)
    grid_spec = pltpu.PrefetchScalarGridSpec(
        num_scalar_prefetch=1,
        grid=(bsz, DIFF_HEADS, nq, nk),
        in_specs=[
            pl.BlockSpec((tq, pw), lambda b, h, i, j, km: (b * nq + i, h)),
            pl.BlockSpec((c, pw), lambda b, h, i, j, km: (ctx_row + b, kcol0 + h)),
            pl.BlockSpec((c, pw), lambda b, h, i, j, km: (ctx_row + b, vcol0 + h)),
            pl.BlockSpec((tkb, pw), lambda b, h, i, j, km: (b * nk + j, kcol0 + h)),
            pl.BlockSpec((tkb, pw), lambda b, h, i, j, km: (b * nk + j, vcol0 + h)),
            pl.BlockSpec((tq, LANES), lambda b, h, i, j, km: (b * nq + i, 0)),
            pl.BlockSpec((4, HEAD_DIM), lambda b, h, i, j, km: (0, 0)),
            pl.BlockSpec((1, pw), lambda b, h, i, j, km: (0, 0)),
        ],
        out_specs=pl.BlockSpec((tq, pw), lambda b, h, i, j, km: (b * nq + i, h)),
        scratch_shapes=scratch(tq) + [pltpu.SMEM((4,), F32)],
    )
    o_lat = pl.pallas_call(
        functools.partial(_diff_attn_kernel, use_latent=True, lam_init=lam_init, sub=sub),
        grid_spec=grid_spec,
        out_shape=jax.ShapeDtypeStruct((t, D_MODEL), BF16),
        compiler_params=_cparams("parallel", "parallel", "parallel", "arbitrary"),
        name="diff_attn_latent",
    )(kmax, qk, qk, qkv, qk, qkv, norms, lam, g2)
    return pl.pallas_call(
        functools.partial(_diff_attn_kernel, use_latent=False, lam_init=lam_init, sub=sub),
        grid=(bsz, DIFF_HEADS, 1, 1),
        in_specs=[
            pl.BlockSpec((c, pw), lambda b, h, i, j: (ctx_row + b, h)),
            pl.BlockSpec((c, pw), lambda b, h, i, j: (ctx_row + b, kcol0 + h)),
            pl.BlockSpec((c, pw), lambda b, h, i, j: (ctx_row + b, vcol0 + h)),
            pl.BlockSpec((4, HEAD_DIM), lambda b, h, i, j: (0, 0)),
            pl.BlockSpec((1, pw), lambda b, h, i, j: (0, 0)),
            pl.BlockSpec(memory_space=pl.ANY),
        ],
        out_specs=pl.BlockSpec((c, pw), lambda b, h, i, j: (ctx_row + b, h)),
        out_shape=jax.ShapeDtypeStruct((t, D_MODEL), BF16),
        scratch_shapes=scratch(c),
        input_output_aliases={5: 0},
        compiler_params=_cparams("parallel", "parallel", "parallel", "arbitrary"),
        name="diff_attn_context",
    )(qk, qk, qkv, lam, g2, o_lat)


def _outproj_kernel(*refs, ssd):
    if ssd:
        (xs_ref, yf_ref, yb_ref, z_ref, dsk_ref, ng_ref, w_ref, x_ref, mod_ref, g_ref, rw_ref, rb_ref,
         xo_ref, hp_ref, lg_ref, acc_ref, ssq_ref) = refs
    else:
        (a_ref, w_ref, x_ref, mod_ref, g_ref, rw_ref, rb_ref,
         xo_ref, hp_ref, lg_ref, acc_ref) = refs
    k = pl.program_id(1)

    @pl.when(k == 0)
    def _():
        acc_ref[...] = jnp.zeros_like(acc_ref)
        if ssd:
            ssq_ref[...] = jnp.zeros_like(ssq_ref)

    if ssd:
        y = dsk_ref[...] * xs_ref[...].astype(F32) + yf_ref[...].astype(F32) + yb_ref[...].astype(F32)
        u = y * _silu(z_ref[...].astype(F32))
        ssq_ref[...] += jnp.sum(u * u, axis=-1, keepdims=True)
        a = (u * ng_ref[...]).astype(BF16)
    else:
        a = a_ref[...]
    acc_ref[...] += jnp.dot(a, w_ref[...], preferred_element_type=F32)

    @pl.when(k == pl.num_programs(1) - 1)
    def _():
        y = acc_ref[...]
        if ssd:
            y = y * lax.rsqrt(ssq_ref[...] * (1.0 / SSM_D_INNER) + NORM_EPS)
        xn = x_ref[...] + mod_ref[2:3, :] * y
        xo_ref[...] = xn
        ms = jnp.mean(xn * xn, axis=-1, keepdims=True)
        h = xn * lax.rsqrt(ms + NORM_EPS) * g_ref[...]
        h = h * (1.0 + mod_ref[4:5, :]) + mod_ref[3:4, :]
        hb = h.astype(BF16)
        hl = (h - hb.astype(F32)).astype(BF16)
        r2 = (jnp.dot(hb, rw_ref[...], preferred_element_type=F32)
              + jnp.dot(hl, rw_ref[...], preferred_element_type=F32))
        lg_ref[...] = r2[:, :LANES] + r2[:, LANES:] + rb_ref[...]
        half = D_MODEL // 2
        lo_bits = lax.bitcast_convert_type(hb[:, :half].astype(F32), U32) >> 16
        hi_bits = lax.bitcast_convert_type(hb[:, half:].astype(F32), U32)
        _tile_store(hp_ref, hi_bits | lo_bits, TOKEN_SUBROWS)


def _outproj(a_args, w, xa, modv, gain, r_w2, r_bias, geom, ssd):
    t, d = xa.shape
    kdim = w.shape[0]
    tm, tk = ROW_TILE, 1024
    tpb, nlt = geom["s"] // tm, geom["n_lat"] // tm
    var = lambda i, k: (_variant_of_tile(i, tpb, nlt), 0, 0)
    const = lambda i, k: (0, 0)
    if ssd:
        xbc, yf, yb, zx, dsk, ng = a_args
        zcol = 0
        a_specs = [
            pl.BlockSpec((tm, tk), lambda i, k: (i, k)),
            pl.BlockSpec((tm, tk), lambda i, k: (i, k)),
            pl.BlockSpec((tm, tk), lambda i, k: (i, k)),
            pl.BlockSpec((tm, tk), lambda i, k: (i, zcol + k)),
            pl.BlockSpec((1, tk), lambda i, k: (0, k)),
            pl.BlockSpec((1, tk), lambda i, k: (0, k)),
        ]
        a_in = [xbc, yf, yb, zx, dsk, ng]
        scratch = [pltpu.VMEM((tm, d), F32), pltpu.VMEM((tm, 1), F32)]
    else:
        a_specs = [pl.BlockSpec((tm, tk), lambda i, k: (i, k))]
        a_in = list(a_args)
        scratch = [pltpu.VMEM((tm, d), F32)]
    return pl.pallas_call(
        functools.partial(_outproj_kernel, ssd=ssd),
        grid=(t // tm, kdim // tk),
        in_specs=a_specs + [
            pl.BlockSpec((tk, d), lambda i, k: (k, 0)),
            pl.BlockSpec((tm, d), lambda i, k: (i, 0)),
            pl.BlockSpec((None, 8, d), var),
            pl.BlockSpec((1, d), const),
            pl.BlockSpec((d, 2 * LANES), const),
            pl.BlockSpec((1, LANES), const),
        ],
        out_specs=[
            pl.BlockSpec((tm, d), lambda i, k: (i, 0)),
            pl.BlockSpec((tm * TOKEN_SUBROWS, LANES), lambda i, k: (i, 0)),
            pl.BlockSpec((tm, LANES), lambda i, k: (i, 0)),
        ],
        out_shape=[
            jax.ShapeDtypeStruct((t, d), F32),
            jax.ShapeDtypeStruct((t * TOKEN_SUBROWS, LANES), U32),
            jax.ShapeDtypeStruct((t, LANES), F32),
        ],
        scratch_shapes=scratch,
        compiler_params=_cparams("parallel", "arbitrary"),
        name="outproj_residual_norm",
    )(*a_in, w, xa, modv, gain.reshape(1, d), r_w2, r_bias)


ROUTE_E0, ROUTE_E1, ROUTE_G0, ROUTE_G1, ROUTE_R0, ROUTE_R1 = range(6)


def _route_kernel(lg_ref, o_ref, cnt_ref, carry_ref):
    rows = lg_ref.shape[0]

    @pl.when(pl.program_id(0) == 0)
    def _():
        carry_ref[...] = jnp.zeros_like(carry_ref)

    lg = lg_ref[...]
    lane = lax.broadcasted_iota(I32, (rows, LANES), 1)
    big = jnp.int32(LANES)

    def first_argmax(v, vmax):
        return jnp.min(jnp.where(v == vmax, lane, big), axis=-1, keepdims=True)

    gl = jnp.where(lane < MOE_GROUPS, lg, NEG)
    gmax = jnp.max(gl, axis=-1, keepdims=True)
    gsum = jnp.sum(jnp.exp(gl - gmax), axis=-1, keepdims=True)
    g_sel = first_argmax(gl, gmax)
    g_p = 1.0 / gsum
    e_lo = MOE_GROUPS + g_sel * MOE_EPG
    el = jnp.where((lane >= e_lo) & (lane < e_lo + MOE_EPG), lg, NEG)
    emax = jnp.max(el, axis=-1, keepdims=True)
    esum = jnp.sum(jnp.exp(el - emax), axis=-1, keepdims=True)
    l0 = first_argmax(el, emax)
    el2 = jnp.where(lane == l0, NEG, el)
    emax2 = jnp.max(el2, axis=-1, keepdims=True)
    l1 = first_argmax(el2, emax2)
    p0 = 1.0 / esum
    p1 = jnp.exp(emax2 - emax) / esum
    gate0 = g_p * p0 / (p0 + p1)
    gate1 = g_p * p1 / (p0 + p1)
    e0 = l0 - MOE_GROUPS
    e1 = l1 - MOE_GROUPS
    oh0 = lane == e0
    oh1 = lane == e1
    hits = oh0.astype(F32) + oh1.astype(F32)
    ri = lax.broadcasted_iota(I32, (rows, rows), 0)
    ci = lax.broadcasted_iota(I32, (rows, rows), 1)
    before = (ci < ri).astype(BF16)
    prior = jnp.dot(before, hits.astype(BF16), preferred_element_type=F32) + carry_ref[0:1, :]
    r0 = jnp.sum(jnp.where(oh0, prior, 0.0), axis=-1, keepdims=True)
    r1 = jnp.sum(jnp.where(oh1, prior, 0.0), axis=-1, keepdims=True)
    carry = carry_ref[0:1, :] + jnp.sum(hits, axis=0, keepdims=True)
    carry_ref[0:1, :] = carry
    cnt_ref[...] = jnp.broadcast_to(carry, cnt_ref.shape)
    rec = jnp.where(lane == ROUTE_E0, e0.astype(F32), 0.0)
    rec = jnp.where(lane == ROUTE_E1, e1.astype(F32), rec)
    rec = jnp.where(lane == ROUTE_G0, gate0, rec)
    rec = jnp.where(lane == ROUTE_G1, gate1, rec)
    rec = jnp.where(lane == ROUTE_R0, r0, rec)
    rec = jnp.where(lane == ROUTE_R1, r1, rec)
    o_ref[...] = rec


def _route(logits):
    t = logits.shape[0]
    rows = 512
    return pl.pallas_call(
        _route_kernel,
        grid=(t // rows,),
        in_specs=[pl.BlockSpec((rows, LANES), lambda i: (i, 0))],
        out_specs=[pl.BlockSpec((rows, LANES), lambda i: (i, 0)), pl.BlockSpec((8, LANES), lambda i: (0, 0))],
        out_shape=[jax.ShapeDtypeStruct((t, LANES), F32), jax.ShapeDtypeStruct((8, LANES), F32)],
        scratch_shapes=[pltpu.VMEM((8, LANES), F32)],
        compiler_params=_cparams("arbitrary"),
        name="moe_route",
    )(logits)


def _gather_tiles(idx_smem, islot, src_hbm, dst_ref, sem, n):
    sub = dst_ref.shape[0] // n
    for r in range(n):
        off = pl.multiple_of(idx_smem[islot, r], sub)
        pltpu.make_async_copy(src_hbm.at[pl.ds(off, sub)], dst_ref.at[pl.ds(r * sub, sub)], sem).start()


def _whole_buffer_copy(src_hbm, buf, sem, n):
    del n
    return pltpu.make_async_copy(src_hbm.at[pl.ds(0, buf.shape[0])], buf, sem)


def _gather_step(step, nsteps, idx_hbm, idx_smem, idx_sem, src_hbm, bufs, row_sems, n):
    def idx_copy(s, slot):
        return pltpu.make_async_copy(idx_hbm.at[s], idx_smem.at[slot], idx_sem.at[slot])

    last = nsteps - 1

    @pl.when(step == 0)
    def _():
        idx_copy(0, 0).start()
        idx_copy(0, 0).wait()
        _gather_tiles(idx_smem, 0, src_hbm, bufs.at[0], row_sems.at[0], n)
        idx_copy(jnp.minimum(1, last), 1).start()

    nxt_i, nxt_b = (step + 1) % 3, (step + 1) % 2
    idx_copy(jnp.minimum(step + 1, last), nxt_i).wait()
    _gather_tiles(idx_smem, nxt_i, src_hbm, bufs.at[nxt_b], row_sems.at[nxt_b], n)
    idx_copy(jnp.minimum(step + 2, last), (step + 2) % 3).start()
    _whole_buffer_copy(src_hbm, bufs.at[step % 2], row_sems.at[step % 2], n).wait()


def _gather_drain(step, nsteps, idx_hbm, idx_smem, idx_sem, src_hbm, bufs, row_sems, n):
    @pl.when(step == nsteps - 1)
    def _():
        nxt_b, nxt_i = (step + 1) % 2, (step + 2) % 3
        _whole_buffer_copy(src_hbm, bufs.at[nxt_b], row_sems.at[nxt_b], n).wait()
        pltpu.make_async_copy(idx_hbm.at[0], idx_smem.at[nxt_i], idx_sem.at[nxt_i]).wait()


def _untile(buf, slot, first, n, sub):
    return jnp.concatenate([buf[slot, pl.ds(first * sub + c, n, stride=sub), :] for c in range(sub)], axis=1)


def _tile_store(ref, val, sub):
    n = val.shape[0]
    for c in range(sub):
        ref[pl.ds(c, n, stride=sub), :] = val[:, c * LANES:(c + 1) * LANES]


def _expert_kernel(bexp_ref, nused_ref, idx_hbm, tok_hbm, wg32_ref, wu32_ref, wd32_ref, y_ref,
                   idx_smem, xbuf, idx_sem, row_sems, wg_ref, wu_ref, wd_ref):
    b = pl.program_id(0)
    nused = nused_ref[0]

    @pl.when((b < nused) & ((b == 0) | (bexp_ref[b] != bexp_ref[jnp.maximum(b - 1, 0)])))
    def _():
        wg_ref[...] = wg32_ref[...].astype(BF16)
        wu_ref[...] = wu32_ref[...].astype(BF16)
        wd_ref[...] = wd32_ref[...].astype(BF16)

    @pl.when(b < nused)
    def _():
        rows = MOE_ROWS
        gather_args = (b, nused, idx_hbm, idx_smem, idx_sem, tok_hbm, xbuf, row_sems, rows)
        _gather_step(*gather_args)
        w = _untile(xbuf, b % 2, 0, rows, TOKEN_SUBROWS)
        x_lo = lax.bitcast_convert_type(w << 16, F32).astype(BF16)
        x_hi = lax.bitcast_convert_type(w & jnp.uint32(0xFFFF0000), F32).astype(BF16)
        half = D_MODEL // 2
        gate = (jnp.dot(x_lo, wg_ref[:half, :], preferred_element_type=F32)
                + jnp.dot(x_hi, wg_ref[half:, :], preferred_element_type=F32))
        up = (jnp.dot(x_lo, wu_ref[:half, :], preferred_element_type=F32)
              + jnp.dot(x_hi, wu_ref[half:, :], preferred_element_type=F32))
        hdn = (_silu(gate) * up).astype(BF16)
        y = jnp.dot(hdn, wd_ref[...], preferred_element_type=F32)
        y_ref[...] = y
        _gather_drain(*gather_args)

    @pl.when(b >= nused)
    def _():
        y_ref[...] = jnp.zeros_like(y_ref)


def _experts(block_exp, n_used, slot_tok, tokens_packed, wg, wu, wd, layer):
    n_blocks = slot_tok.shape[0]
    rows = MOE_ROWS
    half = D_MODEL // 2
    sub = TOKEN_SUBROWS
    grid_spec = pltpu.PrefetchScalarGridSpec(
        num_scalar_prefetch=2,
        grid=(n_blocks,),
        in_specs=[
            pl.BlockSpec(memory_space=pl.ANY),
            pl.BlockSpec(memory_space=pl.ANY),
            pl.BlockSpec((None, None, D_MODEL, MOE_D_FF), lambda b, be, nu: (layer, be[b], 0, 0)),
            pl.BlockSpec((None, None, D_MODEL, MOE_D_FF), lambda b, be, nu: (layer, be[b], 0, 0)),
            pl.BlockSpec((None, None, MOE_D_FF, D_MODEL), lambda b, be, nu: (layer, be[b], 0, 0)),
        ],
        out_specs=pl.BlockSpec((rows, D_MODEL), lambda b, be, nu: (b, 0)),
        scratch_shapes=[
            pltpu.SMEM((3, rows), I32),
            pltpu.VMEM((2, rows * sub, half // sub), U32),
            pltpu.SemaphoreType.DMA((3,)),
            pltpu.SemaphoreType.DMA((2,)),
            pltpu.VMEM((D_MODEL, MOE_D_FF), BF16),
            pltpu.VMEM((D_MODEL, MOE_D_FF), BF16),
            pltpu.VMEM((MOE_D_FF, D_MODEL), BF16),
        ],
    )
    return pl.pallas_call(
        _expert_kernel,
        grid_spec=grid_spec,
        out_shape=jax.ShapeDtypeStruct((n_blocks * rows, D_MODEL), F32),
        compiler_params=_cparams("arbitrary"),
        name="moe_experts",
    )(block_exp, n_used, slot_tok, tokens_packed, wg, wu, wd)


def _combine_kernel(idx_hbm, ys_hbm, x_ref, mod_ref, rt_ref, o_ref, idx_smem, ybuf, idx_sem, row_sems):
    i = pl.program_id(0)
    n = COMBINE_ROWS
    gather_args = (i, pl.num_programs(0), idx_hbm, idx_smem, idx_sem, ys_hbm, ybuf, row_sems, 2 * n)
    _gather_step(*gather_args)
    y0 = ybuf[i % 2, pl.ds(0, n), :]
    y1 = ybuf[i % 2, pl.ds(n, n), :]
    rt = rt_ref[...]
    moe = rt[:, ROUTE_G0:ROUTE_G0 + 1] * y0 + rt[:, ROUTE_G1:ROUTE_G1 + 1] * y1
    o_ref[...] = x_ref[...] + mod_ref[5:6, :] * moe
    _gather_drain(*gather_args)


def _combine(dest_tiles, ys, xa, modv, route, geom, rows_out):
    d = xa.shape[1]
    t = rows_out
    n = COMBINE_ROWS
    tpb, nlt = geom["s"] // n, geom["n_lat"] // n
    return pl.pallas_call(
        _combine_kernel,
        grid=(t // n,),
        in_specs=[
            pl.BlockSpec(memory_space=pl.ANY),
            pl.BlockSpec(memory_space=pl.ANY),
            pl.BlockSpec((n, d), lambda i: (i, 0)),
            pl.BlockSpec((None, 8, d), lambda i: (_variant_of_tile(i, tpb, nlt), 0, 0)),
            pl.BlockSpec((n, LANES), lambda i: (i, 0)),
        ],
        out_specs=pl.BlockSpec((n, d), lambda i: (i, 0)),
        out_shape=jax.ShapeDtypeStruct((t, d), F32),
        scratch_shapes=[
            pltpu.SMEM((3, 2 * n), I32),
            pltpu.VMEM((2, 2 * n, d), F32),
            pltpu.SemaphoreType.DMA((3,)),
            pltpu.SemaphoreType.DMA((2,)),
        ],
        compiler_params=_cparams("arbitrary"),
        name="moe_combine_residual",
    )(dest_tiles, ys, xa, modv, route)


def _moe(xa, tokens_packed, logits, modv, wg, wu, wd, layer, geom, rows_out):
    t = xa.shape[0]
    rows = MOE_ROWS
    route, counts = _route(logits)
    cnt = counts[0, :MOE_EXPERTS].astype(I32)
    padded = (cnt + rows - 1) // rows * rows
    pad_end = jnp.cumsum(padded)
    pad_start = pad_end - padded
    n_blocks = (2 * t + MOE_EXPERTS * (rows - 1) + rows - 1) // rows
    eid = route[:, ROUTE_E0:ROUTE_E1 + 1].astype(I32)
    rank = route[:, ROUTE_R0:ROUTE_R1 + 1].astype(I32)
    dest = pad_start[eid] + rank
    tok = jnp.broadcast_to(jnp.arange(t, dtype=I32)[:, None], (t, 2))
    slot_tok = jnp.zeros((n_blocks * rows,), I32).at[dest.reshape(-1)].set(
        tok.reshape(-1), unique_indices=True)
    block_start = jnp.arange(n_blocks, dtype=I32) * rows
    block_exp = jnp.minimum(
        jnp.sum((pad_end[None, :] <= block_start[:, None]).astype(I32), axis=1), MOE_EXPERTS - 1)
    n_used = (pad_end[-1:] // rows).astype(I32)
    ys = _experts(block_exp, n_used, slot_tok.reshape(n_blocks, rows) * TOKEN_SUBROWS, tokens_packed, wg, wu, wd,
                  layer)
    n = COMBINE_ROWS
    dest_tiles = dest.reshape(t // n, n, 2).transpose(0, 2, 1).reshape(t // n, 2 * n)
    return _combine(dest_tiles, ys, xa, modv, route, geom, rows_out)


def kernel(x, c, ctx, c_ctx, ada_w, ada_b, norm_g, ssm_w_in, ssm_conv_w, ssm_conv_b, ssm_a_log, ssm_dt_bias, ssm_d, ssm_norm_g, ssm_w_out, win_w_qkv, win_q_g, win_k_g, win_sinks, win_w_out, diff_w_qkv, diff_q_g, diff_k_g, diff_lam, diff_subln_g, diff_w_out, moe_w_group, moe_b_group, moe_w_expert, moe_b_expert, moe_w_gate, moe_w_up, moe_w_down):
    bsz, s, d = x.shape
    n_ctx = ctx.shape[1]
    depth = ada_w.shape[0]
    n_lat = bsz * s
    t = n_lat + bsz * n_ctx
    geom = {"b": bsz, "s": s, "c": n_ctx, "n_lat": n_lat, "t": t}
    assert d == D_MODEL and 1 + bsz <= 8
    assert s % max(ROW_TILE, 2048 if s >= 2048 else ROW_TILE) == 0 and (bsz * n_ctx) % ROW_TILE == 0
    assert n_ctx % 256 == 0 and s % GRID_W == 0

    xa = jnp.concatenate([x.reshape(n_lat, d), ctx.reshape(bsz * n_ctx, d)], axis=0)
    cvecs = jnp.zeros((8, d), F32).at[0].set(c_ctx).at[1:1 + bsz].set(c)
    mods = _modulation(cvecs, ada_w, ada_b)
    mods = mods.reshape(depth, 8, 6, d)
    mods = jnp.concatenate([mods, jnp.zeros((depth, 8, 2, d), F32)], axis=2)
    cos_t, sin_t = _rope_tables(geom)
    scale = HEAD_DIM ** -0.5

    for i in range(depth):
        kind, j = i % N_MIXERS, i // N_MIXERS
        modv = mods[i]
        r_w = jnp.zeros((d, LANES), F32).at[:, :MOE_GROUPS].set(moe_w_group[i])
        r_w = r_w.at[:, MOE_GROUPS:MOE_GROUPS + MOE_EXPERTS].set(moe_w_expert[i])
        r_hi = r_w.astype(BF16)
        r_lo = (r_w - r_hi.astype(F32)).astype(BF16)
        r_b = jnp.zeros((1, LANES), F32).at[0, :MOE_GROUPS].set(moe_b_group[i])
        r_b = r_b.at[0, MOE_GROUPS:MOE_GROUPS + MOE_EXPERTS].set(moe_b_expert[i])
        router = (jnp.concatenate([r_hi, r_lo], axis=1), r_b)

        if kind == 0:
            w_in = ssm_w_in[j].astype(BF16)
            n_main = SSM_D_INNER + SSM_CONV_DIM
            zx, dt_raw = _inproj(xa, modv, norm_g[i, 0], w_in[:, :n_main], w_in[:, n_main:], geom)
            xbc = _ssd_conv(zx, ssm_conv_w[j], ssm_conv_b[j], geom)
            a = -jnp.exp(ssm_a_log[j].astype(F32))
            ys_dir = [
                _ssd_scan(xbc, dt_raw, a[dr].reshape(1, SSM_HEADS), ssm_dt_bias[j, dr].reshape(1, SSM_HEADS),
                          geom, reverse=(dr == 1))
                for dr in range(2)
            ]
            dsk = jnp.repeat(ssm_d[j].astype(F32), SSM_HEAD_DIM).reshape(1, SSM_D_INNER)
            a_args = (xbc, ys_dir[0], ys_dir[1], zx, dsk, ssm_norm_g[j].reshape(1, SSM_D_INNER))
            xa, tok_p, logits = _outproj(a_args, ssm_w_out[j].astype(BF16), xa, modv, norm_g[i, 1],
                                         *router, geom, ssd=True)
        elif kind == 1:
            qw, kvw = WIN_HEADS * HEAD_DIM, WIN_KV_HEADS * HEAD_DIM
            qkv = _inproj(xa, modv, norm_g[i, 0], win_w_qkv[j].astype(BF16), None, geom)
            gains = jnp.concatenate([jnp.tile(win_q_g[j], WIN_HEADS), jnp.tile(win_k_g[j], WIN_KV_HEADS)])
            post = jnp.concatenate([jnp.full((qw,), scale, F32), jnp.ones((kvw,), F32)])
            qk, _ = _qk_prep(qkv, gains.reshape(1, -1), post.reshape(1, -1), cos_t, sin_t)
            o = _win_attn(qk, qkv, win_sinks[j].astype(F32), geom)
            xa, tok_p, logits = _outproj((o,), win_w_out[j].astype(BF16), xa, modv, norm_g[i, 1],
                                         *router, geom, ssd=False)
        else:
            lam_init = 0.8 - 0.6 * math.exp(-0.3 * i)
            hw = 2 * DIFF_HEADS * HEAD_DIM
            qkv = _inproj(xa, modv, norm_g[i, 0], diff_w_qkv[j].astype(BF16), None, geom)
            gains = jnp.concatenate([jnp.tile(diff_q_g[j], 2 * DIFF_HEADS), jnp.tile(diff_k_g[j], 2 * DIFF_HEADS)])
            post = jnp.concatenate([jnp.full((hw,), scale * math.log2(math.e), F32), jnp.ones((hw,), F32)])
            qk, norms = _qk_prep(qkv, gains.reshape(1, -1), post.reshape(1, -1), cos_t, sin_t)
            o = _diff_attn(qk, qkv, norms, diff_lam[j].astype(F32), diff_subln_g[j].astype(F32), lam_init, geom)
            xa, tok_p, logits = _outproj((o,), diff_w_out[j].astype(BF16), xa, modv, norm_g[i, 1],
                                         *router, geom, ssd=False)

        rows_out = n_lat if i == depth - 1 else t
        xa = _moe(xa, tok_p, logits, modv, moe_w_gate, moe_w_up, moe_w_down, i, geom, rows_out)

    return xa.reshape(bsz, s, d)
```

```python
import functools
import math

import jax
import jax.numpy as jnp
from jax import lax
from jax.experimental import pallas as pl
from jax.experimental.pallas import tpu as pltpu

F32 = jnp.float32
BF16 = jnp.bfloat16
I32 = jnp.int32
U32 = jnp.uint32

D_MODEL = 2048
GRID_W = 64
NORM_EPS = 1e-6
ROPE_THETA = 10000.0
N_MIXERS = 3

SSM_D_INNER = 2 * D_MODEL
SSM_HEAD_DIM = 64
SSM_HEADS = SSM_D_INNER // SSM_HEAD_DIM
SSM_STATE = 128
SSM_GROUPS = 8
SSM_CHUNK = 128
SSM_CONV_DIM = SSM_D_INNER + 2 * SSM_GROUPS * SSM_STATE
SSM_GROUP_W = SSM_D_INNER // SSM_GROUPS

HEAD_DIM = 128
WIN_HEADS = D_MODEL // HEAD_DIM
WIN_KV_HEADS = 4
WINDOW = 128
ATT_BLOCK = 128
DIFF_HEADS = D_MODEL // (2 * HEAD_DIM)

MOE_GROUPS = 4
MOE_EPG = 8
MOE_EXPERTS = MOE_GROUPS * MOE_EPG
MOE_D_FF = D_MODEL // 4

LANES = 128
NEG = -1e30
VMEM_LIMIT = 48 * 1024 * 1024

ROW_TILE = 512
MOE_ROWS = 256
COMBINE_ROWS = 256
TOKEN_SUBROWS = D_MODEL // 2 // LANES


def _cparams(*sem):
    return pltpu.CompilerParams(dimension_semantics=sem, vmem_limit_bytes=VMEM_LIMIT)


def _variant_of_tile(i, tiles_per_batch, n_latent_tiles):
    return jnp.where(i < n_latent_tiles, 1 + i // tiles_per_batch, 0)


def _silu(v):
    return v * jax.nn.sigmoid(v)


def _split3(v):
    hi = v.astype(BF16)
    r1 = v - hi.astype(F32)
    mid = r1.astype(BF16)
    lo = (r1 - mid.astype(F32)).astype(BF16)
    return hi, mid, lo


def _mod_kernel(c_ref, w_ref, b_ref, o_ref):
    s = _silu(c_ref[...]).astype(BF16)
    o_ref[...] = jnp.dot(s, w_ref[...].astype(BF16), preferred_element_type=F32) + b_ref[...]


def _modulation(cvecs, ada_w, ada_b):
    depth, d, n6 = ada_w.shape
    tn = 1024
    return pl.pallas_call(
        _mod_kernel,
        grid=(depth, n6 // tn),
        in_specs=[
            pl.BlockSpec((8, d), lambda l, j: (0, 0)),
            pl.BlockSpec((None, d, tn), lambda l, j: (l, 0, j)),
            pl.BlockSpec((None, 1, tn), lambda l, j: (l, 0, j)),
        ],
        out_specs=pl.BlockSpec((None, 8, tn), lambda l, j: (l, 0, j)),
        out_shape=jax.ShapeDtypeStruct((depth, 8, n6), F32),
        compiler_params=_cparams("parallel", "parallel"),
        name="adaln_modulation",
    )(cvecs, ada_w, ada_b.reshape(depth, 1, n6))


def _inproj_kernel(x_ref, mod_ref, g_ref, w_ref, *rest, has_tail):
    if has_tail:
        wt_ref, o_ref, ot_ref, h_ref = rest
    else:
        o_ref, h_ref = rest

    @pl.when(pl.program_id(1) == 0)
    def _():
        x = x_ref[...]
        ms = jnp.mean(x * x, axis=-1, keepdims=True)
        y = x * lax.rsqrt(ms + NORM_EPS) * g_ref[...]
        h = (y * (1.0 + mod_ref[1:2, :]) + mod_ref[0:1, :]).astype(BF16)
        h_ref[...] = h
        if has_tail:
            ot_ref[...] = jnp.dot(h, wt_ref[...], preferred_element_type=F32)

    o_ref[...] = jnp.dot(h_ref[...], w_ref[...], preferred_element_type=F32).astype(o_ref.dtype)


def _inproj(xa, modv, gain, w, w_tail, geom):
    t, d = xa.shape
    n = w.shape[1]
    tm = ROW_TILE
    tn = 2048 if n % 2048 == 0 else 1024
    tpb, nlt = geom["s"] // tm, geom["n_lat"] // tm
    var = lambda i, j: (_variant_of_tile(i, tpb, nlt), 0, 0)
    in_specs = [
        pl.BlockSpec((tm, d), lambda i, j: (i, 0)),
        pl.BlockSpec((None, 8, d), var),
        pl.BlockSpec((1, d), lambda i, j: (0, 0)),
        pl.BlockSpec((d, tn), lambda i, j: (0, j)),
    ]
    out_specs = [pl.BlockSpec((tm, tn), lambda i, j: (i, j))]
    out_shape = [jax.ShapeDtypeStruct((t, n), BF16)]
    args = [xa, modv, gain.reshape(1, d), w]
    if w_tail is not None:
        nt = w_tail.shape[1]
        in_specs.append(pl.BlockSpec((d, nt), lambda i, j: (0, 0)))
        out_specs.append(pl.BlockSpec((tm, nt), lambda i, j: (i, 0)))
        out_shape.append(jax.ShapeDtypeStruct((t, nt), F32))
        args.append(w_tail)
    res = pl.pallas_call(
        functools.partial(_inproj_kernel, has_tail=w_tail is not None),
        grid=(t // tm, n // tn),
        in_specs=in_specs,
        out_specs=out_specs,
        out_shape=out_shape,
        scratch_shapes=[pltpu.VMEM((tm, d), BF16)],
        compiler_params=_cparams("parallel", "arbitrary"),
        name="norm_mod_inproj",
    )(*args)
    return res if w_tail is not None else res[0]


def _conv_kernel(xp_ref, x_ref, xn_ref, w_ref, b_ref, o_ref, *, rows, tiles_lat, n_lat_tiles, tiles_ctx):
    i = pl.program_id(0)
    in_lat = i < n_lat_tiles
    k = jnp.where(in_lat, i % tiles_lat, (i - n_lat_tiles) % tiles_ctx)
    n = jnp.where(in_lat, tiles_lat, tiles_ctx)
    x = x_ref[...].astype(F32)
    prev_row = jnp.where(k == 0, 0.0, xp_ref[...].astype(F32)[15:16, :])
    next_row = jnp.where(k == n - 1, 0.0, xn_ref[...].astype(F32)[0:1, :])
    r = lax.broadcasted_iota(I32, (rows, 1), 0)
    xm1 = jnp.where(r == 0, prev_row, pltpu.roll(x, 1, 0))
    xp1 = jnp.where(r == rows - 1, next_row, pltpu.roll(x, rows - 1, 0))
    out = xm1 * w_ref[0:1, :] + x * w_ref[1:2, :] + xp1 * w_ref[2:3, :] + b_ref[...]
    o_ref[...] = _silu(out).astype(o_ref.dtype)


def _ssd_conv(zx, conv_w, conv_b, geom):
    t = zx.shape[0]
    rows, wc = 256, 2048
    col0 = SSM_D_INNER // wc
    halo = 16
    rb = rows // halo
    last_halo = t // halo - 1
    kern = functools.partial(_conv_kernel, rows=rows, tiles_lat=geom["s"] // rows,
                             n_lat_tiles=geom["n_lat"] // rows, tiles_ctx=geom["c"] // rows)
    return pl.pallas_call(
        kern,
        grid=(t // rows, SSM_CONV_DIM // wc),
        in_specs=[
            pl.BlockSpec((halo, wc), lambda i, j: (jnp.maximum(i * rb - 1, 0), col0 + j)),
            pl.BlockSpec((rows, wc), lambda i, j: (i, col0 + j)),
            pl.BlockSpec((halo, wc), lambda i, j: (jnp.minimum((i + 1) * rb, last_halo), col0 + j)),
            pl.BlockSpec((3, wc), lambda i, j: (0, j)),
            pl.BlockSpec((1, wc), lambda i, j: (0, j)),
        ],
        out_specs=pl.BlockSpec((rows, wc), lambda i, j: (i, j)),
        out_shape=jax.ShapeDtypeStruct((t, SSM_CONV_DIM), BF16),
        compiler_params=_cparams("parallel", "parallel"),
        name="ssd_conv_silu",
    )(zx, zx, zx, conv_w, conv_b.reshape(1, SSM_CONV_DIM))


def _ssd_scan_kernel(xf_ref, bf_ref, cf_ref, dtf_ref, xr_ref, br_ref, cr_ref, dtr_ref, a_ref, dtb_ref, e_ref,
                     yf_ref, yr_ref, state_ref):
    @pl.when(pl.program_id(1) == 0)
    def _():
        state_ref[...] = jnp.zeros_like(state_ref)

    _ssd_chunk(xf_ref, bf_ref, cf_ref, dtf_ref, a_ref.at[0:1], dtb_ref.at[0:1], e_ref, yf_ref, state_ref.at[0],
               reverse=False)
    _ssd_chunk(xr_ref, br_ref, cr_ref, dtr_ref, a_ref.at[1:2], dtb_ref.at[1:2], e_ref, yr_ref, state_ref.at[1],
               reverse=True)


def _ssd_chunk(xs_ref, b_ref, c_ref, dt_ref, a_ref, dtb_ref, e_ref, y_ref, state_ref, *, reverse):
    q = SSM_CHUNK
    hpg = SSM_HEADS // SSM_GROUPS
    c0 = SSM_HEADS if reverse else 0
    pre = dt_ref[:, c0:c0 + SSM_HEADS] + dtb_ref[...]
    dt = jnp.maximum(pre, 0.0) + jnp.log1p(jnp.exp(-jnp.abs(pre)))
    la = dt * a_ref[...]
    ri = lax.broadcasted_iota(I32, (q, q), 0)
    ci = lax.broadcasted_iota(I32, (q, q), 1)
    tri = (ri <= ci) if reverse else (ri >= ci)
    trib = tri.astype(BF16)
    hi, mid, lo = _split3(la)
    cum = (jnp.dot(trib, hi, preferred_element_type=F32) + jnp.dot(trib, mid, preferred_element_type=F32)
           + jnp.dot(trib, lo, preferred_element_type=F32))
    total = cum[0:1, :] if reverse else cum[q - 1:q, :]
    cum_t = cum.T
    dt_t = dt.T
    lane = lax.broadcasted_iota(I32, (1, LANES), 1)
    first_half = lane < SSM_HEAD_DIM

    def per_head_to_columns(v, with_lo=True):
        hi = v.astype(BF16)
        out = jnp.dot(hi, e_ref[...], preferred_element_type=F32)
        if with_lo:
            lo = (v - hi.astype(F32)).astype(BF16)
            out = out + jnp.dot(lo, e_ref[...], preferred_element_type=F32)
        return out

    exp_cum_cols = per_head_to_columns(jnp.exp(cum))
    w_state_cols = per_head_to_columns(jnp.exp(total - cum) * dt, with_lo=False)
    exp_total_cols = per_head_to_columns(jnp.broadcast_to(jnp.exp(total), (8, SSM_HEADS)))[0:1, :]

    for g in range(SSM_GROUPS):
        bg = b_ref[:, g * SSM_STATE:(g + 1) * SSM_STATE]
        cg = c_ref[:, g * SSM_STATE:(g + 1) * SSM_STATE]
        cb = lax.dot_general(cg, bg, (((1,), (1,)), ((), ())), preferred_element_type=F32)
        st = state_ref[g]
        y_state = jnp.dot(cg, st.astype(BF16), preferred_element_type=F32)
        xw_parts = []
        for pair in range(hpg // 2):
            h0 = g * hpg + 2 * pair
            col = g * SSM_GROUP_W + pair * LANES
            xpair = xs_ref[:, col:col + LANES]
            ws = []
            for h in (h0, h0 + 1):
                seg = cum[:, h:h + 1] - cum_t[h:h + 1, :]
                dec = jnp.exp(jnp.where(tri, seg, NEG))
                ws.append((cb * dec * dt_t[h:h + 1, :]).astype(BF16))
            zero = jnp.zeros_like(xpair)
            x_diag = jnp.concatenate([jnp.where(first_half, xpair, zero), jnp.where(first_half, zero, xpair)],
                                     axis=0)
            y_intra = jnp.dot(jnp.concatenate(ws, axis=1), x_diag, preferred_element_type=F32)
            ysp = y_state[:, pair * LANES:(pair + 1) * LANES]
            y_ref[:, col:col + LANES] = (y_intra + ysp * exp_cum_cols[:, col:col + LANES]).astype(y_ref.dtype)
            xw_parts.append((xpair.astype(F32) * w_state_cols[:, col:col + LANES]).astype(BF16))
        xw = jnp.concatenate(xw_parts, axis=1)
        scale_row = exp_total_cols[:, g * SSM_GROUP_W:(g + 1) * SSM_GROUP_W]
        upd = lax.dot_general(bg, xw, (((0,), (0,)), ((), ())), preferred_element_type=F32)
        state_ref[g] = st * scale_row + upd


def _ssd_scan(xbc, dt_raw, a, dt_bias, geom):
    t = xbc.shape[0]
    q = SSM_CHUNK
    bsz = geom["b"]
    cq, lq = geom["c"] // q, geom["s"] // q
    lat_blocks = bsz * lq

    def row_block(reverse):
        def f(b, s):
            cchunk = (cq - 1 - s) if reverse else s
            lchunk = (lq - 1 - (s - cq)) if reverse else (s - cq)
            return jnp.where(s < cq, lat_blocks + b * cq + cchunk, b * lq + lchunk)
        return f

    bc_w = SSM_GROUPS * SSM_STATE
    bcol = SSM_D_INNER // bc_w
    head_of_col = jnp.arange(SSM_D_INNER, dtype=I32) // SSM_HEAD_DIM
    expand = (head_of_col[None, :] == jnp.arange(SSM_HEADS, dtype=I32)[:, None]).astype(BF16)

    def chunk_specs(reverse):
        rb = row_block(reverse)
        return [
            pl.BlockSpec((q, SSM_D_INNER), lambda b, s: (rb(b, s), 0)),
            pl.BlockSpec((q, bc_w), lambda b, s: (rb(b, s), bcol)),
            pl.BlockSpec((q, bc_w), lambda b, s: (rb(b, s), bcol + 1)),
            pl.BlockSpec((q, 2 * SSM_HEADS), lambda b, s: (rb(b, s), 0)),
        ]

    const = lambda b, s: (0, 0)
    return pl.pallas_call(
        _ssd_scan_kernel,
        grid=(bsz, cq + lq),
        in_specs=chunk_specs(False) + chunk_specs(True) + [
            pl.BlockSpec((2, SSM_HEADS), const),
            pl.BlockSpec((2, SSM_HEADS), const),
            pl.BlockSpec((SSM_HEADS, SSM_D_INNER), const),
        ],
        out_specs=[pl.BlockSpec((q, SSM_D_INNER), lambda b, s: (row_block(False)(b, s), 0)),
                   pl.BlockSpec((q, SSM_D_INNER), lambda b, s: (row_block(True)(b, s), 0))],
        out_shape=[jax.ShapeDtypeStruct((t, SSM_D_INNER), BF16)] * 2,
        scratch_shapes=[pltpu.VMEM((2, SSM_GROUPS, SSM_STATE, SSM_GROUP_W), F32)],
        compiler_params=_cparams("parallel", "arbitrary"),
        name="ssd_scan",
    )(xbc, xbc, xbc, dt_raw, xbc, xbc, xbc, dt_raw, a, dt_bias, expand)


def _qk_prep_kernel(x_ref, g_ref, s_ref, cos_ref, sin_ref, o_ref, n_ref, *, heads):
    j = pl.program_id(1)
    cosv = cos_ref[...]
    sinv = sin_ref[...]
    lane = lax.broadcasted_iota(I32, (1, HEAD_DIM), 1)
    low = (lane % (HEAD_DIM // 2)) < (HEAD_DIM // 4)

    @pl.when(j == 0)
    def _():
        n_ref[...] = jnp.zeros_like(n_ref)

    sq_norms = n_ref[...]
    for h in range(heads):
        sl = slice(h * HEAD_DIM, (h + 1) * HEAD_DIM)
        x = x_ref[:, sl].astype(F32)
        ms = jnp.mean(x * x, axis=-1, keepdims=True)
        y = x * lax.rsqrt(ms + NORM_EPS) * g_ref[:, sl]
        partner = jnp.where(low, pltpu.roll(y, HEAD_DIM - HEAD_DIM // 4, 1), pltpu.roll(y, HEAD_DIM // 4, 1))
        ob = ((y * cosv + partner * sinv) * s_ref[:, sl]).astype(o_ref.dtype)
        o_ref[:, sl] = ob
        of = ob.astype(F32)
        sq_norms = jnp.where(lane == j * heads + h, jnp.sum(of * of, axis=-1, keepdims=True), sq_norms)
    n_ref[...] = sq_norms


def _qk_prep(qkv, gains, post_scale, cos_t, sin_t):
    t = qkv.shape[0]
    n = gains.shape[1]
    rows, wb = 512, 512
    return pl.pallas_call(
        functools.partial(_qk_prep_kernel, heads=wb // HEAD_DIM),
        grid=(t // rows, n // wb),
        in_specs=[
            pl.BlockSpec((rows, wb), lambda i, j: (i, j)),
            pl.BlockSpec((1, wb), lambda i, j: (0, j)),
            pl.BlockSpec((1, wb), lambda i, j: (0, j)),
            pl.BlockSpec((rows, HEAD_DIM), lambda i, j: (i, 0)),
            pl.BlockSpec((rows, HEAD_DIM), lambda i, j: (i, 0)),
        ],
        out_specs=[pl.BlockSpec((rows, wb), lambda i, j: (i, j)),
                   pl.BlockSpec((rows, HEAD_DIM), lambda i, j: (i, 0))],
        out_shape=[jax.ShapeDtypeStruct((t, n), BF16), jax.ShapeDtypeStruct((t, HEAD_DIM), F32)],
        compiler_params=_cparams("parallel", "arbitrary"),
        name="qk_norm_rope",
    )(qkv, gains, post_scale, cos_t, sin_t)


def _rope_tables(geom):
    s, t = geom["s"], geom["t"]
    pos = jnp.arange(s)
    row = (pos // GRID_W).astype(F32)
    col = (pos % GRID_W).astype(F32)
    quarter = HEAD_DIM // 4
    inv_freq = ROPE_THETA ** (-jnp.arange(quarter, dtype=F32) / quarter)
    ar = row[:, None] * inv_freq
    ac = col[:, None] * inv_freq
    cos_l = jnp.concatenate([jnp.cos(ar), jnp.cos(ar), jnp.cos(ac), jnp.cos(ac)], axis=1)
    sin_l = jnp.concatenate([-jnp.sin(ar), jnp.sin(ar), -jnp.sin(ac), jnp.sin(ac)], axis=1)
    n_ctx_rows = t - geom["n_lat"]
    cos_t = jnp.concatenate([jnp.tile(cos_l, (geom["b"], 1)), jnp.ones((n_ctx_rows, HEAD_DIM), F32)], axis=0)
    sin_t = jnp.concatenate([jnp.tile(sin_l, (geom["b"], 1)), jnp.zeros((n_ctx_rows, HEAD_DIM), F32)], axis=0)
    return cos_t, sin_t


def _win_attn_kernel(sink_ref, q_ref, kp_ref, kc_ref, kn_ref, vp_ref, vc_ref, vn_ref, kx_ref, vx_ref, o_ref,
                     *, n_lat_blocks, ctx_blocks):
    s = pl.program_id(1)
    grp = WIN_HEADS // WIN_KV_HEADS
    blk = ATT_BLOCK
    is_lat = s >= ctx_blocks
    n = s - ctx_blocks
    rows = grp * blk
    qi = lax.broadcasted_iota(I32, (rows, 3 * blk), 0) % blk
    kk = lax.broadcasted_iota(I32, (rows, 3 * blk), 1)
    rel = kk - blk - qi
    in_band = (rel <= WINDOW) & (rel >= -WINDOW)
    lo = jnp.where(n > 0, 0, blk)
    hi = jnp.where(is_lat, jnp.where(n < n_lat_blocks - 1, 3 * blk, 2 * blk), 0)
    mask = in_band & (kk >= lo) & (kk < hi)
    rowi = lax.broadcasted_iota(I32, (rows, 1), 0)
    nt = (((1,), (1,)), ((), ()))
    for kh in range(WIN_KV_HEADS):
        ks = slice(kh * HEAD_DIM, (kh + 1) * HEAD_DIM)
        k_lat = jnp.concatenate([kp_ref[:, ks], kc_ref[:, ks], kn_ref[:, ks]], axis=0)
        v_lat = jnp.concatenate([vp_ref[:, ks], vc_ref[:, ks], vn_ref[:, ks]], axis=0)
        qg = jnp.concatenate(
            [q_ref[:, (kh * grp + j) * HEAD_DIM:(kh * grp + j + 1) * HEAD_DIM] for j in range(grp)], axis=0)
        s_lat = jnp.where(mask, lax.dot_general(qg, k_lat, nt, preferred_element_type=F32), NEG)
        s_ctx = lax.dot_general(qg, kx_ref[:, ks], nt, preferred_element_type=F32)
        sink = jnp.full((rows, 1), sink_ref[kh * grp], F32)
        for j in range(1, grp):
            sink = jnp.where(rowi >= j * blk, sink_ref[kh * grp + j], sink)
        m = jnp.maximum(jnp.maximum(jnp.max(s_lat, axis=-1, keepdims=True),
                                    jnp.max(s_ctx, axis=-1, keepdims=True)), sink)
        p_lat = jnp.exp(s_lat - m)
        p_ctx = jnp.exp(s_ctx - m)
        denom = (jnp.sum(p_lat, axis=-1, keepdims=True) + jnp.sum(p_ctx, axis=-1, keepdims=True)
                 + jnp.exp(sink - m))
        o = (jnp.dot(p_lat.astype(BF16), v_lat, preferred_element_type=F32)
             + jnp.dot(p_ctx.astype(BF16), vx_ref[:, ks], preferred_element_type=F32)) / denom
        for j in range(grp):
            hq = kh * grp + j
            o_ref[:, hq * HEAD_DIM:(hq + 1) * HEAD_DIM] = o[j * blk:(j + 1) * blk].astype(o_ref.dtype)


def _win_attn(qk, qkv, sinks, geom):
    t = qk.shape[0]
    blk = ATT_BLOCK
    bsz, c = geom["b"], geom["c"]
    nb = geom["s"] // blk
    cb = c // blk
    lat_blocks = bsz * nb
    kvw = WIN_KV_HEADS * HEAD_DIM
    qw = WIN_HEADS * HEAD_DIM
    kcol = qw // kvw
    vcol = (qw + kvw) // kvw

    def qrow(b, s, sk):
        return jnp.where(s < cb, lat_blocks + b * cb + s, b * nb + (s - cb))

    def lat(off):
        def f(b, s, sk):
            n = jnp.clip(s - cb + off, 0, nb - 1)
            return b * nb + n
        return f

    ctx_row = lambda b, s, sk: (geom["n_lat"] // c + b)
    grid_spec = pltpu.PrefetchScalarGridSpec(
        num_scalar_prefetch=1,
        grid=(bsz, cb + nb),
        in_specs=[
            pl.BlockSpec((blk, qw), lambda b, s, sk: (qrow(b, s, sk), 0)),
            pl.BlockSpec((blk, kvw), lambda b, s, sk: (lat(-1)(b, s, sk), kcol)),
            pl.BlockSpec((blk, kvw), lambda b, s, sk: (lat(0)(b, s, sk), kcol)),
            pl.BlockSpec((blk, kvw), lambda b, s, sk: (lat(1)(b, s, sk), kcol)),
            pl.BlockSpec((blk, kvw), lambda b, s, sk: (lat(-1)(b, s, sk), vcol)),
            pl.BlockSpec((blk, kvw), lambda b, s, sk: (lat(0)(b, s, sk), vcol)),
            pl.BlockSpec((blk, kvw), lambda b, s, sk: (lat(1)(b, s, sk), vcol)),
            pl.BlockSpec((c, kvw), lambda b, s, sk: (ctx_row(b, s, sk), kcol)),
            pl.BlockSpec((c, kvw), lambda b, s, sk: (ctx_row(b, s, sk), vcol)),
        ],
        out_specs=pl.BlockSpec((blk, qw), lambda b, s, sk: (qrow(b, s, sk), 0)),
    )
    return pl.pallas_call(
        functools.partial(_win_attn_kernel, n_lat_blocks=nb, ctx_blocks=cb),
        grid_spec=grid_spec,
        out_shape=jax.ShapeDtypeStruct((t, qw), BF16),
        compiler_params=_cparams("parallel", "parallel"),
        name="window_gqa",
    )(sinks, qk, qk, qk, qk, qkv, qkv, qkv, qk, qkv)


DIFF_SAFE_EXCESS = 96.0


def _diff_attn_kernel(*refs, use_latent, lam_init, sub):
    if use_latent:
        (kmax_ref, q_ref, kx_ref, vx_ref, k_ref, v_ref, qn_ref, lam_ref, g_ref, o_ref,
         m_ref, l_ref, acc_ref, stat_ref) = refs
    else:
        q_ref, kx_ref, vx_ref, lam_ref, g_ref, _, o_ref, m_ref, l_ref, acc_ref = refs
    j = pl.program_id(3)
    nt = (((1,), (1,)), ((), ()))
    hd = HEAD_DIM

    def update(idx, qh, kh, v):
        s = lax.dot_general(qh, kh, nt, preferred_element_type=F32)
        m_old = m_ref[idx]
        m_new = jnp.maximum(m_old, jnp.max(s, axis=-1, keepdims=True))
        alpha = jnp.exp2(m_old - m_new)
        p = jnp.exp2(s - m_new)
        l_ref[idx] = alpha * l_ref[idx] + jnp.sum(p, axis=-1, keepdims=True)
        acc_ref[idx] = alpha * acc_ref[idx] + jnp.dot(p.astype(BF16), v, preferred_element_type=F32)
        m_ref[idx] = m_new

    @pl.when(j == 0)
    def _():
        m_ref[...] = jnp.full_like(m_ref, NEG)
        l_ref[...] = jnp.zeros_like(l_ref)
        acc_ref[...] = jnp.zeros_like(acc_ref)
        for idx in range(2):
            update(idx, q_ref[:, idx * hd:(idx + 1) * hd], kx_ref[:, idx * hd:(idx + 1) * hd], vx_ref[...])
        if use_latent:
            lane = lax.broadcasted_iota(I32, qn_ref.shape, 1)
            for idx in range(2):
                head = 2 * pl.program_id(1) + idx
                q_sq = jnp.sum(jnp.where(lane == head, qn_ref[...], 0.0), axis=-1, keepdims=True)
                stat_ref[idx] = jnp.sqrt(jnp.max(q_sq))
                stat_ref[2 + idx] = jnp.min(m_ref[idx])

    if use_latent:
        b, h = pl.program_id(0), pl.program_id(1)
        excess = []
        for idx in range(2):
            k_norm = kmax_ref[(b * pl.num_programs(3) + j) * (2 * DIFF_HEADS) + 2 * h + idx]
            excess.append(stat_ref[idx] * k_norm - stat_ref[2 + idx])
        safe = jnp.maximum(excess[0], excess[1]) <= DIFF_SAFE_EXCESS

        @pl.when(safe)
        def _():
            for idx in range(2):
                qh = q_ref[:, idx * hd:(idx + 1) * hd]
                shift = m_ref[idx]
                ps = []
                lsum = jnp.zeros_like(shift)
                for c in range(k_ref.shape[0] // sub):
                    rs = slice(c * sub, (c + 1) * sub)
                    s = lax.dot_general(qh, k_ref[rs, idx * hd:(idx + 1) * hd], nt, preferred_element_type=F32)
                    p = jnp.exp2(s - shift)
                    lsum = lsum + jnp.sum(p, axis=-1, keepdims=True)
                    ps.append(p.astype(BF16))
                l_ref[idx] += lsum
                acc_ref[idx] += jnp.dot(jnp.concatenate(ps, axis=1), v_ref[...], preferred_element_type=F32)

        @pl.when(jnp.logical_not(safe))
        def _():
            for idx in range(2):
                update(idx, q_ref[:, idx * hd:(idx + 1) * hd], k_ref[:, idx * hd:(idx + 1) * hd], v_ref[...])
                stat_ref[2 + idx] = jnp.min(m_ref[idx])

    @pl.when(j == pl.num_programs(3) - 1)
    def _():
        lam = lam_ref[...]
        lam_full = (jnp.exp(jnp.sum(lam[0:1] * lam[1:2], axis=-1, keepdims=True))
                    - jnp.exp(jnp.sum(lam[2:3] * lam[3:4], axis=-1, keepdims=True)) + lam_init)
        o = acc_ref[0] / l_ref[0] - lam_full * (acc_ref[1] / l_ref[1])
        ms = jnp.mean(o * o, axis=-1, keepdims=True)
        o_ref[...] = (o * lax.rsqrt(ms + NORM_EPS) * g_ref[...] * (1.0 - lam_init)).astype(o_ref.dtype)


def _diff_attn(qk, qkv, norms, lam, subln_g, lam_init, geom):
    t = qk.shape[0]
    bsz, s, c = geom["b"], geom["s"], geom["c"]
    pw = 2 * HEAD_DIM
    tq = min(1024, s)
    tkb = min(2048, s)
    sub = min(256, tkb)
    nq, nk = s // tq, s // tkb
    ctx_row = geom["n_lat"] // c
    kcol0, vcol0 = DIFF_HEADS, 2 * DIFF_HEADS
    g2 = subln_g.reshape(1, pw)
    scratch = lambda rows: [pltpu.VMEM((2, rows, 1), F32), pltpu.VMEM((2, rows, 1), F32),
                            pltpu.VMEM((2, rows, pw), F32)]
    nh = 2 * DIFF_HEADS
    k_sq = norms[:geom["n_lat"], nh:2 * nh].reshape(bsz, nk, tkb, nh)
    kmax = (jnp.sqrt(jnp.max(k_sq, axis=2)) * 1.001).reshape(-1)
    grid_spec = pltpu.PrefetchScalarGridSpec(
        num_scalar_prefetch=1,
        grid=(bsz, DIFF_HEADS, nq, nk),
        in_specs=[
            pl.BlockSpec((tq, pw), lambda b, h, i, j, km: (b * nq + i, h)),
            pl.BlockSpec((c, pw), lambda b, h, i, j, km: (ctx_row + b, kcol0 + h)),
            pl.BlockSpec((c, pw), lambda b, h, i, j, km: (ctx_row + b, vcol0 + h)),
            pl.BlockSpec((tkb, pw), lambda b, h, i, j, km: (b * nk + j, kcol0 + h)),
            pl.BlockSpec((tkb, pw), lambda b, h, i, j, km: (b * nk + j, vcol0 + h)),
            pl.BlockSpec((tq, LANES), lambda b, h, i, j, km: (b * nq + i, 0)),
            pl.BlockSpec((4, HEAD_DIM), lambda b, h, i, j, km: (0, 0)),
            pl.BlockSpec((1, pw), lambda b, h, i, j, km: (0, 0)),
        ],
        out_specs=pl.BlockSpec((tq, pw), lambda b, h, i, j, km: (b * nq + i, h)),
        scratch_shapes=scratch(tq) + [pltpu.SMEM((4,), F32)],
    )
    o_lat = pl.pallas_call(
        functools.partial(_diff_attn_kernel, use_latent=True, lam_init=lam_init, sub=sub),
        grid_spec=grid_spec,
        out_shape=jax.ShapeDtypeStruct((t, D_MODEL), BF16),
        compiler_params=_cparams("parallel", "parallel", "parallel", "arbitrary"),
        name="diff_attn_latent",
    )(kmax, qk, qk, qkv, qk, qkv, norms, lam, g2)
    return pl.pallas_call(
        functools.partial(_diff_attn_kernel, use_latent=False, lam_init=lam_init, sub=sub),
        grid=(bsz, DIFF_HEADS, 1, 1),
        in_specs=[
            pl.BlockSpec((c, pw), lambda b, h, i, j: (ctx_row + b, h)),
            pl.BlockSpec((c, pw), lambda b, h, i, j: (ctx_row + b, kcol0 + h)),
            pl.BlockSpec((c, pw), lambda b, h, i, j: (ctx_row + b, vcol0 + h)),
            pl.BlockSpec((4, HEAD_DIM), lambda b, h, i, j: (0, 0)),
            pl.BlockSpec((1, pw), lambda b, h, i, j: (0, 0)),
            pl.BlockSpec(memory_space=pl.ANY),
        ],
        out_specs=pl.BlockSpec((c, pw), lambda b, h, i, j: (ctx_row + b, h)),
        out_shape=jax.ShapeDtypeStruct((t, D_MODEL), BF16),
        scratch_shapes=scratch(c),
        input_output_aliases={5: 0},
        compiler_params=_cparams("parallel", "parallel", "parallel", "arbitrary"),
        name="diff_attn_context",
    )(qk, qk, qkv, lam, g2, o_lat)


def _outproj_kernel(*refs, ssd):
    if ssd:
        (xs_ref, yf_ref, yb_ref, z_ref, dsk_ref, ng_ref, w_ref, x_ref, mod_ref, g_ref, rw_ref, rb_ref,
         xo_ref, hp_ref, lg_ref, acc_ref, ssq_ref) = refs
    else:
        (a_ref, w_ref, x_ref, mod_ref, g_ref, rw_ref, rb_ref,
         xo_ref, hp_ref, lg_ref, acc_ref) = refs
    k = pl.program_id(1)

    @pl.when(k == 0)
    def _():
        acc_ref[...] = jnp.zeros_like(acc_ref)
        if ssd:
            ssq_ref[...] = jnp.zeros_like(ssq_ref)

    if ssd:
        y = dsk_ref[...] * xs_ref[...].astype(F32) + yf_ref[...].astype(F32) + yb_ref[...].astype(F32)
        u = y * _silu(z_ref[...].astype(F32))
        ssq_ref[...] += jnp.sum(u * u, axis=-1, keepdims=True)
        a = (u * ng_ref[...]).astype(BF16)
    else:
        a = a_ref[...]
    acc_ref[...] += jnp.dot(a, w_ref[...], preferred_element_type=F32)

    @pl.when(k == pl.num_programs(1) - 1)
    def _():
        y = acc_ref[...]
        if ssd:
            y = y * lax.rsqrt(ssq_ref[...] * (1.0 / SSM_D_INNER) + NORM_EPS)
        xn = x_ref[...] + mod_ref[2:3, :] * y
        xo_ref[...] = xn
        ms = jnp.mean(xn * xn, axis=-1, keepdims=True)
        h = xn * lax.rsqrt(ms + NORM_EPS) * g_ref[...]
        h = h * (1.0 + mod_ref[4:5, :]) + mod_ref[3:4, :]
        hb = h.astype(BF16)
        hl = (h - hb.astype(F32)).astype(BF16)
        r2 = (jnp.dot(hb, rw_ref[...], preferred_element_type=F32)
              + jnp.dot(hl, rw_ref[...], preferred_element_type=F32))
        lg_ref[...] = r2[:, :LANES] + r2[:, LANES:] + rb_ref[...]
        half = D_MODEL // 2
        lo_bits = lax.bitcast_convert_type(hb[:, :half].astype(F32), U32) >> 16
        hi_bits = lax.bitcast_convert_type(hb[:, half:].astype(F32), U32)
        _tile_store(hp_ref, hi_bits | lo_bits, TOKEN_SUBROWS)


def _outproj(a_args, w, xa, modv, gain, r_w2, r_bias, geom, ssd):
    t, d = xa.shape
    kdim = w.shape[0]
    tm, tk = ROW_TILE, 1024
    tpb, nlt = geom["s"] // tm, geom["n_lat"] // tm
    var = lambda i, k: (_variant_of_tile(i, tpb, nlt), 0, 0)
    const = lambda i, k: (0, 0)
    if ssd:
        xbc, yf, yb, zx, dsk, ng = a_args
        zcol = 0
        a_specs = [
            pl.BlockSpec((tm, tk), lambda i, k: (i, k)),
            pl.BlockSpec((tm, tk), lambda i, k: (i, k)),
            pl.BlockSpec((tm, tk), lambda i, k: (i, k)),
            pl.BlockSpec((tm, tk), lambda i, k: (i, zcol + k)),
            pl.BlockSpec((1, tk), lambda i, k: (0, k)),
            pl.BlockSpec((1, tk), lambda i, k: (0, k)),
        ]
        a_in = [xbc, yf, yb, zx, dsk, ng]
        scratch = [pltpu.VMEM((tm, d), F32), pltpu.VMEM((tm, 1), F32)]
    else:
        a_specs = [pl.BlockSpec((tm, tk), lambda i, k: (i, k))]
        a_in = list(a_args)
        scratch = [pltpu.VMEM((tm, d), F32)]
    return pl.pallas_call(
        functools.partial(_outproj_kernel, ssd=ssd),
        grid=(t // tm, kdim // tk),
        in_specs=a_specs + [
            pl.BlockSpec((tk, d), lambda i, k: (k, 0)),
            pl.BlockSpec((tm, d), lambda i, k: (i, 0)),
            pl.BlockSpec((None, 8, d), var),
            pl.BlockSpec((1, d), const),
            pl.BlockSpec((d, 2 * LANES), const),
            pl.BlockSpec((1, LANES), const),
        ],
        out_specs=[
            pl.BlockSpec((tm, d), lambda i, k: (i, 0)),
            pl.BlockSpec((tm * TOKEN_SUBROWS, LANES), lambda i, k: (i, 0)),
            pl.BlockSpec((tm, LANES), lambda i, k: (i, 0)),
        ],
        out_shape=[
            jax.ShapeDtypeStruct((t, d), F32),
            jax.ShapeDtypeStruct((t * TOKEN_SUBROWS, LANES), U32),
            jax.ShapeDtypeStruct((t, LANES), F32),
        ],
        scratch_shapes=scratch,
        compiler_params=_cparams("parallel", "arbitrary"),
        name="outproj_residual_norm",
    )(*a_in, w, xa, modv, gain.reshape(1, d), r_w2, r_bias)


ROUTE_E0, ROUTE_E1, ROUTE_G0, ROUTE_G1, ROUTE_R0, ROUTE_R1 = range(6)


def _route_kernel(lg_ref, o_ref, cnt_ref, carry_ref):
    rows = lg_ref.shape[0]

    @pl.when(pl.program_id(0) == 0)
    def _():
        carry_ref[...] = jnp.zeros_like(carry_ref)

    lg = lg_ref[...]
    lane = lax.broadcasted_iota(I32, (rows, LANES), 1)
    big = jnp.int32(LANES)

    def first_argmax(v, vmax):
        return jnp.min(jnp.where(v == vmax, lane, big), axis=-1, keepdims=True)

    gl = jnp.where(lane < MOE_GROUPS, lg, NEG)
    gmax = jnp.max(gl, axis=-1, keepdims=True)
    gsum = jnp.sum(jnp.exp(gl - gmax), axis=-1, keepdims=True)
    g_sel = first_argmax(gl, gmax)
    g_p = 1.0 / gsum
    e_lo = MOE_GROUPS + g_sel * MOE_EPG
    el = jnp.where((lane >= e_lo) & (lane < e_lo + MOE_EPG), lg, NEG)
    emax = jnp.max(el, axis=-1, keepdims=True)
    esum = jnp.sum(jnp.exp(el - emax), axis=-1, keepdims=True)
    l0 = first_argmax(el, emax)
    el2 = jnp.where(lane == l0, NEG, el)
    emax2 = jnp.max(el2, axis=-1, keepdims=True)
    l1 = first_argmax(el2, emax2)
    p0 = 1.0 / esum
    p1 = jnp.exp(emax2 - emax) / esum
    gate0 = g_p * p0 / (p0 + p1)
    gate1 = g_p * p1 / (p0 + p1)
    e0 = l0 - MOE_GROUPS
    e1 = l1 - MOE_GROUPS
    oh0 = lane == e0
    oh1 = lane == e1
    hits = oh0.astype(F32) + oh1.astype(F32)
    ri = lax.broadcasted_iota(I32, (rows, rows), 0)
    ci = lax.broadcasted_iota(I32, (rows, rows), 1)
    before = (ci < ri).astype(BF16)
    prior = jnp.dot(before, hits.astype(BF16), preferred_element_type=F32) + carry_ref[0:1, :]
    r0 = jnp.sum(jnp.where(oh0, prior, 0.0), axis=-1, keepdims=True)
    r1 = jnp.sum(jnp.where(oh1, prior, 0.0), axis=-1, keepdims=True)
    carry = carry_ref[0:1, :] + jnp.sum(hits, axis=0, keepdims=True)
    carry_ref[0:1, :] = carry
    cnt_ref[...] = jnp.broadcast_to(carry, cnt_ref.shape)
    rec = jnp.where(lane == ROUTE_E0, e0.astype(F32), 0.0)
    rec = jnp.where(lane == ROUTE_E1, e1.astype(F32), rec)
    rec = jnp.where(lane == ROUTE_G0, gate0, rec)
    rec = jnp.where(lane == ROUTE_G1, gate1, rec)
    rec = jnp.where(lane == ROUTE_R0, r0, rec)
    rec = jnp.where(lane == ROUTE_R1, r1, rec)
    o_ref[...] = rec


def _route(logits):
    t = logits.shape[0]
    rows = 512
    return pl.pallas_call(
        _route_kernel,
        grid=(t // rows,),
        in_specs=[pl.BlockSpec((rows, LANES), lambda i: (i, 0))],
        out_specs=[pl.BlockSpec((rows, LANES), lambda i: (i, 0)), pl.BlockSpec((8, LANES), lambda i: (0, 0))],
        out_shape=[jax.ShapeDtypeStruct((t, LANES), F32), jax.ShapeDtypeStruct((8, LANES), F32)],
        scratch_shapes=[pltpu.VMEM((8, LANES), F32)],
        compiler_params=_cparams("arbitrary"),
        name="moe_route",
    )(logits)


def _gather_tiles(idx_smem, islot, src_hbm, dst_ref, sem, n):
    sub = dst_ref.shape[0] // n
    for r in range(n):
        off = pl.multiple_of(idx_smem[islot, r], sub)
        pltpu.make_async_copy(src_hbm.at[pl.ds(off, sub)], dst_ref.at[pl.ds(r * sub, sub)], sem).start()


def _whole_buffer_copy(src_hbm, buf, sem, n):
    del n
    return pltpu.make_async_copy(src_hbm.at[pl.ds(0, buf.shape[0])], buf, sem)


def _gather_step(step, nsteps, idx_hbm, idx_smem, idx_sem, src_hbm, bufs, row_sems, n):
    def idx_copy(s, slot):
        return pltpu.make_async_copy(idx_hbm.at[s], idx_smem.at[slot], idx_sem.at[slot])

    last = nsteps - 1

    @pl.when(step == 0)
    def _():
        idx_copy(0, 0).start()
        idx_copy(0, 0).wait()
        _gather_tiles(idx_smem, 0, src_hbm, bufs.at[0], row_sems.at[0], n)
        idx_copy(jnp.minimum(1, last), 1).start()

    nxt_i, nxt_b = (step + 1) % 3, (step + 1) % 2
    idx_copy(jnp.minimum(step + 1, last), nxt_i).wait()
    _gather_tiles(idx_smem, nxt_i, src_hbm, bufs.at[nxt_b], row_sems.at[nxt_b], n)
    idx_copy(jnp.minimum(step + 2, last), (step + 2) % 3).start()
    _whole_buffer_copy(src_hbm, bufs.at[step % 2], row_sems.at[step % 2], n).wait()


def _gather_drain(step, nsteps, idx_hbm, idx_smem, idx_sem, src_hbm, bufs, row_sems, n):
    @pl.when(step == nsteps - 1)
    def _():
        nxt_b, nxt_i = (step + 1) % 2, (step + 2) % 3
        _whole_buffer_copy(src_hbm, bufs.at[nxt_b], row_sems.at[nxt_b], n).wait()
        pltpu.make_async_copy(idx_hbm.at[0], idx_smem.at[nxt_i], idx_sem.at[nxt_i]).wait()


def _untile(buf, slot, first, n, sub):
    return jnp.concatenate([buf[slot, pl.ds(first * sub + c, n, stride=sub), :] for c in range(sub)], axis=1)


def _tile_store(ref, val, sub):
    n = val.shape[0]
    for c in range(sub):
        ref[pl.ds(c, n, stride=sub), :] = val[:, c * LANES:(c + 1) * LANES]


def _expert_kernel(bexp_ref, nused_ref, idx_hbm, tok_hbm, wg32_ref, wu32_ref, wd32_ref, y_ref,
                   idx_smem, xbuf, idx_sem, row_sems, wg_ref, wu_ref, wd_ref):
    b = pl.program_id(0)
    nused = nused_ref[0]

    @pl.when((b < nused) & ((b == 0) | (bexp_ref[b] != bexp_ref[jnp.maximum(b - 1, 0)])))
    def _():
        wg_ref[...] = wg32_ref[...].astype(BF16)
        wu_ref[...] = wu32_ref[...].astype(BF16)
        wd_ref[...] = wd32_ref[...].astype(BF16)

    @pl.when(b < nused)
    def _():
        rows = MOE_ROWS
        gather_args = (b, nused, idx_hbm, idx_smem, idx_sem, tok_hbm, xbuf, row_sems, rows)
        _gather_step(*gather_args)
        w = _untile(xbuf, b % 2, 0, rows, TOKEN_SUBROWS)
        x_lo = lax.bitcast_convert_type(w << 16, F32).astype(BF16)
        x_hi = lax.bitcast_convert_type(w & jnp.uint32(0xFFFF0000), F32).astype(BF16)
        half = D_MODEL // 2
        gate = (jnp.dot(x_lo, wg_ref[:half, :], preferred_element_type=F32)
                + jnp.dot(x_hi, wg_ref[half:, :], preferred_element_type=F32))
        up = (jnp.dot(x_lo, wu_ref[:half, :], preferred_element_type=F32)
              + jnp.dot(x_hi, wu_ref[half:, :], preferred_element_type=F32))
        hdn = (_silu(gate) * up).astype(BF16)
        y = jnp.dot(hdn, wd_ref[...], preferred_element_type=F32)
        y_ref[...] = y
        _gather_drain(*gather_args)

    @pl.when(b >= nused)
    def _():
        y_ref[...] = jnp.zeros_like(y_ref)


def _experts(block_exp, n_used, slot_tok, tokens_packed, wg, wu, wd, layer):
    n_blocks = slot_tok.shape[0]
    rows = MOE_ROWS
    half = D_MODEL // 2
    sub = TOKEN_SUBROWS
    grid_spec = pltpu.PrefetchScalarGridSpec(
        num_scalar_prefetch=2,
        grid=(n_blocks,),
        in_specs=[
            pl.BlockSpec(memory_space=pl.ANY),
            pl.BlockSpec(memory_space=pl.ANY),
            pl.BlockSpec((None, None, D_MODEL, MOE_D_FF), lambda b, be, nu: (layer, be[b], 0, 0)),
            pl.BlockSpec((None, None, D_MODEL, MOE_D_FF), lambda b, be, nu: (layer, be[b], 0, 0)),
            pl.BlockSpec((None, None, MOE_D_FF, D_MODEL), lambda b, be, nu: (layer, be[b], 0, 0)),
        ],
        out_specs=pl.BlockSpec((rows, D_MODEL), lambda b, be, nu: (b, 0)),
        scratch_shapes=[
            pltpu.SMEM((3, rows), I32),
            pltpu.VMEM((2, rows * sub, half // sub), U32),
            pltpu.SemaphoreType.DMA((3,)),
            pltpu.SemaphoreType.DMA((2,)),
            pltpu.VMEM((D_MODEL, MOE_D_FF), BF16),
            pltpu.VMEM((D_MODEL, MOE_D_FF), BF16),
            pltpu.VMEM((MOE_D_FF, D_MODEL), BF16),
        ],
    )
    return pl.pallas_call(
        _expert_kernel,
        grid_spec=grid_spec,
        out_shape=jax.ShapeDtypeStruct((n_blocks * rows, D_MODEL), F32),
        compiler_params=_cparams("arbitrary"),
        name="moe_experts",
    )(block_exp, n_used, slot_tok, tokens_packed, wg, wu, wd)


def _combine_kernel(idx_hbm, ys_hbm, x_ref, mod_ref, rt_ref, o_ref, idx_smem, ybuf, idx_sem, row_sems):
    i = pl.program_id(0)
    n = COMBINE_ROWS
    gather_args = (i, pl.num_programs(0), idx_hbm, idx_smem, idx_sem, ys_hbm, ybuf, row_sems, 2 * n)
    _gather_step(*gather_args)
    y0 = ybuf[i % 2, pl.ds(0, n), :]
    y1 = ybuf[i % 2, pl.ds(n, n), :]
    rt = rt_ref[...]
    moe = rt[:, ROUTE_G0:ROUTE_G0 + 1] * y0 + rt[:, ROUTE_G1:ROUTE_G1 + 1] * y1
    o_ref[...] = x_ref[...] + mod_ref[5:6, :] * moe
    _gather_drain(*gather_args)


def _combine(dest_tiles, ys, xa, modv, route, geom, rows_out):
    d = xa.shape[1]
    t = rows_out
    n = COMBINE_ROWS
    tpb, nlt = geom["s"] // n, geom["n_lat"] // n
    return pl.pallas_call(
        _combine_kernel,
        grid=(t // n,),
        in_specs=[
            pl.BlockSpec(memory_space=pl.ANY),
            pl.BlockSpec(memory_space=pl.ANY),
            pl.BlockSpec((n, d), lambda i: (i, 0)),
            pl.BlockSpec((None, 8, d), lambda i: (_variant_of_tile(i, tpb, nlt), 0, 0)),
            pl.BlockSpec((n, LANES), lambda i: (i, 0)),
        ],
        out_specs=pl.BlockSpec((n, d), lambda i: (i, 0)),
        out_shape=jax.ShapeDtypeStruct((t, d), F32),
        scratch_shapes=[
            pltpu.SMEM((3, 2 * n), I32),
            pltpu.VMEM((2, 2 * n, d), F32),
            pltpu.SemaphoreType.DMA((3,)),
            pltpu.SemaphoreType.DMA((2,)),
        ],
        compiler_params=_cparams("arbitrary"),
        name="moe_combine_residual",
    )(dest_tiles, ys, xa, modv, route)


def _moe(xa, tokens_packed, logits, modv, wg, wu, wd, layer, geom, rows_out):
    t = xa.shape[0]
    rows = MOE_ROWS
    route, counts = _route(logits)
    cnt = counts[0, :MOE_EXPERTS].astype(I32)
    padded = (cnt + rows - 1) // rows * rows
    pad_end = jnp.cumsum(padded)
    pad_start = pad_end - padded
    n_blocks = (2 * t + MOE_EXPERTS * (rows - 1) + rows - 1) // rows
    eid = route[:, ROUTE_E0:ROUTE_E1 + 1].astype(I32)
    rank = route[:, ROUTE_R0:ROUTE_R1 + 1].astype(I32)
    dest = pad_start[eid] + rank
    tok = jnp.broadcast_to(jnp.arange(t, dtype=I32)[:, None], (t, 2))
    slot_tok = jnp.zeros((n_blocks * rows,), I32).at[dest.reshape(-1)].set(
        tok.reshape(-1), unique_indices=True)
    block_start = jnp.arange(n_blocks, dtype=I32) * rows
    block_exp = jnp.minimum(
        jnp.sum((pad_end[None, :] <= block_start[:, None]).astype(I32), axis=1), MOE_EXPERTS - 1)
    n_used = (pad_end[-1:] // rows).astype(I32)
    ys = _experts(block_exp, n_used, slot_tok.reshape(n_blocks, rows) * TOKEN_SUBROWS, tokens_packed, wg, wu, wd,
                  layer)
    n = COMBINE_ROWS
    dest_tiles = dest.reshape(t // n, n, 2).transpose(0, 2, 1).reshape(t // n, 2 * n)
    return _combine(dest_tiles, ys, xa, modv, route, geom, rows_out)


def kernel(x, c, ctx, c_ctx, ada_w, ada_b, norm_g, ssm_w_in, ssm_conv_w, ssm_conv_b, ssm_a_log, ssm_dt_bias, ssm_d, ssm_norm_g, ssm_w_out, win_w_qkv, win_q_g, win_k_g, win_sinks, win_w_out, diff_w_qkv, diff_q_g, diff_k_g, diff_lam, diff_subln_g, diff_w_out, moe_w_group, moe_b_group, moe_w_expert, moe_b_expert, moe_w_gate, moe_w_up, moe_w_down):
    bsz, s, d = x.shape
    n_ctx = ctx.shape[1]
    depth = ada_w.shape[0]
    n_lat = bsz * s
    t = n_lat + bsz * n_ctx
    geom = {"b": bsz, "s": s, "c": n_ctx, "n_lat": n_lat, "t": t}
    assert d == D_MODEL and 1 + bsz <= 8
    assert s % max(ROW_TILE, 2048 if s >= 2048 else ROW_TILE) == 0 and (bsz * n_ctx) % ROW_TILE == 0
    assert n_ctx % 256 == 0 and s % GRID_W == 0

    xa = jnp.concatenate([x.reshape(n_lat, d), ctx.reshape(bsz * n_ctx, d)], axis=0)
    cvecs = jnp.zeros((8, d), F32).at[0].set(c_ctx).at[1:1 + bsz].set(c)
    mods = _modulation(cvecs, ada_w, ada_b)
    mods = mods.reshape(depth, 8, 6, d)
    mods = jnp.concatenate([mods, jnp.zeros((depth, 8, 2, d), F32)], axis=2)
    cos_t, sin_t = _rope_tables(geom)
    scale = HEAD_DIM ** -0.5

    for i in range(depth):
        kind, j = i % N_MIXERS, i // N_MIXERS
        modv = mods[i]
        r_w = jnp.zeros((d, LANES), F32).at[:, :MOE_GROUPS].set(moe_w_group[i])
        r_w = r_w.at[:, MOE_GROUPS:MOE_GROUPS + MOE_EXPERTS].set(moe_w_expert[i])
        r_hi = r_w.astype(BF16)
        r_lo = (r_w - r_hi.astype(F32)).astype(BF16)
        r_b = jnp.zeros((1, LANES), F32).at[0, :MOE_GROUPS].set(moe_b_group[i])
        r_b = r_b.at[0, MOE_GROUPS:MOE_GROUPS + MOE_EXPERTS].set(moe_b_expert[i])
        router = (jnp.concatenate([r_hi, r_lo], axis=1), r_b)

        if kind == 0:
            w_in = ssm_w_in[j].astype(BF16)
            n_main = SSM_D_INNER + SSM_CONV_DIM
            zx, dt_raw = _inproj(xa, modv, norm_g[i, 0], w_in[:, :n_main], w_in[:, n_main:], geom)
            xbc = _ssd_conv(zx, ssm_conv_w[j], ssm_conv_b[j], geom)
            a = -jnp.exp(ssm_a_log[j].astype(F32))
            ys_dir = _ssd_scan(xbc, dt_raw, a, ssm_dt_bias[j].astype(F32), geom)
            dsk = jnp.repeat(ssm_d[j].astype(F32), SSM_HEAD_DIM).reshape(1, SSM_D_INNER)
            a_args = (xbc, ys_dir[0], ys_dir[1], zx, dsk, ssm_norm_g[j].reshape(1, SSM_D_INNER))
            xa, tok_p, logits = _outproj(a_args, ssm_w_out[j].astype(BF16), xa, modv, norm_g[i, 1],
                                         *router, geom, ssd=True)
        elif kind == 1:
            qw, kvw = WIN_HEADS * HEAD_DIM, WIN_KV_HEADS * HEAD_DIM
            qkv = _inproj(xa, modv, norm_g[i, 0], win_w_qkv[j].astype(BF16), None, geom)
            gains = jnp.concatenate([jnp.tile(win_q_g[j], WIN_HEADS), jnp.tile(win_k_g[j], WIN_KV_HEADS)])
            post = jnp.concatenate([jnp.full((qw,), scale, F32), jnp.ones((kvw,), F32)])
            qk, _ = _qk_prep(qkv, gains.reshape(1, -1), post.reshape(1, -1), cos_t, sin_t)
            o = _win_attn(qk, qkv, win_sinks[j].astype(F32), geom)
            xa, tok_p, logits = _outproj((o,), win_w_out[j].astype(BF16), xa, modv, norm_g[i, 1],
                                         *router, geom, ssd=False)
        else:
            lam_init = 0.8 - 0.6 * math.exp(-0.3 * i)
            hw = 2 * DIFF_HEADS * HEAD_DIM
            qkv = _inproj(xa, modv, norm_g[i, 0], diff_w_qkv[j].astype(BF16), None, geom)
            gains = jnp.concatenate([jnp.tile(diff_q_g[j], 2 * DIFF_HEADS), jnp.tile(diff_k_g[j], 2 * DIFF_HEADS)])
            post = jnp.concatenate([jnp.full((hw,), scale * math.log2(math.e), F32), jnp.ones((hw,), F32)])
            qk, norms = _qk_prep(qkv, gains.reshape(1, -1), post.reshape(1, -1), cos_t, sin_t)
            o = _diff_attn(qk, qkv, norms, diff_lam[j].astype(F32), diff_subln_g[j].astype(F32), lam_init, geom)
            xa, tok_p, logits = _outproj((o,), diff_w_out[j].astype(BF16), xa, modv, norm_g[i, 1],
                                         *router, geom, ssd=False)

        rows_out = n_lat if i == depth - 1 else t
        xa = _moe(xa, tok_p, logits, modv, moe_w_gate, moe_w_up, moe_w_down, i, geom, rows_out)

    return xa.reshape(bsz, s, d)
```

```python
import functools
import math

import jax
import jax.numpy as jnp
from jax import lax
from jax.experimental import pallas as pl
from jax.experimental.pallas import tpu as pltpu

F32 = jnp.float32
BF16 = jnp.bfloat16
I32 = jnp.int32
U32 = jnp.uint32

D_MODEL = 2048
GRID_W = 64
NORM_EPS = 1e-6
ROPE_THETA = 10000.0
N_MIXERS = 3

SSM_D_INNER = 2 * D_MODEL
SSM_HEAD_DIM = 64
SSM_HEADS = SSM_D_INNER // SSM_HEAD_DIM
SSM_STATE = 128
SSM_GROUPS = 8
SSM_CHUNK = 128
SSM_CONV_DIM = SSM_D_INNER + 2 * SSM_GROUPS * SSM_STATE
SSM_GROUP_W = SSM_D_INNER // SSM_GROUPS

HEAD_DIM = 128
WIN_HEADS = D_MODEL // HEAD_DIM
WIN_KV_HEADS = 4
WINDOW = 128
ATT_BLOCK = 128
DIFF_HEADS = D_MODEL // (2 * HEAD_DIM)

MOE_GROUPS = 4
MOE_EPG = 8
MOE_EXPERTS = MOE_GROUPS * MOE_EPG
MOE_D_FF = D_MODEL // 4

LANES = 128
NEG = -1e30
VMEM_LIMIT = 48 * 1024 * 1024

ROW_TILE = 512
MOE_ROWS = 256
COMBINE_ROWS = 256
TOKEN_SUBROWS = D_MODEL // 2 // LANES


def _cparams(*sem):
    return pltpu.CompilerParams(dimension_semantics=sem, vmem_limit_bytes=VMEM_LIMIT)


def _variant_of_tile(i, tiles_per_batch, n_latent_tiles):
    return jnp.where(i < n_latent_tiles, 1 + i // tiles_per_batch, 0)


def _silu(v):
    return v * jax.nn.sigmoid(v)


def _split3(v):
    hi = v.astype(BF16)
    r1 = v - hi.astype(F32)
    mid = r1.astype(BF16)
    lo = (r1 - mid.astype(F32)).astype(BF16)
    return hi, mid, lo


def _mod_kernel(c_ref, w_ref, b_ref, o_ref):
    s = _silu(c_ref[...]).astype(BF16)
    o_ref[...] = jnp.dot(s, w_ref[...].astype(BF16), preferred_element_type=F32) + b_ref[...]


def _modulation(cvecs, ada_w, ada_b):
    depth, d, n6 = ada_w.shape
    tn = 1024
    return pl.pallas_call(
        _mod_kernel,
        grid=(depth, n6 // tn),
        in_specs=[
            pl.BlockSpec((8, d), lambda l, j: (0, 0)),
            pl.BlockSpec((None, d, tn), lambda l, j: (l, 0, j)),
            pl.BlockSpec((None, 1, tn), lambda l, j: (l, 0, j)),
        ],
        out_specs=pl.BlockSpec((None, 8, tn), lambda l, j: (l, 0, j)),
        out_shape=jax.ShapeDtypeStruct((depth, 8, n6), F32),
        compiler_params=_cparams("parallel", "parallel"),
        name="adaln_modulation",
    )(cvecs, ada_w, ada_b.reshape(depth, 1, n6))


def _inproj_kernel(x_ref, mod_ref, g_ref, w_ref, *rest, has_tail):
    if has_tail:
        wt_ref, o_ref, ot_ref, h_ref = rest
    else:
        o_ref, h_ref = rest

    @pl.when(pl.program_id(1) == 0)
    def _():
        x = x_ref[...]
        ms = jnp.mean(x * x, axis=-1, keepdims=True)
        y = x * lax.rsqrt(ms + NORM_EPS) * g_ref[...]
        h = (y * (1.0 + mod_ref[1:2, :]) + mod_ref[0:1, :]).astype(BF16)
        h_ref[...] = h
        if has_tail:
            ot_ref[...] = jnp.dot(h, wt_ref[...], preferred_element_type=F32)

    o_ref[...] = jnp.dot(h_ref[...], w_ref[...], preferred_element_type=F32).astype(o_ref.dtype)


def _inproj(xa, modv, gain, w, w_tail, geom):
    t, d = xa.shape
    n = w.shape[1]
    tm = ROW_TILE
    tn = 2048 if n % 2048 == 0 else 1024
    tpb, nlt = geom["s"] // tm, geom["n_lat"] // tm
    var = lambda i, j: (_variant_of_tile(i, tpb, nlt), 0, 0)
    in_specs = [
        pl.BlockSpec((tm, d), lambda i, j: (i, 0)),
        pl.BlockSpec((None, 8, d), var),
        pl.BlockSpec((1, d), lambda i, j: (0, 0)),
        pl.BlockSpec((d, tn), lambda i, j: (0, j)),
    ]
    out_specs = [pl.BlockSpec((tm, tn), lambda i, j: (i, j))]
    out_shape = [jax.ShapeDtypeStruct((t, n), BF16)]
    args = [xa, modv, gain.reshape(1, d), w]
    if w_tail is not None:
        nt = w_tail.shape[1]
        in_specs.append(pl.BlockSpec((d, nt), lambda i, j: (0, 0)))
        out_specs.append(pl.BlockSpec((tm, nt), lambda i, j: (i, 0)))
        out_shape.append(jax.ShapeDtypeStruct((t, nt), F32))
        args.append(w_tail)
    res = pl.pallas_call(
        functools.partial(_inproj_kernel, has_tail=w_tail is not None),
        grid=(t // tm, n // tn),
        in_specs=in_specs,
        out_specs=out_specs,
        out_shape=out_shape,
        scratch_shapes=[pltpu.VMEM((tm, d), BF16)],
        compiler_params=_cparams("parallel", "arbitrary"),
        name="norm_mod_inproj",
    )(*args)
    return res if w_tail is not None else res[0]


def _conv_kernel(xp_ref, x_ref, xn_ref, w_ref, b_ref, o_ref, *, rows, tiles_lat, n_lat_tiles, tiles_ctx):
    i = pl.program_id(0)
    in_lat = i < n_lat_tiles
    k = jnp.where(in_lat, i % tiles_lat, (i - n_lat_tiles) % tiles_ctx)
    n = jnp.where(in_lat, tiles_lat, tiles_ctx)
    xb = x_ref[...]
    x = xb.astype(F32)
    prev_row = jnp.where(k == 0, 0.0, xp_ref[...].astype(F32)[15:16, :])
    next_row = jnp.where(k == n - 1, 0.0, xn_ref[...].astype(F32)[0:1, :])
    r = lax.broadcasted_iota(I32, (rows, 1), 0)
    ri = lax.broadcasted_iota(I32, (rows, rows), 0)
    ci = lax.broadcasted_iota(I32, (rows, rows), 1)
    shift_dn = (ri == ci + 1).astype(BF16)
    shift_up = (ri + 1 == ci).astype(BF16)
    xm1 = jnp.where(r == 0, prev_row, jnp.dot(shift_dn, xb, preferred_element_type=F32))
    xp1 = jnp.where(r == rows - 1, next_row, jnp.dot(shift_up, xb, preferred_element_type=F32))
    out = xm1 * w_ref[0:1, :] + x * w_ref[1:2, :] + xp1 * w_ref[2:3, :] + b_ref[...]
    o_ref[...] = _silu(out).astype(o_ref.dtype)


def _ssd_conv(zx, conv_w, conv_b, geom):
    t = zx.shape[0]
    rows, wc = 256, 2048
    col0 = SSM_D_INNER // wc
    halo = 16
    rb = rows // halo
    last_halo = t // halo - 1
    kern = functools.partial(_conv_kernel, rows=rows, tiles_lat=geom["s"] // rows,
                             n_lat_tiles=geom["n_lat"] // rows, tiles_ctx=geom["c"] // rows)
    return pl.pallas_call(
        kern,
        grid=(t // rows, SSM_CONV_DIM // wc),
        in_specs=[
            pl.BlockSpec((halo, wc), lambda i, j: (jnp.maximum(i * rb - 1, 0), col0 + j)),
            pl.BlockSpec((rows, wc), lambda i, j: (i, col0 + j)),
            pl.BlockSpec((halo, wc), lambda i, j: (jnp.minimum((i + 1) * rb, last_halo), col0 + j)),
            pl.BlockSpec((3, wc), lambda i, j: (0, j)),
            pl.BlockSpec((1, wc), lambda i, j: (0, j)),
        ],
        out_specs=pl.BlockSpec((rows, wc), lambda i, j: (i, j)),
        out_shape=jax.ShapeDtypeStruct((t, SSM_CONV_DIM), BF16),
        compiler_params=_cparams("parallel", "parallel"),
        name="ssd_conv_silu",
    )(zx, zx, zx, conv_w, conv_b.reshape(1, SSM_CONV_DIM))


def _ssd_scan_kernel(xf_ref, bf_ref, cf_ref, dtf_ref, xr_ref, br_ref, cr_ref, dtr_ref, a_ref, dtb_ref, e_ref,
                     yf_ref, yr_ref, state_ref):
    @pl.when(pl.program_id(1) == 0)
    def _():
        state_ref[...] = jnp.zeros_like(state_ref)

    _ssd_chunk(xf_ref, bf_ref, cf_ref, dtf_ref, a_ref.at[0:1], dtb_ref.at[0:1], e_ref, yf_ref, state_ref.at[0],
               reverse=False)
    _ssd_chunk(xr_ref, br_ref, cr_ref, dtr_ref, a_ref.at[1:2], dtb_ref.at[1:2], e_ref, yr_ref, state_ref.at[1],
               reverse=True)


def _ssd_chunk(xs_ref, b_ref, c_ref, dt_ref, a_ref, dtb_ref, e_ref, y_ref, state_ref, *, reverse):
    q = SSM_CHUNK
    hpg = SSM_HEADS // SSM_GROUPS
    c0 = SSM_HEADS if reverse else 0
    pre = dt_ref[:, c0:c0 + SSM_HEADS] + dtb_ref[...]
    dt = jnp.maximum(pre, 0.0) + jnp.log1p(jnp.exp(-jnp.abs(pre)))
    la = dt * a_ref[...]
    ri = lax.broadcasted_iota(I32, (q, q), 0)
    ci = lax.broadcasted_iota(I32, (q, q), 1)
    tri = (ri <= ci) if reverse else (ri >= ci)
    trib = tri.astype(BF16)
    hi, mid, lo = _split3(la)
    cum = (jnp.dot(trib, hi, preferred_element_type=F32) + jnp.dot(trib, mid, preferred_element_type=F32)
           + jnp.dot(trib, lo, preferred_element_type=F32))
    total = cum[0:1, :] if reverse else cum[q - 1:q, :]
    cum_t = cum.T
    dt_t = dt.T
    lane = lax.broadcasted_iota(I32, (1, LANES), 1)
    first_half = lane < SSM_HEAD_DIM

    def per_head_to_columns(v, with_lo=True):
        hi = v.astype(BF16)
        out = jnp.dot(hi, e_ref[...], preferred_element_type=F32)
        if with_lo:
            lo = (v - hi.astype(F32)).astype(BF16)
            out = out + jnp.dot(lo, e_ref[...], preferred_element_type=F32)
        return out

    exp_cum_cols = per_head_to_columns(jnp.exp(cum))
    w_state_cols = per_head_to_columns(jnp.exp(total - cum) * dt, with_lo=False)
    exp_total_cols = per_head_to_columns(jnp.broadcast_to(jnp.exp(total), (8, SSM_HEADS)))[0:1, :]

    for g in range(SSM_GROUPS):
        bg = b_ref[:, g * SSM_STATE:(g + 1) * SSM_STATE]
        cg = c_ref[:, g * SSM_STATE:(g + 1) * SSM_STATE]
        cb = lax.dot_general(cg, bg, (((1,), (1,)), ((), ())), preferred_element_type=F32)
        st = state_ref[g]
        y_state = jnp.dot(cg, st.astype(BF16), preferred_element_type=F32)
        xw_parts = []
        for pair in range(hpg // 2):
            h0 = g * hpg + 2 * pair
            col = g * SSM_GROUP_W + pair * LANES
            xpair = xs_ref[:, col:col + LANES]
            ws = []
            for h in (h0, h0 + 1):
                seg = cum[:, h:h + 1] - cum_t[h:h + 1, :]
                dec = jnp.exp(jnp.where(tri, seg, NEG))
                ws.append((cb * dec * dt_t[h:h + 1, :]).astype(BF16))
            zero = jnp.zeros_like(xpair)
            x_diag = jnp.concatenate([jnp.where(first_half, xpair, zero), jnp.where(first_half, zero, xpair)],
                                     axis=0)
            y_intra = jnp.dot(jnp.concatenate(ws, axis=1), x_diag, preferred_element_type=F32)
            ysp = y_state[:, pair * LANES:(pair + 1) * LANES]
            y_ref[:, col:col + LANES] = (y_intra + ysp * exp_cum_cols[:, col:col + LANES]).astype(y_ref.dtype)
            xw_parts.append((xpair.astype(F32) * w_state_cols[:, col:col + LANES]).astype(BF16))
        xw = jnp.concatenate(xw_parts, axis=1)
        scale_row = exp_total_cols[:, g * SSM_GROUP_W:(g + 1) * SSM_GROUP_W]
        upd = lax.dot_general(bg, xw, (((0,), (0,)), ((), ())), preferred_element_type=F32)
        state_ref[g] = st * scale_row + upd


def _ssd_scan(xbc, dt_raw, a, dt_bias, geom):
    t = xbc.shape[0]
    q = SSM_CHUNK
    bsz = geom["b"]
    cq, lq = geom["c"] // q, geom["s"] // q
    lat_blocks = bsz * lq

    def row_block(reverse):
        def f(b, s):
            cchunk = (cq - 1 - s) if reverse else s
            lchunk = (lq - 1 - (s - cq)) if reverse else (s - cq)
            return jnp.where(s < cq, lat_blocks + b * cq + cchunk, b * lq + lchunk)
        return f

    bc_w = SSM_GROUPS * SSM_STATE
    bcol = SSM_D_INNER // bc_w
    head_of_col = jnp.arange(SSM_D_INNER, dtype=I32) // SSM_HEAD_DIM
    expand = (head_of_col[None, :] == jnp.arange(SSM_HEADS, dtype=I32)[:, None]).astype(BF16)

    def chunk_specs(reverse):
        rb = row_block(reverse)
        return [
            pl.BlockSpec((q, SSM_D_INNER), lambda b, s: (rb(b, s), 0)),
            pl.BlockSpec((q, bc_w), lambda b, s: (rb(b, s), bcol)),
            pl.BlockSpec((q, bc_w), lambda b, s: (rb(b, s), bcol + 1)),
            pl.BlockSpec((q, 2 * SSM_HEADS), lambda b, s: (rb(b, s), 0)),
        ]

    const = lambda b, s: (0, 0)
    return pl.pallas_call(
        _ssd_scan_kernel,
        grid=(bsz, cq + lq),
        in_specs=chunk_specs(False) + chunk_specs(True) + [
            pl.BlockSpec((2, SSM_HEADS), const),
            pl.BlockSpec((2, SSM_HEADS), const),
            pl.BlockSpec((SSM_HEADS, SSM_D_INNER), const),
        ],
        out_specs=[pl.BlockSpec((q, SSM_D_INNER), lambda b, s: (row_block(False)(b, s), 0)),
                   pl.BlockSpec((q, SSM_D_INNER), lambda b, s: (row_block(True)(b, s), 0))],
        out_shape=[jax.ShapeDtypeStruct((t, SSM_D_INNER), BF16)] * 2,
        scratch_shapes=[pltpu.VMEM((2, SSM_GROUPS, SSM_STATE, SSM_GROUP_W), F32)],
        compiler_params=_cparams("parallel", "arbitrary"),
        name="ssd_scan",
    )(xbc, xbc, xbc, dt_raw, xbc, xbc, xbc, dt_raw, a, dt_bias, expand)


def _qk_prep_kernel(x_ref, g_ref, s_ref, cos_ref, sin_ref, o_ref, n_ref, *, heads):
    j = pl.program_id(1)
    cosv = cos_ref[...]
    sinv = sin_ref[...]
    lane = lax.broadcasted_iota(I32, (1, HEAD_DIM), 1)
    low = (lane % (HEAD_DIM // 2)) < (HEAD_DIM // 4)

    @pl.when(j == 0)
    def _():
        n_ref[...] = jnp.zeros_like(n_ref)

    sq_norms = n_ref[...]
    for h in range(heads):
        sl = slice(h * HEAD_DIM, (h + 1) * HEAD_DIM)
        x = x_ref[:, sl].astype(F32)
        ms = jnp.mean(x * x, axis=-1, keepdims=True)
        y = x * lax.rsqrt(ms + NORM_EPS) * g_ref[:, sl]
        partner = jnp.where(low, pltpu.roll(y, HEAD_DIM - HEAD_DIM // 4, 1), pltpu.roll(y, HEAD_DIM // 4, 1))
        ob = ((y * cosv + partner * sinv) * s_ref[:, sl]).astype(o_ref.dtype)
        o_ref[:, sl] = ob
        of = ob.astype(F32)
        sq_norms = jnp.where(lane == j * heads + h, jnp.sum(of * of, axis=-1, keepdims=True), sq_norms)
    n_ref[...] = sq_norms


def _qk_prep(qkv, gains, post_scale, cos_t, sin_t):
    t = qkv.shape[0]
    n = gains.shape[1]
    rows, wb = 512, 512
    return pl.pallas_call(
        functools.partial(_qk_prep_kernel, heads=wb // HEAD_DIM),
        grid=(t // rows, n // wb),
        in_specs=[
            pl.BlockSpec((rows, wb), lambda i, j: (i, j)),
            pl.BlockSpec((1, wb), lambda i, j: (0, j)),
            pl.BlockSpec((1, wb), lambda i, j: (0, j)),
            pl.BlockSpec((rows, HEAD_DIM), lambda i, j: (i, 0)),
            pl.BlockSpec((rows, HEAD_DIM), lambda i, j: (i, 0)),
        ],
        out_specs=[pl.BlockSpec((rows, wb), lambda i, j: (i, j)),
                   pl.BlockSpec((rows, HEAD_DIM), lambda i, j: (i, 0))],
        out_shape=[jax.ShapeDtypeStruct((t, n), BF16), jax.ShapeDtypeStruct((t, HEAD_DIM), F32)],
        compiler_params=_cparams("parallel", "arbitrary"),
        name="qk_norm_rope",
    )(qkv, gains, post_scale, cos_t, sin_t)


def _rope_tables(geom):
    s, t = geom["s"], geom["t"]
    pos = jnp.arange(s)
    row = (pos // GRID_W).astype(F32)
    col = (pos % GRID_W).astype(F32)
    quarter = HEAD_DIM // 4
    inv_freq = ROPE_THETA ** (-jnp.arange(quarter, dtype=F32) / quarter)
    ar = row[:, None] * inv_freq
    ac = col[:, None] * inv_freq
    cos_l = jnp.concatenate([jnp.cos(ar), jnp.cos(ar), jnp.cos(ac), jnp.cos(ac)], axis=1)
    sin_l = jnp.concatenate([-jnp.sin(ar), jnp.sin(ar), -jnp.sin(ac), jnp.sin(ac)], axis=1)
    n_ctx_rows = t - geom["n_lat"]
    cos_t = jnp.concatenate([jnp.tile(cos_l, (geom["b"], 1)), jnp.ones((n_ctx_rows, HEAD_DIM), F32)], axis=0)
    sin_t = jnp.concatenate([jnp.tile(sin_l, (geom["b"], 1)), jnp.zeros((n_ctx_rows, HEAD_DIM), F32)], axis=0)
    return cos_t, sin_t


def _win_attn_kernel(sink_ref, q_ref, kp_ref, kc_ref, kn_ref, vp_ref, vc_ref, vn_ref, kx_ref, vx_ref, o_ref,
                     *, n_lat_blocks, ctx_blocks):
    s = pl.program_id(1)
    grp = WIN_HEADS // WIN_KV_HEADS
    blk = ATT_BLOCK
    is_lat = s >= ctx_blocks
    n = s - ctx_blocks
    rows = grp * blk
    qi = lax.broadcasted_iota(I32, (rows, 3 * blk), 0) % blk
    kk = lax.broadcasted_iota(I32, (rows, 3 * blk), 1)
    rel = kk - blk - qi
    in_band = (rel <= WINDOW) & (rel >= -WINDOW)
    lo = jnp.where(n > 0, 0, blk)
    hi = jnp.where(is_lat, jnp.where(n < n_lat_blocks - 1, 3 * blk, 2 * blk), 0)
    mask = in_band & (kk >= lo) & (kk < hi)
    rowi = lax.broadcasted_iota(I32, (rows, 1), 0)
    nt = (((1,), (1,)), ((), ()))
    for kh in range(WIN_KV_HEADS):
        ks = slice(kh * HEAD_DIM, (kh + 1) * HEAD_DIM)
        k_lat = jnp.concatenate([kp_ref[:, ks], kc_ref[:, ks], kn_ref[:, ks]], axis=0)
        v_lat = jnp.concatenate([vp_ref[:, ks], vc_ref[:, ks], vn_ref[:, ks]], axis=0)
        qg = jnp.concatenate(
            [q_ref[:, (kh * grp + j) * HEAD_DIM:(kh * grp + j + 1) * HEAD_DIM] for j in range(grp)], axis=0)
        s_lat = jnp.where(mask, lax.dot_general(qg, k_lat, nt, preferred_element_type=F32), NEG)
        s_ctx = lax.dot_general(qg, kx_ref[:, ks], nt, preferred_element_type=F32)
        sink = jnp.full((rows, 1), sink_ref[kh * grp], F32)
        for j in range(1, grp):
            sink = jnp.where(rowi >= j * blk, sink_ref[kh * grp + j], sink)
        m = jnp.maximum(jnp.maximum(jnp.max(s_lat, axis=-1, keepdims=True),
                                    jnp.max(s_ctx, axis=-1, keepdims=True)), sink)
        p_lat = jnp.exp(s_lat - m)
        p_ctx = jnp.exp(s_ctx - m)
        denom = (jnp.sum(p_lat, axis=-1, keepdims=True) + jnp.sum(p_ctx, axis=-1, keepdims=True)
                 + jnp.exp(sink - m))
        o = (jnp.dot(p_lat.astype(BF16), v_lat, preferred_element_type=F32)
             + jnp.dot(p_ctx.astype(BF16), vx_ref[:, ks], preferred_element_type=F32)) / denom
        for j in range(grp):
            hq = kh * grp + j
            o_ref[:, hq * HEAD_DIM:(hq + 1) * HEAD_DIM] = o[j * blk:(j + 1) * blk].astype(o_ref.dtype)


def _win_attn(qk, qkv, sinks, geom):
    t = qk.shape[0]
    blk = ATT_BLOCK
    bsz, c = geom["b"], geom["c"]
    nb = geom["s"] // blk
    cb = c // blk
    lat_blocks = bsz * nb
    kvw = WIN_KV_HEADS * HEAD_DIM
    qw = WIN_HEADS * HEAD_DIM
    kcol = qw // kvw
    vcol = (qw + kvw) // kvw

    def qrow(b, s, sk):
        return jnp.where(s < cb, lat_blocks + b * cb + s, b * nb + (s - cb))

    def lat(off):
        def f(b, s, sk):
            n = jnp.clip(s - cb + off, 0, nb - 1)
            return b * nb + n
        return f

    ctx_row = lambda b, s, sk: (geom["n_lat"] // c + b)
    grid_spec = pltpu.PrefetchScalarGridSpec(
        num_scalar_prefetch=1,
        grid=(bsz, cb + nb),
        in_specs=[
            pl.BlockSpec((blk, qw), lambda b, s, sk: (qrow(b, s, sk), 0)),
            pl.BlockSpec((blk, kvw), lambda b, s, sk: (lat(-1)(b, s, sk), kcol)),
            pl.BlockSpec((blk, kvw), lambda b, s, sk: (lat(0)(b, s, sk), kcol)),
            pl.BlockSpec((blk, kvw), lambda b, s, sk: (lat(1)(b, s, sk), kcol)),
            pl.BlockSpec((blk, kvw), lambda b, s, sk: (lat(-1)(b, s, sk), vcol)),
            pl.BlockSpec((blk, kvw), lambda b, s, sk: (lat(0)(b, s, sk), vcol)),
            pl.BlockSpec((blk, kvw), lambda b, s, sk: (lat(1)(b, s, sk), vcol)),
            pl.BlockSpec((c, kvw), lambda b, s, sk: (ctx_row(b, s, sk), kcol)),
            pl.BlockSpec((c, kvw), lambda b, s, sk: (ctx_row(b, s, sk), vcol)),
        ],
        out_specs=pl.BlockSpec((blk, qw), lambda b, s, sk: (qrow(b, s, sk), 0)),
    )
    return pl.pallas_call(
        functools.partial(_win_attn_kernel, n_lat_blocks=nb, ctx_blocks=cb),
        grid_spec=grid_spec,
        out_shape=jax.ShapeDtypeStruct((t, qw), BF16),
        compiler_params=_cparams("parallel", "parallel"),
        name="window_gqa",
    )(sinks, qk, qk, qk, qk, qkv, qkv, qkv, qk, qkv)


DIFF_SAFE_EXCESS = 96.0


def _diff_attn_kernel(*refs, use_latent, lam_init, sub):
    if use_latent:
        (kmax_ref, q_ref, kx_ref, vx_ref, k_ref, v_ref, qn_ref, lam_ref, g_ref, o_ref,
         m_ref, l_ref, acc_ref, stat_ref) = refs
    else:
        q_ref, kx_ref, vx_ref, lam_ref, g_ref, _, o_ref, m_ref, l_ref, acc_ref = refs
    j = pl.program_id(3)
    nt = (((1,), (1,)), ((), ()))
    hd = HEAD_DIM

    def update(idx, qh, kh, v):
        s = lax.dot_general(qh, kh, nt, preferred_element_type=F32)
        m_old = m_ref[idx]
        m_new = jnp.maximum(m_old, jnp.max(s, axis=-1, keepdims=True))
        alpha = jnp.exp2(m_old - m_new)
        p = jnp.exp2(s - m_new)
        l_ref[idx] = alpha * l_ref[idx] + jnp.sum(p, axis=-1, keepdims=True)
        acc_ref[idx] = alpha * acc_ref[idx] + jnp.dot(p.astype(BF16), v, preferred_element_type=F32)
        m_ref[idx] = m_new

    @pl.when(j == 0)
    def _():
        for idx in range(2):
            s = lax.dot_general(q_ref[:, idx * hd:(idx + 1) * hd], kx_ref[:, idx * hd:(idx + 1) * hd], nt,
                                preferred_element_type=F32)
            m0 = jnp.max(s, axis=-1, keepdims=True)
            p = jnp.exp2(s - m0)
            m_ref[idx] = m0
            l_ref[idx] = jnp.sum(p, axis=-1, keepdims=True)
            acc_ref[idx] = jnp.dot(p.astype(BF16), vx_ref[...], preferred_element_type=F32)
        if use_latent:
            lane = lax.broadcasted_iota(I32, qn_ref.shape, 1)
            for idx in range(2):
                head = 2 * pl.program_id(1) + idx
                q_sq = jnp.sum(jnp.where(lane == head, qn_ref[...], 0.0), axis=-1, keepdims=True)
                stat_ref[idx] = jnp.sqrt(jnp.max(q_sq))
                stat_ref[2 + idx] = jnp.min(m_ref[idx])

    if use_latent:
        b, h = pl.program_id(0), pl.program_id(1)
        excess = []
        for idx in range(2):
            k_norm = kmax_ref[(b * pl.num_programs(3) + j) * (2 * DIFF_HEADS) + 2 * h + idx]
            excess.append(stat_ref[idx] * k_norm - stat_ref[2 + idx])
        safe = jnp.maximum(excess[0], excess[1]) <= DIFF_SAFE_EXCESS

        @pl.when(safe)
        def _():
            for idx in range(2):
                qh = q_ref[:, idx * hd:(idx + 1) * hd]
                shift = m_ref[idx]
                ps = []
                lsum = jnp.zeros_like(shift)
                for c in range(k_ref.shape[0] // sub):
                    rs = slice(c * sub, (c + 1) * sub)
                    s = lax.dot_general(qh, k_ref[rs, idx * hd:(idx + 1) * hd], nt, preferred_element_type=F32)
                    p = jnp.exp2(s - shift)
                    lsum = lsum + jnp.sum(p, axis=-1, keepdims=True)
                    ps.append(p.astype(BF16))
                l_ref[idx] += lsum
                acc_ref[idx] += jnp.dot(jnp.concatenate(ps, axis=1), v_ref[...], preferred_element_type=F32)

        @pl.when(jnp.logical_not(safe))
        def _():
            for idx in range(2):
                update(idx, q_ref[:, idx * hd:(idx + 1) * hd], k_ref[:, idx * hd:(idx + 1) * hd], v_ref[...])
                stat_ref[2 + idx] = jnp.min(m_ref[idx])

    @pl.when(j == pl.num_programs(3) - 1)
    def _():
        lam = lam_ref[...]
        lam_full = (jnp.exp(jnp.sum(lam[0:1] * lam[1:2], axis=-1, keepdims=True))
                    - jnp.exp(jnp.sum(lam[2:3] * lam[3:4], axis=-1, keepdims=True)) + lam_init)
        o = acc_ref[0] / l_ref[0] - lam_full * (acc_ref[1] / l_ref[1])
        ms = jnp.mean(o * o, axis=-1, keepdims=True)
        o_ref[...] = (o * lax.rsqrt(ms + NORM_EPS) * g_ref[...] * (1.0 - lam_init)).astype(o_ref.dtype)


def _diff_attn(qk, qkv, norms, lam, subln_g, lam_init, geom):
    t = qk.shape[0]
    bsz, s, c = geom["b"], geom["s"], geom["c"]
    pw = 2 * HEAD_DIM
    tq = min(1024, s)
    tkb = min(2048, s)
    sub = min(256, tkb)
    nq, nk = s // tq, s // tkb
    ctx_row = geom["n_lat"] // c
    kcol0, vcol0 = DIFF_HEADS, 2 * DIFF_HEADS
    g2 = subln_g.reshape(1, pw)
    scratch = lambda rows: [pltpu.VMEM((2, rows, 1), F32), pltpu.VMEM((2, rows, 1), F32),
                            pltpu.VMEM((2, rows, pw), F32)]
    nh = 2 * DIFF_HEADS
    k_sq = norms[:geom["n_lat"], nh:2 * nh].reshape(bsz, nk, tkb, nh)
    kmax = (jnp.sqrt(jnp.max(k_sq, axis=2)) * 1.001).reshape(-1)
    grid_spec = pltpu.PrefetchScalarGridSpec(
        num_scalar_prefetch=1,
        grid=(bsz, DIFF_HEADS, nq, nk),
        in_specs=[
            pl.BlockSpec((tq, pw), lambda b, h, i, j, km: (b * nq + i, h)),
            pl.BlockSpec((c, pw), lambda b, h, i, j, km: (ctx_row + b, kcol0 + h)),
            pl.BlockSpec((c, pw), lambda b, h, i, j, km: (ctx_row + b, vcol0 + h)),
            pl.BlockSpec((tkb, pw), lambda b, h, i, j, km: (b * nk + j, kcol0 + h)),
            pl.BlockSpec((tkb, pw), lambda b, h, i, j, km: (b * nk + j, vcol0 + h)),
            pl.BlockSpec((tq, LANES), lambda b, h, i, j, km: (b * nq + i, 0)),
            pl.BlockSpec((4, HEAD_DIM), lambda b, h, i, j, km: (0, 0)),
            pl.BlockSpec((1, pw), lambda b, h, i, j, km: (0, 0)),
        ],
        out_specs=pl.BlockSpec((tq, pw), lambda b, h, i, j, km: (b * nq + i, h)),
        scratch_shapes=scratch(tq) + [pltpu.SMEM((4,), F32)],
    )
    o_lat = pl.pallas_call(
        functools.partial(_diff_attn_kernel, use_latent=True, lam_init=lam_init, sub=sub),
        grid_spec=grid_spec,
        out_shape=jax.ShapeDtypeStruct((t, D_MODEL), BF16),
        compiler_params=_cparams("parallel", "parallel", "parallel", "arbitrary"),
        name="diff_attn_latent",
    )(kmax, qk, qk, qkv, qk, qkv, norms, lam, g2)
    return pl.pallas_call(
        functools.partial(_diff_attn_kernel, use_latent=False, lam_init=lam_init, sub=sub),
        grid=(bsz, DIFF_HEADS, 1, 1),
        in_specs=[
            pl.BlockSpec((c, pw), lambda b, h, i, j: (ctx_row + b, h)),
            pl.BlockSpec((c, pw), lambda b, h, i, j: (ctx_row + b, kcol0 + h)),
            pl.BlockSpec((c, pw), lambda b, h, i, j: (ctx_row + b, vcol0 + h)),
            pl.BlockSpec((4, HEAD_DIM), lambda b, h, i, j: (0, 0)),
            pl.BlockSpec((1, pw), lambda b, h, i, j: (0, 0)),
            pl.BlockSpec(memory_space=pl.ANY),
        ],
        out_specs=pl.BlockSpec((c, pw), lambda b, h, i, j: (ctx_row + b, h)),
        out_shape=jax.ShapeDtypeStruct((t, D_MODEL), BF16),
        scratch_shapes=scratch(c),
        input_output_aliases={5: 0},
        compiler_params=_cparams("parallel", "parallel", "parallel", "arbitrary"),
        name="diff_attn_context",
    )(qk, qk, qkv, lam, g2, o_lat)


def _outproj_kernel(*refs, ssd):
    if ssd:
        (xs_ref, yf_ref, yb_ref, z_ref, dsk_ref, ng_ref, w_ref, x_ref, mod_ref, g_ref, rw_ref, rb_ref,
         xo_ref, hp_ref, lg_ref, acc_ref, ssq_ref) = refs
    else:
        (a_ref, w_ref, x_ref, mod_ref, g_ref, rw_ref, rb_ref,
         xo_ref, hp_ref, lg_ref, acc_ref) = refs
    k = pl.program_id(1)

    @pl.when(k == 0)
    def _():
        acc_ref[...] = jnp.zeros_like(acc_ref)
        if ssd:
            ssq_ref[...] = jnp.zeros_like(ssq_ref)

    if ssd:
        y = dsk_ref[...] * xs_ref[...].astype(F32) + yf_ref[...].astype(F32) + yb_ref[...].astype(F32)
        u = y * _silu(z_ref[...].astype(F32))
        ssq_ref[...] += jnp.sum(u * u, axis=-1, keepdims=True)
        a = (u * ng_ref[...]).astype(BF16)
    else:
        a = a_ref[...]
    acc_ref[...] += jnp.dot(a, w_ref[...], preferred_element_type=F32)

    @pl.when(k == pl.num_programs(1) - 1)
    def _():
        y = acc_ref[...]
        if ssd:
            y = y * lax.rsqrt(ssq_ref[...] * (1.0 / SSM_D_INNER) + NORM_EPS)
        xn = x_ref[...] + mod_ref[2:3, :] * y
        xo_ref[...] = xn
        ms = jnp.mean(xn * xn, axis=-1, keepdims=True)
        h = xn * lax.rsqrt(ms + NORM_EPS) * g_ref[...]
        h = h * (1.0 + mod_ref[4:5, :]) + mod_ref[3:4, :]
        hb = h.astype(BF16)
        hl = (h - hb.astype(F32)).astype(BF16)
        r2 = (jnp.dot(hb, rw_ref[...], preferred_element_type=F32)
              + jnp.dot(hl, rw_ref[...], preferred_element_type=F32))
        lg_ref[...] = r2[:, :LANES] + r2[:, LANES:] + rb_ref[...]
        half = D_MODEL // 2
        lo_bits = lax.bitcast_convert_type(hb[:, :half].astype(F32), U32) >> 16
        hi_bits = lax.bitcast_convert_type(hb[:, half:].astype(F32), U32)
        _tile_store(hp_ref, hi_bits | lo_bits, TOKEN_SUBROWS)


def _outproj(a_args, w, xa, modv, gain, r_w2, r_bias, geom, ssd):
    t, d = xa.shape
    kdim = w.shape[0]
    tm, tk = ROW_TILE, 1024
    tpb, nlt = geom["s"] // tm, geom["n_lat"] // tm
    var = lambda i, k: (_variant_of_tile(i, tpb, nlt), 0, 0)
    const = lambda i, k: (0, 0)
    if ssd:
        xbc, yf, yb, zx, dsk, ng = a_args
        zcol = 0
        a_specs = [
            pl.BlockSpec((tm, tk), lambda i, k: (i, k)),
            pl.BlockSpec((tm, tk), lambda i, k: (i, k)),
            pl.BlockSpec((tm, tk), lambda i, k: (i, k)),
            pl.BlockSpec((tm, tk), lambda i, k: (i, zcol + k)),
            pl.BlockSpec((1, tk), lambda i, k: (0, k)),
            pl.BlockSpec((1, tk), lambda i, k: (0, k)),
        ]
        a_in = [xbc, yf, yb, zx, dsk, ng]
        scratch = [pltpu.VMEM((tm, d), F32), pltpu.VMEM((tm, 1), F32)]
    else:
        a_specs = [pl.BlockSpec((tm, tk), lambda i, k: (i, k))]
        a_in = list(a_args)
        scratch = [pltpu.VMEM((tm, d), F32)]
    return pl.pallas_call(
        functools.partial(_outproj_kernel, ssd=ssd),
        grid=(t // tm, kdim // tk),
        in_specs=a_specs + [
            pl.BlockSpec((tk, d), lambda i, k: (k, 0)),
            pl.BlockSpec((tm, d), lambda i, k: (i, 0)),
            pl.BlockSpec((None, 8, d), var),
            pl.BlockSpec((1, d), const),
            pl.BlockSpec((d, 2 * LANES), const),
            pl.BlockSpec((1, LANES), const),
        ],
        out_specs=[
            pl.BlockSpec((tm, d), lambda i, k: (i, 0)),
            pl.BlockSpec((tm * TOKEN_SUBROWS, LANES), lambda i, k: (i, 0)),
            pl.BlockSpec((tm, LANES), lambda i, k: (i, 0)),
        ],
        out_shape=[
            jax.ShapeDtypeStruct((t, d), F32),
            jax.ShapeDtypeStruct((t * TOKEN_SUBROWS, LANES), U32),
            jax.ShapeDtypeStruct((t, LANES), F32),
        ],
        scratch_shapes=scratch,
        compiler_params=_cparams("parallel", "arbitrary"),
        name="outproj_residual_norm",
    )(*a_in, w, xa, modv, gain.reshape(1, d), r_w2, r_bias)


ROUTE_E0, ROUTE_E1, ROUTE_G0, ROUTE_G1, ROUTE_R0, ROUTE_R1 = range(6)


def _route_kernel(lg_ref, o_ref, cnt_ref, carry_ref):
    rows = lg_ref.shape[0]

    @pl.when(pl.program_id(0) == 0)
    def _():
        carry_ref[...] = jnp.zeros_like(carry_ref)

    lg = lg_ref[...]
    lane = lax.broadcasted_iota(I32, (rows, LANES), 1)
    big = jnp.int32(LANES)

    def first_argmax(v, vmax):
        return jnp.min(jnp.where(v == vmax, lane, big), axis=-1, keepdims=True)

    gl = jnp.where(lane < MOE_GROUPS, lg, NEG)
    gmax = jnp.max(gl, axis=-1, keepdims=True)
    gsum = jnp.sum(jnp.exp(gl - gmax), axis=-1, keepdims=True)
    g_sel = first_argmax(gl, gmax)
    g_p = 1.0 / gsum
    e_lo = MOE_GROUPS + g_sel * MOE_EPG
    el = jnp.where((lane >= e_lo) & (lane < e_lo + MOE_EPG), lg, NEG)
    emax = jnp.max(el, axis=-1, keepdims=True)
    esum = jnp.sum(jnp.exp(el - emax), axis=-1, keepdims=True)
    l0 = first_argmax(el, emax)
    el2 = jnp.where(lane == l0, NEG, el)
    emax2 = jnp.max(el2, axis=-1, keepdims=True)
    l1 = first_argmax(el2, emax2)
    p0 = 1.0 / esum
    p1 = jnp.exp(emax2 - emax) / esum
    gate0 = g_p * p0 / (p0 + p1)
    gate1 = g_p * p1 / (p0 + p1)
    e0 = l0 - MOE_GROUPS
    e1 = l1 - MOE_GROUPS
    oh0 = lane == e0
    oh1 = lane == e1
    hits = oh0.astype(F32) + oh1.astype(F32)
    ri = lax.broadcasted_iota(I32, (rows, rows), 0)
    ci = lax.broadcasted_iota(I32, (rows, rows), 1)
    before = (ci < ri).astype(BF16)
    prior = jnp.dot(before, hits.astype(BF16), preferred_element_type=F32) + carry_ref[0:1, :]
    r0 = jnp.sum(jnp.where(oh0, prior, 0.0), axis=-1, keepdims=True)
    r1 = jnp.sum(jnp.where(oh1, prior, 0.0), axis=-1, keepdims=True)
    carry = carry_ref[0:1, :] + jnp.sum(hits, axis=0, keepdims=True)
    carry_ref[0:1, :] = carry
    cnt_ref[...] = jnp.broadcast_to(carry, cnt_ref.shape)
    rec = jnp.where(lane == ROUTE_E0, e0.astype(F32), 0.0)
    rec = jnp.where(lane == ROUTE_E1, e1.astype(F32), rec)
    rec = jnp.where(lane == ROUTE_G0, gate0, rec)
    rec = jnp.where(lane == ROUTE_G1, gate1, rec)
    rec = jnp.where(lane == ROUTE_R0, r0, rec)
    rec = jnp.where(lane == ROUTE_R1, r1, rec)
    o_ref[...] = rec


def _route(logits):
    t = logits.shape[0]
    rows = 512
    return pl.pallas_call(
        _route_kernel,
        grid=(t // rows,),
        in_specs=[pl.BlockSpec((rows, LANES), lambda i: (i, 0))],
        out_specs=[pl.BlockSpec((rows, LANES), lambda i: (i, 0)), pl.BlockSpec((8, LANES), lambda i: (0, 0))],
        out_shape=[jax.ShapeDtypeStruct((t, LANES), F32), jax.ShapeDtypeStruct((8, LANES), F32)],
        scratch_shapes=[pltpu.VMEM((8, LANES), F32)],
        compiler_params=_cparams("arbitrary"),
        name="moe_route",
    )(logits)


def _gather_tiles(idx_smem, islot, src_hbm, dst_ref, sem, n):
    sub = dst_ref.shape[0] // n
    for r in range(n):
        off = pl.multiple_of(idx_smem[islot, r], sub)
        pltpu.make_async_copy(src_hbm.at[pl.ds(off, sub)], dst_ref.at[pl.ds(r * sub, sub)], sem).start()


def _whole_buffer_copy(src_hbm, buf, sem, n):
    del n
    return pltpu.make_async_copy(src_hbm.at[pl.ds(0, buf.shape[0])], buf, sem)


def _gather_step(step, nsteps, idx_hbm, idx_smem, idx_sem, src_hbm, bufs, row_sems, n):
    def idx_copy(s, slot):
        return pltpu.make_async_copy(idx_hbm.at[s], idx_smem.at[slot], idx_sem.at[slot])

    last = nsteps - 1

    @pl.when(step == 0)
    def _():
        idx_copy(0, 0).start()
        idx_copy(0, 0).wait()
        _gather_tiles(idx_smem, 0, src_hbm, bufs.at[0], row_sems.at[0], n)
        idx_copy(jnp.minimum(1, last), 1).start()

    nxt_i, nxt_b = (step + 1) % 3, (step + 1) % 2
    idx_copy(jnp.minimum(step + 1, last), nxt_i).wait()
    _gather_tiles(idx_smem, nxt_i, src_hbm, bufs.at[nxt_b], row_sems.at[nxt_b], n)
    idx_copy(jnp.minimum(step + 2, last), (step + 2) % 3).start()
    _whole_buffer_copy(src_hbm, bufs.at[step % 2], row_sems.at[step % 2], n).wait()


def _gather_drain(step, nsteps, idx_hbm, idx_smem, idx_sem, src_hbm, bufs, row_sems, n):
    @pl.when(step == nsteps - 1)
    def _():
        nxt_b, nxt_i = (step + 1) % 2, (step + 2) % 3
        _whole_buffer_copy(src_hbm, bufs.at[nxt_b], row_sems.at[nxt_b], n).wait()
        pltpu.make_async_copy(idx_hbm.at[0], idx_smem.at[nxt_i], idx_sem.at[nxt_i]).wait()


def _untile(buf, slot, first, n, sub):
    return jnp.concatenate([buf[slot, pl.ds(first * sub + c, n, stride=sub), :] for c in range(sub)], axis=1)


def _tile_store(ref, val, sub):
    n = val.shape[0]
    for c in range(sub):
        ref[pl.ds(c, n, stride=sub), :] = val[:, c * LANES:(c + 1) * LANES]


def _expert_kernel(bexp_ref, nused_ref, idx_hbm, tok_hbm, wg32_ref, wu32_ref, wd32_ref, y_ref,
                   idx_smem, xbuf, idx_sem, row_sems, wg_ref, wu_ref, wd_ref):
    b = pl.program_id(0)
    nused = nused_ref[0]

    @pl.when((b < nused) & ((b == 0) | (bexp_ref[b] != bexp_ref[jnp.maximum(b - 1, 0)])))
    def _():
        wg_ref[...] = wg32_ref[...].astype(BF16)
        wu_ref[...] = wu32_ref[...].astype(BF16)
        wd_ref[...] = wd32_ref[...].astype(BF16)

    @pl.when(b < nused)
    def _():
        rows = MOE_ROWS
        gather_args = (b, nused, idx_hbm, idx_smem, idx_sem, tok_hbm, xbuf, row_sems, rows)
        _gather_step(*gather_args)
        w = _untile(xbuf, b % 2, 0, rows, TOKEN_SUBROWS)
        x_lo = lax.bitcast_convert_type(w << 16, F32).astype(BF16)
        x_hi = lax.bitcast_convert_type(w & jnp.uint32(0xFFFF0000), F32).astype(BF16)
        half = D_MODEL // 2
        gate = (jnp.dot(x_lo, wg_ref[:half, :], preferred_element_type=F32)
                + jnp.dot(x_hi, wg_ref[half:, :], preferred_element_type=F32))
        up = (jnp.dot(x_lo, wu_ref[:half, :], preferred_element_type=F32)
              + jnp.dot(x_hi, wu_ref[half:, :], preferred_element_type=F32))
        hdn = (_silu(gate) * up).astype(BF16)
        y = jnp.dot(hdn, wd_ref[...], preferred_element_type=F32)
        y_ref[...] = y
        _gather_drain(*gather_args)

    @pl.when(b >= nused)
    def _():
        y_ref[...] = jnp.zeros_like(y_ref)


def _experts(block_exp, n_used, slot_tok, tokens_packed, wg, wu, wd, layer):
    n_blocks = slot_tok.shape[0]
    rows = MOE_ROWS
    half = D_MODEL // 2
    sub = TOKEN_SUBROWS
    grid_spec = pltpu.PrefetchScalarGridSpec(
        num_scalar_prefetch=2,
        grid=(n_blocks,),
        in_specs=[
            pl.BlockSpec(memory_space=pl.ANY),
            pl.BlockSpec(memory_space=pl.ANY),
            pl.BlockSpec((None, None, D_MODEL, MOE_D_FF), lambda b, be, nu: (layer, be[b], 0, 0)),
            pl.BlockSpec((None, None, D_MODEL, MOE_D_FF), lambda b, be, nu: (layer, be[b], 0, 0)),
            pl.BlockSpec((None, None, MOE_D_FF, D_MODEL), lambda b, be, nu: (layer, be[b], 0, 0)),
        ],
        out_specs=pl.BlockSpec((rows, D_MODEL), lambda b, be, nu: (b, 0)),
        scratch_shapes=[
            pltpu.SMEM((3, rows), I32),
            pltpu.VMEM((2, rows * sub, half // sub), U32),
            pltpu.SemaphoreType.DMA((3,)),
            pltpu.SemaphoreType.DMA((2,)),
            pltpu.VMEM((D_MODEL, MOE_D_FF), BF16),
            pltpu.VMEM((D_MODEL, MOE_D_FF), BF16),
            pltpu.VMEM((MOE_D_FF, D_MODEL), BF16),
        ],
    )
    return pl.pallas_call(
        _expert_kernel,
        grid_spec=grid_spec,
        out_shape=jax.ShapeDtypeStruct((n_blocks * rows, D_MODEL), F32),
        compiler_params=_cparams("arbitrary"),
        name="moe_experts",
    )(block_exp, n_used, slot_tok, tokens_packed, wg, wu, wd)


def _combine_kernel(idx_hbm, ys_hbm, x_ref, mod_ref, rt_ref, o_ref, idx_smem, ybuf, idx_sem, row_sems):
    i = pl.program_id(0)
    n = COMBINE_ROWS
    gather_args = (i, pl.num_programs(0), idx_hbm, idx_smem, idx_sem, ys_hbm, ybuf, row_sems, 2 * n)
    _gather_step(*gather_args)
    y0 = ybuf[i % 2, pl.ds(0, n), :]
    y1 = ybuf[i % 2, pl.ds(n, n), :]
    rt = rt_ref[...]
    moe = rt[:, ROUTE_G0:ROUTE_G0 + 1] * y0 + rt[:, ROUTE_G1:ROUTE_G1 + 1] * y1
    o_ref[...] = x_ref[...] + mod_ref[5:6, :] * moe
    _gather_drain(*gather_args)


def _combine(dest_tiles, ys, xa, modv, route, geom, rows_out):
    d = xa.shape[1]
    t = rows_out
    n = COMBINE_ROWS
    tpb, nlt = geom["s"] // n, geom["n_lat"] // n
    return pl.pallas_call(
        _combine_kernel,
        grid=(t // n,),
        in_specs=[
            pl.BlockSpec(memory_space=pl.ANY),
            pl.BlockSpec(memory_space=pl.ANY),
            pl.BlockSpec((n, d), lambda i: (i, 0)),
            pl.BlockSpec((None, 8, d), lambda i: (_variant_of_tile(i, tpb, nlt), 0, 0)),
            pl.BlockSpec((n, LANES), lambda i: (i, 0)),
        ],
        out_specs=pl.BlockSpec((n, d), lambda i: (i, 0)),
        out_shape=jax.ShapeDtypeStruct((t, d), F32),
        scratch_shapes=[
            pltpu.SMEM((3, 2 * n), I32),
            pltpu.VMEM((2, 2 * n, d), F32),
            pltpu.SemaphoreType.DMA((3,)),
            pltpu.SemaphoreType.DMA((2,)),
        ],
        compiler_params=_cparams("arbitrary"),
        name="moe_combine_residual",
    )(dest_tiles, ys, xa, modv, route)


def _moe(xa, tokens_packed, logits, modv, wg, wu, wd, layer, geom, rows_out):
    t = xa.shape[0]
    rows = MOE_ROWS
    route, counts = _route(logits)
    cnt = counts[0, :MOE_EXPERTS].astype(I32)
    padded = (cnt + rows - 1) // rows * rows
    pad_end = jnp.cumsum(padded)
    pad_start = pad_end - padded
    n_blocks = (2 * t + MOE_EXPERTS * (rows - 1) + rows - 1) // rows
    eid = route[:, ROUTE_E0:ROUTE_E1 + 1].astype(I32)
    rank = route[:, ROUTE_R0:ROUTE_R1 + 1].astype(I32)
    dest = pad_start[eid] + rank
    tok = jnp.broadcast_to(jnp.arange(t, dtype=I32)[:, None], (t, 2))
    slot_tok = jnp.zeros((n_blocks * rows,), I32).at[dest.reshape(-1)].set(
        tok.reshape(-1), unique_indices=True)
    block_start = jnp.arange(n_blocks, dtype=I32) * rows
    block_exp = jnp.minimum(
        jnp.sum((pad_end[None, :] <= block_start[:, None]).astype(I32), axis=1), MOE_EXPERTS - 1)
    n_used = (pad_end[-1:] // rows).astype(I32)
    ys = _experts(block_exp, n_used, slot_tok.reshape(n_blocks, rows) * TOKEN_SUBROWS, tokens_packed, wg, wu, wd,
                  layer)
    n = COMBINE_ROWS
    dest_tiles = dest.reshape(t // n, n, 2).transpose(0, 2, 1).reshape(t // n, 2 * n)
    return _combine(dest_tiles, ys, xa, modv, route, geom, rows_out)


def kernel(x, c, ctx, c_ctx, ada_w, ada_b, norm_g, ssm_w_in, ssm_conv_w, ssm_conv_b, ssm_a_log, ssm_dt_bias, ssm_d, ssm_norm_g, ssm_w_out, win_w_qkv, win_q_g, win_k_g, win_sinks, win_w_out, diff_w_qkv, diff_q_g, diff_k_g, diff_lam, diff_subln_g, diff_w_out, moe_w_group, moe_b_group, moe_w_expert, moe_b_expert, moe_w_gate, moe_w_up, moe_w_down):
    bsz, s, d = x.shape
    n_ctx = ctx.shape[1]
    depth = ada_w.shape[0]
    n_lat = bsz * s
    t = n_lat + bsz * n_ctx
    geom = {"b": bsz, "s": s, "c": n_ctx, "n_lat": n_lat, "t": t}
    assert d == D_MODEL and 1 + bsz <= 8
    assert s % max(ROW_TILE, 2048 if s >= 2048 else ROW_TILE) == 0 and (bsz * n_ctx) % ROW_TILE == 0
    assert n_ctx % 256 == 0 and s % GRID_W == 0

    xa = jnp.concatenate([x.reshape(n_lat, d), ctx.reshape(bsz * n_ctx, d)], axis=0)
    cvecs = jnp.zeros((8, d), F32).at[0].set(c_ctx).at[1:1 + bsz].set(c)
    mods = _modulation(cvecs, ada_w, ada_b)
    mods = mods.reshape(depth, 8, 6, d)
    mods = jnp.concatenate([mods, jnp.zeros((depth, 8, 2, d), F32)], axis=2)
    cos_t, sin_t = _rope_tables(geom)
    scale = HEAD_DIM ** -0.5

    for i in range(depth):
        kind, j = i % N_MIXERS, i // N_MIXERS
        modv = mods[i]
        r_w = jnp.zeros((d, LANES), F32).at[:, :MOE_GROUPS].set(moe_w_group[i])
        r_w = r_w.at[:, MOE_GROUPS:MOE_GROUPS + MOE_EXPERTS].set(moe_w_expert[i])
        r_hi = r_w.astype(BF16)
        r_lo = (r_w - r_hi.astype(F32)).astype(BF16)
        r_b = jnp.zeros((1, LANES), F32).at[0, :MOE_GROUPS].set(moe_b_group[i])
        r_b = r_b.at[0, MOE_GROUPS:MOE_GROUPS + MOE_EXPERTS].set(moe_b_expert[i])
        router = (jnp.concatenate([r_hi, r_lo], axis=1), r_b)

        if kind == 0:
            w_in = ssm_w_in[j].astype(BF16)
            n_main = SSM_D_INNER + SSM_CONV_DIM
            zx, dt_raw = _inproj(xa, modv, norm_g[i, 0], w_in[:, :n_main], w_in[:, n_main:], geom)
            xbc = _ssd_conv(zx, ssm_conv_w[j], ssm_conv_b[j], geom)
            a = -jnp.exp(ssm_a_log[j].astype(F32))
            ys_dir = _ssd_scan(xbc, dt_raw, a, ssm_dt_bias[j].astype(F32), geom)
            dsk = jnp.repeat(ssm_d[j].astype(F32), SSM_HEAD_DIM).reshape(1, SSM_D_INNER)
            a_args = (xbc, ys_dir[0], ys_dir[1], zx, dsk, ssm_norm_g[j].reshape(1, SSM_D_INNER))
            xa, tok_p, logits = _outproj(a_args, ssm_w_out[j].astype(BF16), xa, modv, norm_g[i, 1],
                                         *router, geom, ssd=True)
        elif kind == 1:
            qw, kvw = WIN_HEADS * HEAD_DIM, WIN_KV_HEADS * HEAD_DIM
            qkv = _inproj(xa, modv, norm_g[i, 0], win_w_qkv[j].astype(BF16), None, geom)
            gains = jnp.concatenate([jnp.tile(win_q_g[j], WIN_HEADS), jnp.tile(win_k_g[j], WIN_KV_HEADS)])
            post = jnp.concatenate([jnp.full((qw,), scale, F32), jnp.ones((kvw,), F32)])
            qk, _ = _qk_prep(qkv, gains.reshape(1, -1), post.reshape(1, -1), cos_t, sin_t)
            o = _win_attn(qk, qkv, win_sinks[j].astype(F32), geom)
            xa, tok_p, logits = _outproj((o,), win_w_out[j].astype(BF16), xa, modv, norm_g[i, 1],
                                         *router, geom, ssd=False)
        else:
            lam_init = 0.8 - 0.6 * math.exp(-0.3 * i)
            hw = 2 * DIFF_HEADS * HEAD_DIM
            qkv = _inproj(xa, modv, norm_g[i, 0], diff_w_qkv[j].astype(BF16), None, geom)
            gains = jnp.concatenate([jnp.tile(diff_q_g[j], 2 * DIFF_HEADS), jnp.tile(diff_k_g[j], 2 * DIFF_HEADS)])
            post = jnp.concatenate([jnp.full((hw,), scale * math.log2(math.e), F32), jnp.ones((hw,), F32)])
            qk, norms = _qk_prep(qkv, gains.reshape(1, -1), post.reshape(1, -1), cos_t, sin_t)
            o = _diff_attn(qk, qkv, norms, diff_lam[j].astype(F32), diff_subln_g[j].astype(F32), lam_init, geom)
            xa, tok_p, logits = _outproj((o,), diff_w_out[j].astype(BF16), xa, modv, norm_g[i, 1],
                                         *router, geom, ssd=False)

        rows_out = n_lat if i == depth - 1 else t
        xa = _moe(xa, tok_p, logits, modv, moe_w_gate, moe_w_up, moe_w_down, i, geom, rows_out)

    return xa.reshape(bsz, s, d)
```

```python
import functools
import math

import jax
import jax.numpy as jnp
from jax import lax
from jax.experimental import pallas as pl
from jax.experimental.pallas import tpu as pltpu

F32 = jnp.float32
BF16 = jnp.bfloat16
I32 = jnp.int32
U32 = jnp.uint32

D_MODEL = 2048
GRID_W = 64
NORM_EPS = 1e-6
ROPE_THETA = 10000.0
N_MIXERS = 3

SSM_D_INNER = 2 * D_MODEL
SSM_HEAD_DIM = 64
SSM_HEADS = SSM_D_INNER // SSM_HEAD_DIM
SSM_STATE = 128
SSM_GROUPS = 8
SSM_CHUNK = 128
SSM_CONV_DIM = SSM_D_INNER + 2 * SSM_GROUPS * SSM_STATE
SSM_GROUP_W = SSM_D_INNER // SSM_GROUPS

HEAD_DIM = 128
WIN_HEADS = D_MODEL // HEAD_DIM
WIN_KV_HEADS = 4
WINDOW = 128
ATT_BLOCK = 128
DIFF_HEADS = D_MODEL // (2 * HEAD_DIM)

MOE_GROUPS = 4
MOE_EPG = 8
MOE_EXPERTS = MOE_GROUPS * MOE_EPG
MOE_D_FF = D_MODEL // 4

LANES = 128
NEG = -1e30
VMEM_LIMIT = 48 * 1024 * 1024

ROW_TILE = 512
MOE_ROWS = 256
COMBINE_ROWS = 256
TOKEN_SUBROWS = D_MODEL // 2 // LANES


def _cparams(*sem):
    return pltpu.CompilerParams(dimension_semantics=sem, vmem_limit_bytes=VMEM_LIMIT)


def _variant_of_tile(i, tiles_per_batch, n_latent_tiles):
    return jnp.where(i < n_latent_tiles, 1 + i // tiles_per_batch, 0)


def _silu(v):
    return v * jax.nn.sigmoid(v)


def _split3(v):
    hi = v.astype(BF16)
    r1 = v - hi.astype(F32)
    mid = r1.astype(BF16)
    lo = (r1 - mid.astype(F32)).astype(BF16)
    return hi, mid, lo


def _mod_kernel(c_ref, w_ref, b_ref, o_ref):
    s = _silu(c_ref[...]).astype(BF16)
    o_ref[...] = jnp.dot(s, w_ref[...].astype(BF16), preferred_element_type=F32) + b_ref[...]


def _modulation(cvecs, ada_w, ada_b):
    depth, d, n6 = ada_w.shape
    tn = 1024
    return pl.pallas_call(
        _mod_kernel,
        grid=(depth, n6 // tn),
        in_specs=[
            pl.BlockSpec((8, d), lambda l, j: (0, 0)),
            pl.BlockSpec((None, d, tn), lambda l, j: (l, 0, j)),
            pl.BlockSpec((None, 1, tn), lambda l, j: (l, 0, j)),
        ],
        out_specs=pl.BlockSpec((None, 8, tn), lambda l, j: (l, 0, j)),
        out_shape=jax.ShapeDtypeStruct((depth, 8, n6), F32),
        compiler_params=_cparams("parallel", "parallel"),
        name="adaln_modulation",
    )(cvecs, ada_w, ada_b.reshape(depth, 1, n6))


def _inproj_kernel(x_ref, mod_ref, g_ref, w_ref, *rest, has_tail):
    if has_tail:
        wt_ref, o_ref, ot_ref, h_ref = rest
    else:
        o_ref, h_ref = rest

    @pl.when(pl.program_id(1) == 0)
    def _():
        x = x_ref[...]
        ms = jnp.mean(x * x, axis=-1, keepdims=True)
        y = x * lax.rsqrt(ms + NORM_EPS) * g_ref[...]
        h = (y * (1.0 + mod_ref[1:2, :]) + mod_ref[0:1, :]).astype(BF16)
        h_ref[...] = h
        if has_tail:
            ot_ref[...] = jnp.dot(h, wt_ref[...], preferred_element_type=F32)

    o_ref[...] = jnp.dot(h_ref[...], w_ref[...], preferred_element_type=F32).astype(o_ref.dtype)


def _inproj(xa, modv, gain, w, w_tail, geom):
    t, d = xa.shape
    n = w.shape[1]
    tm = ROW_TILE
    tn = 2048 if n % 2048 == 0 else 1024
    tpb, nlt = geom["s"] // tm, geom["n_lat"] // tm
    var = lambda i, j: (_variant_of_tile(i, tpb, nlt), 0, 0)
    in_specs = [
        pl.BlockSpec((tm, d), lambda i, j: (i, 0)),
        pl.BlockSpec((None, 8, d), var),
        pl.BlockSpec((1, d), lambda i, j: (0, 0)),
        pl.BlockSpec((d, tn), lambda i, j: (0, j)),
    ]
    out_specs = [pl.BlockSpec((tm, tn), lambda i, j: (i, j))]
    out_shape = [jax.ShapeDtypeStruct((t, n), BF16)]
    args = [xa, modv, gain.reshape(1, d), w]
    if w_tail is not None:
        nt = w_tail.shape[1]
        in_specs.append(pl.BlockSpec((d, nt), lambda i, j: (0, 0)))
        out_specs.append(pl.BlockSpec((tm, nt), lambda i, j: (i, 0)))
        out_shape.append(jax.ShapeDtypeStruct((t, nt), F32))
        args.append(w_tail)
    res = pl.pallas_call(
        functools.partial(_inproj_kernel, has_tail=w_tail is not None),
        grid=(t // tm, n // tn),
        in_specs=in_specs,
        out_specs=out_specs,
        out_shape=out_shape,
        scratch_shapes=[pltpu.VMEM((tm, d), BF16)],
        compiler_params=_cparams("parallel", "arbitrary"),
        name="norm_mod_inproj",
    )(*args)
    return res if w_tail is not None else res[0]


def _conv_kernel(xp_ref, x_ref, xn_ref, w_ref, b_ref, o_ref, *, rows, tiles_lat, n_lat_tiles, tiles_ctx):
    i = pl.program_id(0)
    in_lat = i < n_lat_tiles
    k = jnp.where(in_lat, i % tiles_lat, (i - n_lat_tiles) % tiles_ctx)
    n = jnp.where(in_lat, tiles_lat, tiles_ctx)
    xb = x_ref[...]
    x = xb.astype(F32)
    prev_row = jnp.where(k == 0, 0.0, xp_ref[...].astype(F32)[15:16, :])
    next_row = jnp.where(k == n - 1, 0.0, xn_ref[...].astype(F32)[0:1, :])
    r = lax.broadcasted_iota(I32, (rows, 1), 0)
    ri = lax.broadcasted_iota(I32, (rows, rows), 0)
    ci = lax.broadcasted_iota(I32, (rows, rows), 1)
    shift_dn = (ri == ci + 1).astype(BF16)
    shift_up = (ri + 1 == ci).astype(BF16)
    xm1 = jnp.where(r == 0, prev_row, jnp.dot(shift_dn, xb, preferred_element_type=F32))
    xp1 = jnp.where(r == rows - 1, next_row, jnp.dot(shift_up, xb, preferred_element_type=F32))
    out = xm1 * w_ref[0:1, :] + x * w_ref[1:2, :] + xp1 * w_ref[2:3, :] + b_ref[...]
    o_ref[...] = _silu(out).astype(o_ref.dtype)


def _ssd_conv(zx, conv_w, conv_b, geom):
    t = zx.shape[0]
    rows, wc = 256, 2048
    col0 = SSM_D_INNER // wc
    halo = 16
    rb = rows // halo
    last_halo = t // halo - 1
    kern = functools.partial(_conv_kernel, rows=rows, tiles_lat=geom["s"] // rows,
                             n_lat_tiles=geom["n_lat"] // rows, tiles_ctx=geom["c"] // rows)
    return pl.pallas_call(
        kern,
        grid=(t // rows, SSM_CONV_DIM // wc),
        in_specs=[
            pl.BlockSpec((halo, wc), lambda i, j: (jnp.maximum(i * rb - 1, 0), col0 + j)),
            pl.BlockSpec((rows, wc), lambda i, j: (i, col0 + j)),
            pl.BlockSpec((halo, wc), lambda i, j: (jnp.minimum((i + 1) * rb, last_halo), col0 + j)),
            pl.BlockSpec((3, wc), lambda i, j: (0, j)),
            pl.BlockSpec((1, wc), lambda i, j: (0, j)),
        ],
        out_specs=pl.BlockSpec((rows, wc), lambda i, j: (i, j)),
        out_shape=jax.ShapeDtypeStruct((t, SSM_CONV_DIM), BF16),
        compiler_params=_cparams("parallel", "parallel"),
        name="ssd_conv_silu",
    )(zx, zx, zx, conv_w, conv_b.reshape(1, SSM_CONV_DIM))


def _ssd_scan_kernel(xf_ref, bf_ref, cf_ref, dtf_ref, xr_ref, br_ref, cr_ref, dtr_ref, a_ref, dtb_ref, e_ref,
                     yf_ref, yr_ref, state_ref):
    @pl.when(pl.program_id(1) == 0)
    def _():
        state_ref[...] = jnp.zeros_like(state_ref)

    _ssd_chunk(xf_ref, bf_ref, cf_ref, dtf_ref, a_ref.at[0:1], dtb_ref.at[0:1], e_ref, yf_ref, state_ref.at[0],
               reverse=False)
    _ssd_chunk(xr_ref, br_ref, cr_ref, dtr_ref, a_ref.at[1:2], dtb_ref.at[1:2], e_ref, yr_ref, state_ref.at[1],
               reverse=True)


def _ssd_chunk(xs_ref, b_ref, c_ref, dt_ref, a_ref, dtb_ref, e_ref, y_ref, state_ref, *, reverse):
    q = SSM_CHUNK
    hpg = SSM_HEADS // SSM_GROUPS
    c0 = SSM_HEADS if reverse else 0
    pre = dt_ref[:, c0:c0 + SSM_HEADS] + dtb_ref[...]
    dt = jnp.maximum(pre, 0.0) + jnp.log1p(jnp.exp(-jnp.abs(pre)))
    la = dt * a_ref[...]
    ri = lax.broadcasted_iota(I32, (q, q), 0)
    ci = lax.broadcasted_iota(I32, (q, q), 1)
    tri = (ri <= ci) if reverse else (ri >= ci)
    trib = tri.astype(BF16)
    hi, mid, lo = _split3(la)
    cum = (jnp.dot(trib, hi, preferred_element_type=F32) + jnp.dot(trib, mid, preferred_element_type=F32)
           + jnp.dot(trib, lo, preferred_element_type=F32))
    total = cum[0:1, :] if reverse else cum[q - 1:q, :]
    cum_t = cum.T
    dt_t = dt.T
    lane = lax.broadcasted_iota(I32, (1, LANES), 1)
    first_half = lane < SSM_HEAD_DIM

    def per_head_to_columns(v, with_lo=True):
        hi = v.astype(BF16)
        out = jnp.dot(hi, e_ref[...], preferred_element_type=F32)
        if with_lo:
            lo = (v - hi.astype(F32)).astype(BF16)
            out = out + jnp.dot(lo, e_ref[...], preferred_element_type=F32)
        return out

    exp_cum_cols = per_head_to_columns(jnp.exp(cum))
    w_state_cols = per_head_to_columns(jnp.exp(total - cum) * dt, with_lo=False)
    exp_total_cols = per_head_to_columns(jnp.broadcast_to(jnp.exp(total), (8, SSM_HEADS)))[0:1, :]

    for g in range(SSM_GROUPS):
        bg = b_ref[:, g * SSM_STATE:(g + 1) * SSM_STATE]
        cg = c_ref[:, g * SSM_STATE:(g + 1) * SSM_STATE]
        cb = lax.dot_general(cg, bg, (((1,), (1,)), ((), ())), preferred_element_type=F32)
        st = state_ref[g]
        y_state = jnp.dot(cg, st.astype(BF16), preferred_element_type=F32)
        xw_parts = []
        for pair in range(hpg // 2):
            h0 = g * hpg + 2 * pair
            col = g * SSM_GROUP_W + pair * LANES
            xpair = xs_ref[:, col:col + LANES]
            ws = []
            for h in (h0, h0 + 1):
                seg = cum[:, h:h + 1] - cum_t[h:h + 1, :]
                dec = jnp.exp(jnp.where(tri, seg, NEG))
                ws.append((cb * dec * dt_t[h:h + 1, :]).astype(BF16))
            zero = jnp.zeros_like(xpair)
            x_diag = jnp.concatenate([jnp.where(first_half, xpair, zero), jnp.where(first_half, zero, xpair)],
                                     axis=0)
            y_intra = jnp.dot(jnp.concatenate(ws, axis=1), x_diag, preferred_element_type=F32)
            ysp = y_state[:, pair * LANES:(pair + 1) * LANES]
            y_ref[:, col:col + LANES] = (y_intra + ysp * exp_cum_cols[:, col:col + LANES]).astype(y_ref.dtype)
            xw_parts.append((xpair.astype(F32) * w_state_cols[:, col:col + LANES]).astype(BF16))
        xw = jnp.concatenate(xw_parts, axis=1)
        scale_row = exp_total_cols[:, g * SSM_GROUP_W:(g + 1) * SSM_GROUP_W]
        upd = lax.dot_general(bg, xw, (((0,), (0,)), ((), ())), preferred_element_type=F32)
        state_ref[g] = st * scale_row + upd


def _ssd_scan(xbc, dt_raw, a, dt_bias, geom):
    t = xbc.shape[0]
    q = SSM_CHUNK
    bsz = geom["b"]
    cq, lq = geom["c"] // q, geom["s"] // q
    lat_blocks = bsz * lq

    def row_block(reverse):
        def f(b, s):
            cchunk = (cq - 1 - s) if reverse else s
            lchunk = (lq - 1 - (s - cq)) if reverse else (s - cq)
            return jnp.where(s < cq, lat_blocks + b * cq + cchunk, b * lq + lchunk)
        return f

    bc_w = SSM_GROUPS * SSM_STATE
    bcol = SSM_D_INNER // bc_w
    head_of_col = jnp.arange(SSM_D_INNER, dtype=I32) // SSM_HEAD_DIM
    expand = (head_of_col[None, :] == jnp.arange(SSM_HEADS, dtype=I32)[:, None]).astype(BF16)

    def chunk_specs(reverse):
        rb = row_block(reverse)
        return [
            pl.BlockSpec((q, SSM_D_INNER), lambda b, s: (rb(b, s), 0)),
            pl.BlockSpec((q, bc_w), lambda b, s: (rb(b, s), bcol)),
            pl.BlockSpec((q, bc_w), lambda b, s: (rb(b, s), bcol + 1)),
            pl.BlockSpec((q, 2 * SSM_HEADS), lambda b, s: (rb(b, s), 0)),
        ]

    const = lambda b, s: (0, 0)
    return pl.pallas_call(
        _ssd_scan_kernel,
        grid=(bsz, cq + lq),
        in_specs=chunk_specs(False) + chunk_specs(True) + [
            pl.BlockSpec((2, SSM_HEADS), const),
            pl.BlockSpec((2, SSM_HEADS), const),
            pl.BlockSpec((SSM_HEADS, SSM_D_INNER), const),
        ],
        out_specs=[pl.BlockSpec((q, SSM_D_INNER), lambda b, s: (row_block(False)(b, s), 0)),
                   pl.BlockSpec((q, SSM_D_INNER), lambda b, s: (row_block(True)(b, s), 0))],
        out_shape=[jax.ShapeDtypeStruct((t, SSM_D_INNER), BF16)] * 2,
        scratch_shapes=[pltpu.VMEM((2, SSM_GROUPS, SSM_STATE, SSM_GROUP_W), F32)],
        compiler_params=_cparams("parallel", "arbitrary"),
        name="ssd_scan",
    )(xbc, xbc, xbc, dt_raw, xbc, xbc, xbc, dt_raw, a, dt_bias, expand)


def _qk_prep_kernel(x_ref, g_ref, s_ref, cos_ref, sin_ref, o_ref, n_ref, *, heads):
    j = pl.program_id(1)
    cosv = cos_ref[...]
    sinv = sin_ref[...]
    lane = lax.broadcasted_iota(I32, (1, HEAD_DIM), 1)
    low = (lane % (HEAD_DIM // 2)) < (HEAD_DIM // 4)

    @pl.when(j == 0)
    def _():
        n_ref[...] = jnp.zeros_like(n_ref)

    sq_norms = n_ref[...]
    for h in range(heads):
        sl = slice(h * HEAD_DIM, (h + 1) * HEAD_DIM)
        x = x_ref[:, sl].astype(F32)
        ms = jnp.mean(x * x, axis=-1, keepdims=True)
        y = x * lax.rsqrt(ms + NORM_EPS) * g_ref[:, sl]
        partner = jnp.where(low, pltpu.roll(y, HEAD_DIM - HEAD_DIM // 4, 1), pltpu.roll(y, HEAD_DIM // 4, 1))
        ob = ((y * cosv + partner * sinv) * s_ref[:, sl]).astype(o_ref.dtype)
        o_ref[:, sl] = ob
        of = ob.astype(F32)
        sq_norms = jnp.where(lane == j * heads + h, jnp.sum(of * of, axis=-1, keepdims=True), sq_norms)
    n_ref[...] = sq_norms


def _qk_prep(qkv, gains, post_scale, cos_t, sin_t):
    t = qkv.shape[0]
    n = gains.shape[1]
    rows, wb = 512, 512
    return pl.pallas_call(
        functools.partial(_qk_prep_kernel, heads=wb // HEAD_DIM),
        grid=(t // rows, n // wb),
        in_specs=[
            pl.BlockSpec((rows, wb), lambda i, j: (i, j)),
            pl.BlockSpec((1, wb), lambda i, j: (0, j)),
            pl.BlockSpec((1, wb), lambda i, j: (0, j)),
            pl.BlockSpec((rows, HEAD_DIM), lambda i, j: (i, 0)),
            pl.BlockSpec((rows, HEAD_DIM), lambda i, j: (i, 0)),
        ],
        out_specs=[pl.BlockSpec((rows, wb), lambda i, j: (i, j)),
                   pl.BlockSpec((rows, HEAD_DIM), lambda i, j: (i, 0))],
        out_shape=[jax.ShapeDtypeStruct((t, n), BF16), jax.ShapeDtypeStruct((t, HEAD_DIM), F32)],
        compiler_params=_cparams("parallel", "arbitrary"),
        name="qk_norm_rope",
    )(qkv, gains, post_scale, cos_t, sin_t)


def _rope_tables(geom):
    s, t = geom["s"], geom["t"]
    pos = jnp.arange(s)
    row = (pos // GRID_W).astype(F32)
    col = (pos % GRID_W).astype(F32)
    quarter = HEAD_DIM // 4
    inv_freq = ROPE_THETA ** (-jnp.arange(quarter, dtype=F32) / quarter)
    ar = row[:, None] * inv_freq
    ac = col[:, None] * inv_freq
    cos_l = jnp.concatenate([jnp.cos(ar), jnp.cos(ar), jnp.cos(ac), jnp.cos(ac)], axis=1)
    sin_l = jnp.concatenate([-jnp.sin(ar), jnp.sin(ar), -jnp.sin(ac), jnp.sin(ac)], axis=1)
    n_ctx_rows = t - geom["n_lat"]
    cos_t = jnp.concatenate([jnp.tile(cos_l, (geom["b"], 1)), jnp.ones((n_ctx_rows, HEAD_DIM), F32)], axis=0)
    sin_t = jnp.concatenate([jnp.tile(sin_l, (geom["b"], 1)), jnp.zeros((n_ctx_rows, HEAD_DIM), F32)], axis=0)
    return cos_t, sin_t


def _win_attn_kernel(sink_ref, q_ref, kp_ref, kc_ref, kn_ref, vp_ref, vc_ref, vn_ref, kx_ref, vx_ref, o_ref,
                     *, n_lat_blocks, ctx_blocks):
    s = pl.program_id(1)
    grp = WIN_HEADS // WIN_KV_HEADS
    blk = ATT_BLOCK
    is_lat = s >= ctx_blocks
    n = s - ctx_blocks
    rows = grp * blk
    qi = lax.broadcasted_iota(I32, (rows, 3 * blk), 0) % blk
    kk = lax.broadcasted_iota(I32, (rows, 3 * blk), 1)
    rel = kk - blk - qi
    in_band = (rel <= WINDOW) & (rel >= -WINDOW)
    lo = jnp.where(n > 0, 0, blk)
    hi = jnp.where(is_lat, jnp.where(n < n_lat_blocks - 1, 3 * blk, 2 * blk), 0)
    mask = in_band & (kk >= lo) & (kk < hi)
    rowi = lax.broadcasted_iota(I32, (rows, 1), 0)
    nt = (((1,), (1,)), ((), ()))
    for kh in range(WIN_KV_HEADS):
        ks = slice(kh * HEAD_DIM, (kh + 1) * HEAD_DIM)
        k_lat = jnp.concatenate([kp_ref[:, ks], kc_ref[:, ks], kn_ref[:, ks]], axis=0)
        v_lat = jnp.concatenate([vp_ref[:, ks], vc_ref[:, ks], vn_ref[:, ks]], axis=0)
        qg = jnp.concatenate(
            [q_ref[:, (kh * grp + j) * HEAD_DIM:(kh * grp + j + 1) * HEAD_DIM] for j in range(grp)], axis=0)
        s_lat = jnp.where(mask, lax.dot_general(qg, k_lat, nt, preferred_element_type=F32), NEG)
        s_ctx = lax.dot_general(qg, kx_ref[:, ks], nt, preferred_element_type=F32)
        sink = jnp.full((rows, 1), sink_ref[kh * grp], F32)
        for j in range(1, grp):
            sink = jnp.where(rowi >= j * blk, sink_ref[kh * grp + j], sink)
        m = jnp.maximum(jnp.maximum(jnp.max(s_lat, axis=-1, keepdims=True),
                                    jnp.max(s_ctx, axis=-1, keepdims=True)), sink)
        p_lat = jnp.exp(s_lat - m)
        p_ctx = jnp.exp(s_ctx - m)
        denom = (jnp.sum(p_lat, axis=-1, keepdims=True) + jnp.sum(p_ctx, axis=-1, keepdims=True)
                 + jnp.exp(sink - m))
        o = (jnp.dot(p_lat.astype(BF16), v_lat, preferred_element_type=F32)
             + jnp.dot(p_ctx.astype(BF16), vx_ref[:, ks], preferred_element_type=F32)) / denom
        for j in range(grp):
            hq = kh * grp + j
            o_ref[:, hq * HEAD_DIM:(hq + 1) * HEAD_DIM] = o[j * blk:(j + 1) * blk].astype(o_ref.dtype)


def _win_attn(qk, qkv, sinks, geom):
    t = qk.shape[0]
    blk = ATT_BLOCK
    bsz, c = geom["b"], geom["c"]
    nb = geom["s"] // blk
    cb = c // blk
    lat_blocks = bsz * nb
    kvw = WIN_KV_HEADS * HEAD_DIM
    qw = WIN_HEADS * HEAD_DIM
    kcol = qw // kvw
    vcol = (qw + kvw) // kvw

    def qrow(b, s, sk):
        return jnp.where(s < cb, lat_blocks + b * cb + s, b * nb + (s - cb))

    def lat(off):
        def f(b, s, sk):
            n = jnp.clip(s - cb + off, 0, nb - 1)
            return b * nb + n
        return f

    ctx_row = lambda b, s, sk: (geom["n_lat"] // c + b)
    grid_spec = pltpu.PrefetchScalarGridSpec(
        num_scalar_prefetch=1,
        grid=(bsz, cb + nb),
        in_specs=[
            pl.BlockSpec((blk, qw), lambda b, s, sk: (qrow(b, s, sk), 0)),
            pl.BlockSpec((blk, kvw), lambda b, s, sk: (lat(-1)(b, s, sk), kcol)),
            pl.BlockSpec((blk, kvw), lambda b, s, sk: (lat(0)(b, s, sk), kcol)),
            pl.BlockSpec((blk, kvw), lambda b, s, sk: (lat(1)(b, s, sk), kcol)),
            pl.BlockSpec((blk, kvw), lambda b, s, sk: (lat(-1)(b, s, sk), vcol)),
            pl.BlockSpec((blk, kvw), lambda b, s, sk: (lat(0)(b, s, sk), vcol)),
            pl.BlockSpec((blk, kvw), lambda b, s, sk: (lat(1)(b, s, sk), vcol)),
            pl.BlockSpec((c, kvw), lambda b, s, sk: (ctx_row(b, s, sk), kcol)),
            pl.BlockSpec((c, kvw), lambda b, s, sk: (ctx_row(b, s, sk), vcol)),
        ],
        out_specs=pl.BlockSpec((blk, qw), lambda b, s, sk: (qrow(b, s, sk), 0)),
    )
    return pl.pallas_call(
        functools.partial(_win_attn_kernel, n_lat_blocks=nb, ctx_blocks=cb),
        grid_spec=grid_spec,
        out_shape=jax.ShapeDtypeStruct((t, qw), BF16),
        compiler_params=_cparams("parallel", "parallel"),
        name="window_gqa",
    )(sinks, qk, qk, qk, qk, qkv, qkv, qkv, qk, qkv)


DIFF_SAFE_EXCESS = 96.0


def _diff_attn_kernel(*refs, use_latent, lam_init, sub):
    if use_latent:
        (kmax_ref, q_ref, kx_ref, vx_ref, k_ref, v_ref, qn_ref, lam_ref, g_ref, o_ref,
         m_ref, l_ref, acc_ref, stat_ref) = refs
    else:
        q_ref, kx_ref, vx_ref, lam_ref, g_ref, _, o_ref, m_ref, l_ref, acc_ref = refs
    j = pl.program_id(3)
    nt = (((1,), (1,)), ((), ()))
    hd = HEAD_DIM

    def update(idx, qh, kh, v):
        s = lax.dot_general(qh, kh, nt, preferred_element_type=F32)
        m_old = m_ref[idx]
        m_new = jnp.maximum(m_old, jnp.max(s, axis=-1, keepdims=True))
        alpha = jnp.exp2(m_old - m_new)
        p = jnp.exp2(s - m_new)
        l_ref[idx] = alpha * l_ref[idx] + jnp.sum(p, axis=-1, keepdims=True)
        acc_ref[idx] = alpha * acc_ref[idx] + jnp.dot(p.astype(BF16), v, preferred_element_type=F32)
        m_ref[idx] = m_new

    @pl.when(j == 0)
    def _():
        for idx in range(2):
            s = lax.dot_general(q_ref[:, idx * hd:(idx + 1) * hd], kx_ref[:, idx * hd:(idx + 1) * hd], nt,
                                preferred_element_type=F32)
            m0 = jnp.max(s, axis=-1, keepdims=True)
            p = jnp.exp2(s - m0)
            m_ref[idx] = m0
            l_ref[idx] = jnp.sum(p, axis=-1, keepdims=True)
            acc_ref[idx] = jnp.dot(p.astype(BF16), vx_ref[...], preferred_element_type=F32)
        if use_latent:
            lane = lax.broadcasted_iota(I32, qn_ref.shape, 1)
            for idx in range(2):
                head = 2 * pl.program_id(1) + idx
                q_sq = jnp.sum(jnp.where(lane == head, qn_ref[...], 0.0), axis=-1, keepdims=True)
                stat_ref[idx] = jnp.sqrt(jnp.max(q_sq))
                stat_ref[2 + idx] = jnp.min(m_ref[idx])

    if use_latent:
        b, h = pl.program_id(0), pl.program_id(1)
        excess = []
        for idx in range(2):
            k_norm = kmax_ref[(b * pl.num_programs(3) + j) * (2 * DIFF_HEADS) + 2 * h + idx]
            excess.append(stat_ref[idx] * k_norm - stat_ref[2 + idx])
        safe = jnp.maximum(excess[0], excess[1]) <= DIFF_SAFE_EXCESS

        @pl.when(safe)
        def _():
            for idx in range(2):
                qh = q_ref[:, idx * hd:(idx + 1) * hd]
                shift = m_ref[idx]
                ps = []
                lsum = jnp.zeros_like(shift)
                for c in range(k_ref.shape[0] // sub):
                    rs = slice(c * sub, (c + 1) * sub)
                    s = lax.dot_general(qh, k_ref[rs, idx * hd:(idx + 1) * hd], nt, preferred_element_type=F32)
                    p = jnp.exp2(s - shift)
                    lsum = lsum + jnp.sum(p, axis=-1, keepdims=True)
                    ps.append(p.astype(BF16))
                l_ref[idx] += lsum
                acc_ref[idx] += jnp.dot(jnp.concatenate(ps, axis=1), v_ref[...], preferred_element_type=F32)

        @pl.when(jnp.logical_not(safe))
        def _():
            for idx in range(2):
                update(idx, q_ref[:, idx * hd:(idx + 1) * hd], k_ref[:, idx * hd:(idx + 1) * hd], v_ref[...])
                stat_ref[2 + idx] = jnp.min(m_ref[idx])

    @pl.when(j == pl.num_programs(3) - 1)
    def _():
        lam = lam_ref[...]
        lam_full = (jnp.exp(jnp.sum(lam[0:1] * lam[1:2], axis=-1, keepdims=True))
                    - jnp.exp(jnp.sum(lam[2:3] * lam[3:4], axis=-1, keepdims=True)) + lam_init)
        o = acc_ref[0] / l_ref[0] - lam_full * (acc_ref[1] / l_ref[1])
        ms = jnp.mean(o * o, axis=-1, keepdims=True)
        o_ref[...] = (o * lax.rsqrt(ms + NORM_EPS) * g_ref[...] * (1.0 - lam_init)).astype(o_ref.dtype)


def _diff_attn(qk, qkv, norms, lam, subln_g, lam_init, geom):
    t = qk.shape[0]
    bsz, s, c = geom["b"], geom["s"], geom["c"]
    pw = 2 * HEAD_DIM
    tq = min(1024, s)
    tkb = min(4096, s)
    sub = min(256, tkb)
    nq, nk = s // tq, s // tkb
    ctx_row = geom["n_lat"] // c
    kcol0, vcol0 = DIFF_HEADS, 2 * DIFF_HEADS
    g2 = subln_g.reshape(1, pw)
    scratch = lambda rows: [pltpu.VMEM((2, rows, 1), F32), pltpu.VMEM((2, rows, 1), F32),
                            pltpu.VMEM((2, rows, pw), F32)]
    nh = 2 * DIFF_HEADS
    k_sq = norms[:geom["n_lat"], nh:2 * nh].reshape(bsz, nk, tkb, nh)
    kmax = (jnp.sqrt(jnp.max(k_sq, axis=2)) * 1.001).reshape(-1)
    grid_spec = pltpu.PrefetchScalarGridSpec(
        num_scalar_prefetch=1,
        grid=(bsz, DIFF_HEADS, nq, nk),
        in_specs=[
            pl.BlockSpec((tq, pw), lambda b, h, i, j, km: (b * nq + i, h)),
            pl.BlockSpec((c, pw), lambda b, h, i, j, km: (ctx_row + b, kcol0 + h)),
            pl.BlockSpec((c, pw), lambda b, h, i, j, km: (ctx_row + b, vcol0 + h)),
            pl.BlockSpec((tkb, pw), lambda b, h, i, j, km: (b * nk + j, kcol0 + h)),
            pl.BlockSpec((tkb, pw), lambda b, h, i, j, km: (b * nk + j, vcol0 + h)),
            pl.BlockSpec((tq, LANES), lambda b, h, i, j, km: (b * nq + i, 0)),
            pl.BlockSpec((4, HEAD_DIM), lambda b, h, i, j, km: (0, 0)),
            pl.BlockSpec((1, pw), lambda b, h, i, j, km: (0, 0)),
        ],
        out_specs=pl.BlockSpec((tq, pw), lambda b, h, i, j, km: (b * nq + i, h)),
        scratch_shapes=scratch(tq) + [pltpu.SMEM((4,), F32)],
    )
    o_lat = pl.pallas_call(
        functools.partial(_diff_attn_kernel, use_latent=True, lam_init=lam_init, sub=sub),
        grid_spec=grid_spec,
        out_shape=jax.ShapeDtypeStruct((t, D_MODEL), BF16),
        compiler_params=_cparams("parallel", "parallel", "parallel", "arbitrary"),
        name="diff_attn_latent",
    )(kmax, qk, qk, qkv, qk, qkv, norms, lam, g2)
    return pl.pallas_call(
        functools.partial(_diff_attn_kernel, use_latent=False, lam_init=lam_init, sub=sub),
        grid=(bsz, DIFF_HEADS, 1, 1),
        in_specs=[
            pl.BlockSpec((c, pw), lambda b, h, i, j: (ctx_row + b, h)),
            pl.BlockSpec((c, pw), lambda b, h, i, j: (ctx_row + b, kcol0 + h)),
            pl.BlockSpec((c, pw), lambda b, h, i, j: (ctx_row + b, vcol0 + h)),
            pl.BlockSpec((4, HEAD_DIM), lambda b, h, i, j: (0, 0)),
            pl.BlockSpec((1, pw), lambda b, h, i, j: (0, 0)),
            pl.BlockSpec(memory_space=pl.ANY),
        ],
        out_specs=pl.BlockSpec((c, pw), lambda b, h, i, j: (ctx_row + b, h)),
        out_shape=jax.ShapeDtypeStruct((t, D_MODEL), BF16),
        scratch_shapes=scratch(c),
        input_output_aliases={5: 0},
        compiler_params=_cparams("parallel", "parallel", "parallel", "arbitrary"),
        name="diff_attn_context",
    )(qk, qk, qkv, lam, g2, o_lat)


def _outproj_kernel(*refs, ssd):
    if ssd:
        (xs_ref, yf_ref, yb_ref, z_ref, dsk_ref, ng_ref, w_ref, x_ref, mod_ref, g_ref, rw_ref, rb_ref,
         xo_ref, hp_ref, lg_ref, acc_ref, ssq_ref) = refs
    else:
        (a_ref, w_ref, x_ref, mod_ref, g_ref, rw_ref, rb_ref,
         xo_ref, hp_ref, lg_ref, acc_ref) = refs
    k = pl.program_id(1)

    @pl.when(k == 0)
    def _():
        acc_ref[...] = jnp.zeros_like(acc_ref)
        if ssd:
            ssq_ref[...] = jnp.zeros_like(ssq_ref)

    if ssd:
        y = dsk_ref[...] * xs_ref[...].astype(F32) + yf_ref[...].astype(F32) + yb_ref[...].astype(F32)
        u = y * _silu(z_ref[...].astype(F32))
        ssq_ref[...] += jnp.sum(u * u, axis=-1, keepdims=True)
        a = (u * ng_ref[...]).astype(BF16)
    else:
        a = a_ref[...]
    acc_ref[...] += jnp.dot(a, w_ref[...], preferred_element_type=F32)

    @pl.when(k == pl.num_programs(1) - 1)
    def _():
        y = acc_ref[...]
        if ssd:
            y = y * lax.rsqrt(ssq_ref[...] * (1.0 / SSM_D_INNER) + NORM_EPS)
        xn = x_ref[...] + mod_ref[2:3, :] * y
        xo_ref[...] = xn
        ms = jnp.mean(xn * xn, axis=-1, keepdims=True)
        h = xn * lax.rsqrt(ms + NORM_EPS) * g_ref[...]
        h = h * (1.0 + mod_ref[4:5, :]) + mod_ref[3:4, :]
        hb = h.astype(BF16)
        hl = (h - hb.astype(F32)).astype(BF16)
        r2 = (jnp.dot(hb, rw_ref[...], preferred_element_type=F32)
              + jnp.dot(hl, rw_ref[...], preferred_element_type=F32))
        lg_ref[...] = r2[:, :LANES] + r2[:, LANES:] + rb_ref[...]
        half = D_MODEL // 2
        lo_bits = lax.bitcast_convert_type(hb[:, :half].astype(F32), U32) >> 16
        hi_bits = lax.bitcast_convert_type(hb[:, half:].astype(F32), U32)
        _tile_store(hp_ref, hi_bits | lo_bits, TOKEN_SUBROWS)


def _outproj(a_args, w, xa, modv, gain, r_w2, r_bias, geom, ssd):
    t, d = xa.shape
    kdim = w.shape[0]
    tm, tk = ROW_TILE, 1024
    tpb, nlt = geom["s"] // tm, geom["n_lat"] // tm
    var = lambda i, k: (_variant_of_tile(i, tpb, nlt), 0, 0)
    const = lambda i, k: (0, 0)
    if ssd:
        xbc, yf, yb, zx, dsk, ng = a_args
        zcol = 0
        a_specs = [
            pl.BlockSpec((tm, tk), lambda i, k: (i, k)),
            pl.BlockSpec((tm, tk), lambda i, k: (i, k)),
            pl.BlockSpec((tm, tk), lambda i, k: (i, k)),
            pl.BlockSpec((tm, tk), lambda i, k: (i, zcol + k)),
            pl.BlockSpec((1, tk), lambda i, k: (0, k)),
            pl.BlockSpec((1, tk), lambda i, k: (0, k)),
        ]
        a_in = [xbc, yf, yb, zx, dsk, ng]
        scratch = [pltpu.VMEM((tm, d), F32), pltpu.VMEM((tm, 1), F32)]
    else:
        a_specs = [pl.BlockSpec((tm, tk), lambda i, k: (i, k))]
        a_in = list(a_args)
        scratch = [pltpu.VMEM((tm, d), F32)]
    return pl.pallas_call(
        functools.partial(_outproj_kernel, ssd=ssd),
        grid=(t // tm, kdim // tk),
        in_specs=a_specs + [
            pl.BlockSpec((tk, d), lambda i, k: (k, 0)),
            pl.BlockSpec((tm, d), lambda i, k: (i, 0)),
            pl.BlockSpec((None, 8, d), var),
            pl.BlockSpec((1, d), const),
            pl.BlockSpec((d, 2 * LANES), const),
            pl.BlockSpec((1, LANES), const),
        ],
        out_specs=[
            pl.BlockSpec((tm, d), lambda i, k: (i, 0)),
            pl.BlockSpec((tm * TOKEN_SUBROWS, LANES), lambda i, k: (i, 0)),
            pl.BlockSpec((tm, LANES), lambda i, k: (i, 0)),
        ],
        out_shape=[
            jax.ShapeDtypeStruct((t, d), F32),
            jax.ShapeDtypeStruct((t * TOKEN_SUBROWS, LANES), U32),
            jax.ShapeDtypeStruct((t, LANES), F32),
        ],
        scratch_shapes=scratch,
        compiler_params=_cparams("parallel", "arbitrary"),
        name="outproj_residual_norm",
    )(*a_in, w, xa, modv, gain.reshape(1, d), r_w2, r_bias)


ROUTE_E0, ROUTE_E1, ROUTE_G0, ROUTE_G1, ROUTE_R0, ROUTE_R1 = range(6)


def _route_kernel(lg_ref, o_ref, cnt_ref, carry_ref):
    rows = lg_ref.shape[0]

    @pl.when(pl.program_id(0) == 0)
    def _():
        carry_ref[...] = jnp.zeros_like(carry_ref)

    lg = lg_ref[...]
    lane = lax.broadcasted_iota(I32, (rows, LANES), 1)
    big = jnp.int32(LANES)

    def first_argmax(v, vmax):
        return jnp.min(jnp.where(v == vmax, lane, big), axis=-1, keepdims=True)

    gl = jnp.where(lane < MOE_GROUPS, lg, NEG)
    gmax = jnp.max(gl, axis=-1, keepdims=True)
    gsum = jnp.sum(jnp.exp(gl - gmax), axis=-1, keepdims=True)
    g_sel = first_argmax(gl, gmax)
    g_p = 1.0 / gsum
    e_lo = MOE_GROUPS + g_sel * MOE_EPG
    el = jnp.where((lane >= e_lo) & (lane < e_lo + MOE_EPG), lg, NEG)
    emax = jnp.max(el, axis=-1, keepdims=True)
    esum = jnp.sum(jnp.exp(el - emax), axis=-1, keepdims=True)
    l0 = first_argmax(el, emax)
    el2 = jnp.where(lane == l0, NEG, el)
    emax2 = jnp.max(el2, axis=-1, keepdims=True)
    l1 = first_argmax(el2, emax2)
    p0 = 1.0 / esum
    p1 = jnp.exp(emax2 - emax) / esum
    gate0 = g_p * p0 / (p0 + p1)
    gate1 = g_p * p1 / (p0 + p1)
    e0 = l0 - MOE_GROUPS
    e1 = l1 - MOE_GROUPS
    oh0 = lane == e0
    oh1 = lane == e1
    hits = oh0.astype(F32) + oh1.astype(F32)
    ri = lax.broadcasted_iota(I32, (rows, rows), 0)
    ci = lax.broadcasted_iota(I32, (rows, rows), 1)
    before = (ci < ri).astype(BF16)
    prior = jnp.dot(before, hits.astype(BF16), preferred_element_type=F32) + carry_ref[0:1, :]
    r0 = jnp.sum(jnp.where(oh0, prior, 0.0), axis=-1, keepdims=True)
    r1 = jnp.sum(jnp.where(oh1, prior, 0.0), axis=-1, keepdims=True)
    carry = carry_ref[0:1, :] + jnp.sum(hits, axis=0, keepdims=True)
    carry_ref[0:1, :] = carry
    cnt_ref[...] = jnp.broadcast_to(carry, cnt_ref.shape)
    rec = jnp.where(lane == ROUTE_E0, e0.astype(F32), 0.0)
    rec = jnp.where(lane == ROUTE_E1, e1.astype(F32), rec)
    rec = jnp.where(lane == ROUTE_G0, gate0, rec)
    rec = jnp.where(lane == ROUTE_G1, gate1, rec)
    rec = jnp.where(lane == ROUTE_R0, r0, rec)
    rec = jnp.where(lane == ROUTE_R1, r1, rec)
    o_ref[...] = rec


def _route(logits):
    t = logits.shape[0]
    rows = 512
    return pl.pallas_call(
        _route_kernel,
        grid=(t // rows,),
        in_specs=[pl.BlockSpec((rows, LANES), lambda i: (i, 0))],
        out_specs=[pl.BlockSpec((rows, LANES), lambda i: (i, 0)), pl.BlockSpec((8, LANES), lambda i: (0, 0))],
        out_shape=[jax.ShapeDtypeStruct((t, LANES), F32), jax.ShapeDtypeStruct((8, LANES), F32)],
        scratch_shapes=[pltpu.VMEM((8, LANES), F32)],
        compiler_params=_cparams("arbitrary"),
        name="moe_route",
    )(logits)


def _gather_tiles(idx_smem, islot, src_hbm, dst_ref, sem, n):
    sub = dst_ref.shape[0] // n
    for r in range(n):
        off = pl.multiple_of(idx_smem[islot, r], sub)
        pltpu.make_async_copy(src_hbm.at[pl.ds(off, sub)], dst_ref.at[pl.ds(r * sub, sub)], sem).start()


def _whole_buffer_copy(src_hbm, buf, sem, n):
    del n
    return pltpu.make_async_copy(src_hbm.at[pl.ds(0, buf.shape[0])], buf, sem)


def _gather_step(step, nsteps, idx_hbm, idx_smem, idx_sem, src_hbm, bufs, row_sems, n):
    def idx_copy(s, slot):
        return pltpu.make_async_copy(idx_hbm.at[s], idx_smem.at[slot], idx_sem.at[slot])

    last = nsteps - 1

    @pl.when(step == 0)
    def _():
        idx_copy(0, 0).start()
        idx_copy(0, 0).wait()
        _gather_tiles(idx_smem, 0, src_hbm, bufs.at[0], row_sems.at[0], n)
        idx_copy(jnp.minimum(1, last), 1).start()

    nxt_i, nxt_b = (step + 1) % 3, (step + 1) % 2
    idx_copy(jnp.minimum(step + 1, last), nxt_i).wait()
    _gather_tiles(idx_smem, nxt_i, src_hbm, bufs.at[nxt_b], row_sems.at[nxt_b], n)
    idx_copy(jnp.minimum(step + 2, last), (step + 2) % 3).start()
    _whole_buffer_copy(src_hbm, bufs.at[step % 2], row_sems.at[step % 2], n).wait()


def _gather_drain(step, nsteps, idx_hbm, idx_smem, idx_sem, src_hbm, bufs, row_sems, n):
    @pl.when(step == nsteps - 1)
    def _():
        nxt_b, nxt_i = (step + 1) % 2, (step + 2) % 3
        _whole_buffer_copy(src_hbm, bufs.at[nxt_b], row_sems.at[nxt_b], n).wait()
        pltpu.make_async_copy(idx_hbm.at[0], idx_smem.at[nxt_i], idx_sem.at[nxt_i]).wait()


def _untile(buf, slot, first, n, sub):
    return jnp.concatenate([buf[slot, pl.ds(first * sub + c, n, stride=sub), :] for c in range(sub)], axis=1)


def _tile_store(ref, val, sub):
    n = val.shape[0]
    for c in range(sub):
        ref[pl.ds(c, n, stride=sub), :] = val[:, c * LANES:(c + 1) * LANES]


def _expert_kernel(bexp_ref, nused_ref, idx_hbm, tok_hbm, wg32_ref, wu32_ref, wd32_ref, y_ref,
                   idx_smem, xbuf, idx_sem, row_sems, wg_ref, wu_ref, wd_ref):
    b = pl.program_id(0)
    nused = nused_ref[0]

    @pl.when((b < nused) & ((b == 0) | (bexp_ref[b] != bexp_ref[jnp.maximum(b - 1, 0)])))
    def _():
        wg_ref[...] = wg32_ref[...].astype(BF16)
        wu_ref[...] = wu32_ref[...].astype(BF16)
        wd_ref[...] = wd32_ref[...].astype(BF16)

    @pl.when(b < nused)
    def _():
        rows = MOE_ROWS
        gather_args = (b, nused, idx_hbm, idx_smem, idx_sem, tok_hbm, xbuf, row_sems, rows)
        _gather_step(*gather_args)
        w = _untile(xbuf, b % 2, 0, rows, TOKEN_SUBROWS)
        x_lo = lax.bitcast_convert_type(w << 16, F32).astype(BF16)
        x_hi = lax.bitcast_convert_type(w & jnp.uint32(0xFFFF0000), F32).astype(BF16)
        half = D_MODEL // 2
        gate = (jnp.dot(x_lo, wg_ref[:half, :], preferred_element_type=F32)
                + jnp.dot(x_hi, wg_ref[half:, :], preferred_element_type=F32))
        up = (jnp.dot(x_lo, wu_ref[:half, :], preferred_element_type=F32)
              + jnp.dot(x_hi, wu_ref[half:, :], preferred_element_type=F32))
        hdn = (_silu(gate) * up).astype(BF16)
        y = jnp.dot(hdn, wd_ref[...], preferred_element_type=F32)
        y_ref[...] = y
        _gather_drain(*gather_args)

    @pl.when(b >= nused)
    def _():
        y_ref[...] = jnp.zeros_like(y_ref)


def _experts(block_exp, n_used, slot_tok, tokens_packed, wg, wu, wd, layer):
    n_blocks = slot_tok.shape[0]
    rows = MOE_ROWS
    half = D_MODEL // 2
    sub = TOKEN_SUBROWS
    grid_spec = pltpu.PrefetchScalarGridSpec(
        num_scalar_prefetch=2,
        grid=(n_blocks,),
        in_specs=[
            pl.BlockSpec(memory_space=pl.ANY),
            pl.BlockSpec(memory_space=pl.ANY),
            pl.BlockSpec((None, None, D_MODEL, MOE_D_FF), lambda b, be, nu: (layer, be[b], 0, 0)),
            pl.BlockSpec((None, None, D_MODEL, MOE_D_FF), lambda b, be, nu: (layer, be[b], 0, 0)),
            pl.BlockSpec((None, None, MOE_D_FF, D_MODEL), lambda b, be, nu: (layer, be[b], 0, 0)),
        ],
        out_specs=pl.BlockSpec((rows, D_MODEL), lambda b, be, nu: (b, 0)),
        scratch_shapes=[
            pltpu.SMEM((3, rows), I32),
            pltpu.VMEM((2, rows * sub, half // sub), U32),
            pltpu.SemaphoreType.DMA((3,)),
            pltpu.SemaphoreType.DMA((2,)),
            pltpu.VMEM((D_MODEL, MOE_D_FF), BF16),
            pltpu.VMEM((D_MODEL, MOE_D_FF), BF16),
            pltpu.VMEM((MOE_D_FF, D_MODEL), BF16),
        ],
    )
    return pl.pallas_call(
        _expert_kernel,
        grid_spec=grid_spec,
        out_shape=jax.ShapeDtypeStruct((n_blocks * rows, D_MODEL), F32),
        compiler_params=_cparams("arbitrary"),
        name="moe_experts",
    )(block_exp, n_used, slot_tok, tokens_packed, wg, wu, wd)


def _combine_kernel(idx_hbm, ys_hbm, x_ref, mod_ref, rt_ref, o_ref, idx_smem, ybuf, idx_sem, row_sems):
    i = pl.program_id(0)
    n = COMBINE_ROWS
    gather_args = (i, pl.num_programs(0), idx_hbm, idx_smem, idx_sem, ys_hbm, ybuf, row_sems, 2 * n)
    _gather_step(*gather_args)
    y0 = ybuf[i % 2, pl.ds(0, n), :]
    y1 = ybuf[i % 2, pl.ds(n, n), :]
    rt = rt_ref[...]
    moe = rt[:, ROUTE_G0:ROUTE_G0 + 1] * y0 + rt[:, ROUTE_G1:ROUTE_G1 + 1] * y1
    o_ref[...] = x_ref[...] + mod_ref[5:6, :] * moe
    _gather_drain(*gather_args)


def _combine(dest_tiles, ys, xa, modv, route, geom, rows_out):
    d = xa.shape[1]
    t = rows_out
    n = COMBINE_ROWS
    tpb, nlt = geom["s"] // n, geom["n_lat"] // n
    return pl.pallas_call(
        _combine_kernel,
        grid=(t // n,),
        in_specs=[
            pl.BlockSpec(memory_space=pl.ANY),
            pl.BlockSpec(memory_space=pl.ANY),
            pl.BlockSpec((n, d), lambda i: (i, 0)),
            pl.BlockSpec((None, 8, d), lambda i: (_variant_of_tile(i, tpb, nlt), 0, 0)),
            pl.BlockSpec((n, LANES), lambda i: (i, 0)),
        ],
        out_specs=pl.BlockSpec((n, d), lambda i: (i, 0)),
        out_shape=jax.ShapeDtypeStruct((t, d), F32),
        scratch_shapes=[
            pltpu.SMEM((3, 2 * n), I32),
            pltpu.VMEM((2, 2 * n, d), F32),
            pltpu.SemaphoreType.DMA((3,)),
            pltpu.SemaphoreType.DMA((2,)),
        ],
        compiler_params=_cparams("arbitrary"),
        name="moe_combine_residual",
    )(dest_tiles, ys, xa, modv, route)


def _moe(xa, tokens_packed, logits, modv, wg, wu, wd, layer, geom, rows_out):
    t = xa.shape[0]
    rows = MOE_ROWS
    route, counts = _route(logits)
    cnt = counts[0, :MOE_EXPERTS].astype(I32)
    padded = (cnt + rows - 1) // rows * rows
    pad_end = jnp.cumsum(padded)
    pad_start = pad_end - padded
    n_blocks = (2 * t + MOE_EXPERTS * (rows - 1) + rows - 1) // rows
    eid = route[:, ROUTE_E0:ROUTE_E1 + 1].astype(I32)
    rank = route[:, ROUTE_R0:ROUTE_R1 + 1].astype(I32)
    dest = pad_start[eid] + rank
    tok = jnp.broadcast_to(jnp.arange(t, dtype=I32)[:, None], (t, 2))
    slot_tok = jnp.zeros((n_blocks * rows,), I32).at[dest.reshape(-1)].set(
        tok.reshape(-1), unique_indices=True)
    block_start = jnp.arange(n_blocks, dtype=I32) * rows
    block_exp = jnp.minimum(
        jnp.sum((pad_end[None, :] <= block_start[:, None]).astype(I32), axis=1), MOE_EXPERTS - 1)
    n_used = (pad_end[-1:] // rows).astype(I32)
    ys = _experts(block_exp, n_used, slot_tok.reshape(n_blocks, rows) * TOKEN_SUBROWS, tokens_packed, wg, wu, wd,
                  layer)
    n = COMBINE_ROWS
    dest_tiles = dest.reshape(t // n, n, 2).transpose(0, 2, 1).reshape(t // n, 2 * n)
    return _combine(dest_tiles, ys, xa, modv, route, geom, rows_out)


def kernel(x, c, ctx, c_ctx, ada_w, ada_b, norm_g, ssm_w_in, ssm_conv_w, ssm_conv_b, ssm_a_log, ssm_dt_bias, ssm_d, ssm_norm_g, ssm_w_out, win_w_qkv, win_q_g, win_k_g, win_sinks, win_w_out, diff_w_qkv, diff_q_g, diff_k_g, diff_lam, diff_subln_g, diff_w_out, moe_w_group, moe_b_group, moe_w_expert, moe_b_expert, moe_w_gate, moe_w_up, moe_w_down):
    bsz, s, d = x.shape
    n_ctx = ctx.shape[1]
    depth = ada_w.shape[0]
    n_lat = bsz * s
    t = n_lat + bsz * n_ctx
    geom = {"b": bsz, "s": s, "c": n_ctx, "n_lat": n_lat, "t": t}
    assert d == D_MODEL and 1 + bsz <= 8
    assert s % max(ROW_TILE, 2048 if s >= 2048 else ROW_TILE) == 0 and (bsz * n_ctx) % ROW_TILE == 0
    assert n_ctx % 256 == 0 and s % GRID_W == 0

    xa = jnp.concatenate([x.reshape(n_lat, d), ctx.reshape(bsz * n_ctx, d)], axis=0)
    cvecs = jnp.zeros((8, d), F32).at[0].set(c_ctx).at[1:1 + bsz].set(c)
    mods = _modulation(cvecs, ada_w, ada_b)
    mods = mods.reshape(depth, 8, 6, d)
    mods = jnp.concatenate([mods, jnp.zeros((depth, 8, 2, d), F32)], axis=2)
    cos_t, sin_t = _rope_tables(geom)
    scale = HEAD_DIM ** -0.5

    for i in range(depth):
        kind, j = i % N_MIXERS, i // N_MIXERS
        modv = mods[i]
        r_w = jnp.zeros((d, LANES), F32).at[:, :MOE_GROUPS].set(moe_w_group[i])
        r_w = r_w.at[:, MOE_GROUPS:MOE_GROUPS + MOE_EXPERTS].set(moe_w_expert[i])
        r_hi = r_w.astype(BF16)
        r_lo = (r_w - r_hi.astype(F32)).astype(BF16)
        r_b = jnp.zeros((1, LANES), F32).at[0, :MOE_GROUPS].set(moe_b_group[i])
        r_b = r_b.at[0, MOE_GROUPS:MOE_GROUPS + MOE_EXPERTS].set(moe_b_expert[i])
        router = (jnp.concatenate([r_hi, r_lo], axis=1), r_b)

        if kind == 0:
            w_in = ssm_w_in[j].astype(BF16)
            n_main = SSM_D_INNER + SSM_CONV_DIM
            zx, dt_raw = _inproj(xa, modv, norm_g[i, 0], w_in[:, :n_main], w_in[:, n_main:], geom)
            xbc = _ssd_conv(zx, ssm_conv_w[j], ssm_conv_b[j], geom)
            a = -jnp.exp(ssm_a_log[j].astype(F32))
            ys_dir = _ssd_scan(xbc, dt_raw, a, ssm_dt_bias[j].astype(F32), geom)
            dsk = jnp.repeat(ssm_d[j].astype(F32), SSM_HEAD_DIM).reshape(1, SSM_D_INNER)
            a_args = (xbc, ys_dir[0], ys_dir[1], zx, dsk, ssm_norm_g[j].reshape(1, SSM_D_INNER))
            xa, tok_p, logits = _outproj(a_args, ssm_w_out[j].astype(BF16), xa, modv, norm_g[i, 1],
                                         *router, geom, ssd=True)
        elif kind == 1:
            qw, kvw = WIN_HEADS * HEAD_DIM, WIN_KV_HEADS * HEAD_DIM
            qkv = _inproj(xa, modv, norm_g[i, 0], win_w_qkv[j].astype(BF16), None, geom)
            gains = jnp.concatenate([jnp.tile(win_q_g[j], WIN_HEADS), jnp.tile(win_k_g[j], WIN_KV_HEADS)])
            post = jnp.concatenate([jnp.full((qw,), scale, F32), jnp.ones((kvw,), F32)])
            qk, _ = _qk_prep(qkv, gains.reshape(1, -1), post.reshape(1, -1), cos_t, sin_t)
            o = _win_attn(qk, qkv, win_sinks[j].astype(F32), geom)
            xa, tok_p, logits = _outproj((o,), win_w_out[j].astype(BF16), xa, modv, norm_g[i, 1],
                                         *router, geom, ssd=False)
        else:
            lam_init = 0.8 - 0.6 * math.exp(-0.3 * i)
            hw = 2 * DIFF_HEADS * HEAD_DIM
            qkv = _inproj(xa, modv, norm_g[i, 0], diff_w_qkv[j].astype(BF16), None, geom)
            gains = jnp.concatenate([jnp.tile(diff_q_g[j], 2 * DIFF_HEADS), jnp.tile(diff_k_g[j], 2 * DIFF_HEADS)])
            post = jnp.concatenate([jnp.full((hw,), scale * math.log2(math.e), F32), jnp.ones((hw,), F32)])
            qk, norms = _qk_prep(qkv, gains.reshape(1, -1), post.reshape(1, -1), cos_t, sin_t)
            o = _diff_attn(qk, qkv, norms, diff_lam[j].astype(F32), diff_subln_g[j].astype(F32), lam_init, geom)
            xa, tok_p, logits = _outproj((o,), diff_w_out[j].astype(BF16), xa, modv, norm_g[i, 1],
                                         *router, geom, ssd=False)

        rows_out = n_lat if i == depth - 1 else t
        xa = _moe(xa, tok_p, logits, modv, moe_w_gate, moe_w_up, moe_w_down, i, geom, rows_out)

    return xa.reshape(bsz, s, d)
```
